```python
import jax, jax.numpy as jnp
from jax import lax
import numpy as np

D_MODEL = 2048
BATCH = 2
SEQ = 16384
DEPTH = 2

GRID_W = 64
CTX_LEN = 256
HEAD_DIM = 128
HALF_DIM = HEAD_DIM // 2
AXIS_FREQS = HEAD_DIM // 4
ROPE_THETA = 10000.0
N_HEADS_A = 8
N_KV_A = 2
N_HEADS_B = 8
N_KV_B = 2
G_A = N_HEADS_A // N_KV_A
G_B = N_HEADS_B // N_KV_B
Q_BLOCK = 128
WINDOW = 128
BAND = Q_BLOCK + 2 * WINDOW
ATTN_SCALE = HEAD_DIM ** -0.5
NEG_INF = -1e30
Q_W = (N_HEADS_A + N_HEADS_B) * HEAD_DIM
KV_A_W = N_KV_A * HEAD_DIM
KV_B_W = N_KV_B * HEAD_DIM
KV_W = 2 * KV_A_W + 2 * KV_B_W
ATTN_WIDTH = Q_W
EVEN_IN = Q_W + KV_W + ATTN_WIDTH
POOL_WIDTH = D_MODEL
POOL_SIZES = (2, 4, 8, 16)
N_POOL_GROUPS = len(POOL_SIZES)
POOL_GROUP = POOL_WIDTH // N_POOL_GROUPS
ODD_IN = 2 * POOL_WIDTH
EPS = 1e-6
N_EVEN = (DEPTH + 1) // 2
N_ODD = DEPTH // 2

kernel_name = 'hybrid_dit_gqa_window_pool_prefix'


def _rmsnorm(x, g):
    xf = x.astype(jnp.float32)
    y = xf * lax.rsqrt(jnp.mean(xf * xf, axis=-1, keepdims=True) + EPS)
    return (y * g.astype(jnp.float32)).astype(x.dtype)


def _modulation(cvec, w, b):
    m = jax.nn.silu(cvec) @ w + b
    m = m[..., None, :]
    return m[..., :D_MODEL], m[..., D_MODEL:2 * D_MODEL], m[..., 2 * D_MODEL:]


def _axial_rope(rows):
    row = jnp.broadcast_to(jnp.arange(rows)[:, None], (rows, GRID_W)).reshape(-1).astype(jnp.float32)
    col = jnp.broadcast_to(jnp.arange(GRID_W)[None, :], (rows, GRID_W)).reshape(-1).astype(jnp.float32)
    inv = ROPE_THETA ** (-jnp.arange(AXIS_FREQS, dtype=jnp.float32) / AXIS_FREQS)
    ang = jnp.concatenate([row[:, None] * inv, col[:, None] * inv], axis=-1)
    return jnp.cos(ang), jnp.sin(ang)


def _rope(x, cos, sin):
    shape = (cos.shape[0],) + (1,) * (x.ndim - 3) + (cos.shape[1],)
    cos = cos.reshape(shape)
    sin = sin.reshape(shape)
    xf = x.astype(jnp.float32)
    x1, x2 = xf[..., :HALF_DIM], xf[..., HALF_DIM:]
    return jnp.concatenate([x1 * cos - x2 * sin, x2 * cos + x1 * sin], axis=-1).astype(x.dtype)


def _split_kv(kv):
    B, L = kv.shape[:2]
    kA = kv[..., :KV_A_W].reshape(B, L, N_KV_A, HEAD_DIM)
    vA = kv[..., KV_A_W:2 * KV_A_W].reshape(B, L, N_KV_A, HEAD_DIM)
    kB = kv[..., 2 * KV_A_W:2 * KV_A_W + KV_B_W].reshape(B, L, N_KV_B, HEAD_DIM)
    vB = kv[..., 2 * KV_A_W + KV_B_W:].reshape(B, L, N_KV_B, HEAD_DIM)
    return kA, vA, kB, vB


def _split_q(q):
    B, L = q.shape[:2]
    q = q.reshape(B, L, N_HEADS_A + N_HEADS_B, HEAD_DIM)
    qA = q[:, :, :N_HEADS_A].reshape(B, L, N_KV_A, G_A, HEAD_DIM)
    qB = q[:, :, N_HEADS_A:].reshape(B, L, N_KV_B, G_B, HEAD_DIM)
    return qA, qB


def _dense_attention(q, k, v):
    B, L = q.shape[:2]
    nb = L // Q_BLOCK

    def block(n):
        qb = lax.dynamic_slice_in_dim(q, n * Q_BLOCK, Q_BLOCK, axis=1)
        s = jnp.einsum('bqkgd,bskd->bkgqs', qb, k).astype(jnp.float32) * ATTN_SCALE
        p = jax.nn.softmax(s, axis=-1).astype(v.dtype)
        return jnp.einsum('bkgqs,bskd->bqkgd', p, v)

    o = lax.map(block, jnp.arange(nb))
    return jnp.moveaxis(o, 0, 1).reshape(B, L, -1)


def _window_attention(q, k, v, kc, vc, sink):
    B, L, KV, G, _ = q.shape
    C = kc.shape[1]
    nb = L // Q_BLOCK
    pad = ((0, 0), (WINDOW, WINDOW), (0, 0), (0, 0))
    kp = jnp.pad(k, pad)
    vp = jnp.pad(v, pad)
    sink_l = sink.astype(jnp.float32).reshape(1, KV, G, 1, 1)

    def block(n):
        start = n * Q_BLOCK
        qb = lax.dynamic_slice_in_dim(q, start, Q_BLOCK, axis=1)
        kb = lax.dynamic_slice_in_dim(kp, start, BAND, axis=1)
        vb = lax.dynamic_slice_in_dim(vp, start, BAND, axis=1)
        qpos = start + jnp.arange(Q_BLOCK)
        kpos = start - WINDOW + jnp.arange(BAND)
        valid = (jnp.abs(kpos[None, :] - qpos[:, None]) <= WINDOW) & (kpos >= 0)[None, :] & (kpos < L)[None, :]
        s_band = jnp.einsum('bqkgd,bjkd->bkgqj', qb, kb).astype(jnp.float32) * ATTN_SCALE
        s_band = jnp.where(valid, s_band, NEG_INF)
        s_ctx = jnp.einsum('bqkgd,bckd->bkgqc', qb, kc).astype(jnp.float32) * ATTN_SCALE
        s_sink = jnp.broadcast_to(sink_l, s_ctx.shape[:-1] + (1,))
        p = jax.nn.softmax(jnp.concatenate([s_band, s_ctx, s_sink], axis=-1), axis=-1).astype(v.dtype)
        return (jnp.einsum('bkgqj,bjkd->bqkgd', p[..., :BAND], vb)
                + jnp.einsum('bkgqc,bckd->bqkgd', p[..., BAND:BAND + C], vc))

    o = lax.map(block, jnp.arange(nb))
    return jnp.moveaxis(o, 0, 1).reshape(B, L, -1)


def _ctx_attention(q, k, v, sink=None):
    B, C = q.shape[:2]
    s = jnp.einsum('bqkgd,bckd->bkgqc', q, k).astype(jnp.float32) * ATTN_SCALE
    if sink is not None:
        KV, G = q.shape[2], q.shape[3]
        s_sink = jnp.broadcast_to(sink.astype(jnp.float32).reshape(1, KV, G, 1, 1), s.shape[:-1] + (1,))
        p = jax.nn.softmax(jnp.concatenate([s, s_sink], axis=-1), axis=-1)[..., :C]
    else:
        p = jax.nn.softmax(s, axis=-1)
    return jnp.einsum('bkgqc,bckd->bqkgd', p.astype(v.dtype), v).reshape(B, C, -1)


def _attention_layer(x, ctx, cos, sin, c, c_ctx, mod_w, mod_b, pre_g, post_g, w_in, q_norm, k_norm, sink, w_out, need_ctx_out):
    shift, scale, gate = _modulation(c, mod_w, mod_b)
    shift_c, scale_c, gate_c = _modulation(c_ctx, mod_w, mod_b)
    h = _rmsnorm(x, pre_g) * (1 + scale) + shift
    hc = _rmsnorm(ctx, pre_g) * (1 + scale_c) + shift_c

    proj = h @ w_in
    qA, qB = _split_q(proj[..., :Q_W])
    kA, vA, kB, vB = _split_kv(proj[..., Q_W:Q_W + KV_W])
    z = proj[..., Q_W + KV_W:]

    kA_c, vA_c, kB_c, vB_c = _split_kv(hc @ w_in[:, Q_W:Q_W + KV_W])
    kA_c = _rmsnorm(kA_c, k_norm)

    qA = _rope(_rmsnorm(qA, q_norm), cos, sin)
    kA = _rope(_rmsnorm(kA, k_norm), cos, sin)
    outA = _dense_attention(qA, jnp.concatenate([kA, kA_c], axis=1), jnp.concatenate([vA, vA_c], axis=1))

    outB = _window_attention(_rope(qB, cos, sin), _rope(kB, cos, sin), vB, kB_c, vB_c, sink)

    y = jnp.concatenate([outA, outB], axis=-1) * jax.nn.silu(z)
    x = x + gate * _rmsnorm(y @ w_out, post_g)

    if need_ctx_out:
        qA_c, qB_c = _split_q(hc @ w_in[:, :Q_W])
        z_c = hc @ w_in[:, Q_W + KV_W:]
        oA_c = _ctx_attention(_rmsnorm(qA_c, q_norm), kA_c, vA_c)
        oB_c = _ctx_attention(qB_c, kB_c, vB_c, sink)
        yc = jnp.concatenate([oA_c, oB_c], axis=-1) * jax.nn.silu(z_c)
        ctx = ctx + gate_c * _rmsnorm(yc @ w_out, post_g)
    return x, ctx


def _multiscale_pool(u, pool_w, pool_scale):
    B, L, _ = u.shape
    uf = u.astype(jnp.float32)
    cs = jnp.concatenate([jnp.zeros_like(uf[:, :1]), lax.cumsum(uf, axis=1)], axis=1)
    t = jnp.arange(L)
    outs = []
    for g, w in enumerate(POOL_SIZES):
        lo = jnp.clip(t - w // 2, 0, L)
        hi = jnp.clip(t + w // 2, 0, L)
        csg = cs[..., g * POOL_GROUP:(g + 1) * POOL_GROUP]
        mean = (jnp.take(csg, hi, axis=1) - jnp.take(csg, lo, axis=1)) / (hi - lo).astype(jnp.float32)[:, None]
        outs.append(mean - uf[..., g * POOL_GROUP:(g + 1) * POOL_GROUP])
    pooled = jnp.stack(outs, axis=2).astype(u.dtype)
    mixed = jnp.einsum('blgc,gcd->blgd', pooled, pool_w).reshape(B, L, POOL_WIDTH)
    return mixed * pool_scale


def _pool_layer(x, ctx, c, c_ctx, mod_w, mod_b, pre_g, post_g, w_in, pool_w, pool_scale, w_out, need_ctx_out):
    shift, scale, gate = _modulation(c, mod_w, mod_b)
    h = _rmsnorm(x, pre_g) * (1 + scale) + shift
    proj = h @ w_in
    y = _multiscale_pool(proj[..., :POOL_WIDTH], pool_w, pool_scale) * jax.nn.silu(proj[..., POOL_WIDTH:])
    x = x + gate * _rmsnorm(y @ w_out, post_g)
    if need_ctx_out:
        shift_c, scale_c, gate_c = _modulation(c_ctx, mod_w, mod_b)
        hc = _rmsnorm(ctx, pre_g) * (1 + scale_c) + shift_c
        pc = hc @ w_in
        yc = _multiscale_pool(pc[..., :POOL_WIDTH], pool_w, pool_scale) * jax.nn.silu(pc[..., POOL_WIDTH:])
        ctx = ctx + gate_c * _rmsnorm(yc @ w_out, post_g)
    return x, ctx


def setup_inputs(seed: int = 0) -> dict:
    key = jax.random.key(seed)
    ks = jax.random.split(key, 24)
    f32 = jnp.float32

    def nrm(k, shape, s):
        return jax.random.normal(k, shape, f32) * s

    return {
        'x': nrm(ks[0], (BATCH, SEQ, D_MODEL), 1.0),
        'c': nrm(ks[1], (BATCH, D_MODEL), 1.0),
        'ctx': nrm(ks[2], (BATCH, CTX_LEN, D_MODEL), 1.0),
        'c_ctx': nrm(ks[3], (D_MODEL,), 1.0),
        'ev_mod_w': nrm(ks[4], (N_EVEN, D_MODEL, 3 * D_MODEL), 0.5 * D_MODEL ** -0.5),
        'ev_mod_b': nrm(ks[5], (N_EVEN, 3 * D_MODEL), 0.02),
        'ev_pre_g': 1.0 + nrm(ks[6], (N_EVEN, D_MODEL), 0.05),
        'ev_post_g': 1.0 + nrm(ks[7], (N_EVEN, D_MODEL), 0.05),
        'ev_w_in': nrm(ks[8], (N_EVEN, D_MODEL, EVEN_IN), D_MODEL ** -0.5),
        'ev_q_norm': 1.0 + nrm(ks[9], (N_EVEN, HEAD_DIM), 0.05),
        'ev_k_norm': 1.0 + nrm(ks[10], (N_EVEN, HEAD_DIM), 0.05),
        'ev_sink': nrm(ks[11], (N_EVEN, N_HEADS_B), 0.5),
        'ev_w_out': nrm(ks[12], (N_EVEN, ATTN_WIDTH, D_MODEL), ATTN_WIDTH ** -0.5),
        'od_mod_w': nrm(ks[13], (N_ODD, D_MODEL, 3 * D_MODEL), 0.5 * D_MODEL ** -0.5),
        'od_mod_b': nrm(ks[14], (N_ODD, 3 * D_MODEL), 0.02),
        'od_pre_g': 1.0 + nrm(ks[15], (N_ODD, D_MODEL), 0.05),
        'od_post_g': 1.0 + nrm(ks[16], (N_ODD, D_MODEL), 0.05),
        'od_w_in': nrm(ks[17], (N_ODD, D_MODEL, ODD_IN), D_MODEL ** -0.5),
        'od_pool_w': nrm(ks[18], (N_ODD, N_POOL_GROUPS, POOL_GROUP, POOL_GROUP), POOL_GROUP ** -0.5),
        'od_pool_scale': 1.0 + nrm(ks[19], (N_ODD, POOL_WIDTH), 0.1),
        'od_w_out': nrm(ks[20], (N_ODD, POOL_WIDTH, D_MODEL), POOL_WIDTH ** -0.5),
    }


def reference(x, c, ctx, c_ctx, ev_mod_w, ev_mod_b, ev_pre_g, ev_post_g, ev_w_in, ev_q_norm, ev_k_norm, ev_sink, ev_w_out,
              od_mod_w, od_mod_b, od_pre_g, od_post_g, od_w_in, od_pool_w, od_pool_scale, od_w_out):
    ROWS = x.shape[1] // GRID_W
    cos, sin = _axial_rope(ROWS)
    for i in range(DEPTH):
        need_ctx_out = any(j % 2 == 0 for j in range(i + 1, DEPTH))
        if i % 2 == 0:
            e = i // 2
            x, ctx = _attention_layer(x, ctx, cos, sin, c, c_ctx, ev_mod_w[e], ev_mod_b[e], ev_pre_g[e], ev_post_g[e],
                                      ev_w_in[e], ev_q_norm[e], ev_k_norm[e], ev_sink[e], ev_w_out[e], need_ctx_out)
        else:
            o = i // 2
            x, ctx = _pool_layer(x, ctx, c, c_ctx, od_mod_w[o], od_mod_b[o], od_pre_g[o], od_post_g[o],
                                 od_w_in[o], od_pool_w[o], od_pool_scale[o], od_w_out[o], need_ctx_out)
    return x
```

```python
import functools
import math

import jax
import jax.numpy as jnp
from jax import lax
from jax.experimental import pallas as pl
from jax.experimental.pallas import tpu as pltpu

D_MODEL = 2048
GRID_W = 64
HEAD_DIM = 128
AXIS_FREQS = HEAD_DIM // 4
ROPE_THETA = 10000.0
N_HEADS_A = 8
N_KV_A = 2
N_HEADS_B = 8
N_KV_B = 2
GROUP = N_HEADS_A // N_KV_A
WINDOW = 128
ATTN_SCALE = HEAD_DIM ** -0.5
NEG_INF = -1e30
Q_W = (N_HEADS_A + N_HEADS_B) * HEAD_DIM
KV_A_W = N_KV_A * HEAD_DIM
KV_B_W = N_KV_B * HEAD_DIM
KV_W = 2 * KV_A_W + 2 * KV_B_W
POOL_SIZES = (2, 4, 8, 16)
POOL_GROUP = D_MODEL // len(POOL_SIZES)
POOL_HALO = 8
EPS = 1e-6

LANES = 128
SUBLANES = 8
VMEM_LIMIT_BYTES = 56 * 1024 * 1024

MOD_ROWS = SUBLANES
MOD_TN = 512
ROW_TILE = 512
SUB_ROWS = 256
COL_CHUNK = 512
ATTN_TQ = 256
ATTN_TK = 512
EXP2_SCALE = ATTN_SCALE * math.log2(math.e)

OP_RAW, OP_ROPE, OP_QNORM_ROPE, OP_KNORM_ROPE, OP_KNORM = range(5)

F32 = jnp.float32
BF16 = jnp.bfloat16


def _params(n_axes):
    return pltpu.CompilerParams(
        dimension_semantics=("arbitrary",) * n_axes,
        vmem_limit_bytes=VMEM_LIMIT_BYTES,
    )


def _resident(block_shape, index_map):
    return pl.BlockSpec(block_shape, index_map, pipeline_mode=pl.Buffered(1))


def _silu(v):
    return v * jax.nn.sigmoid(v)


def _rms(v, gain):
    ms = jnp.mean(v * v, axis=1, keepdims=True)
    return (v * lax.rsqrt(ms + EPS)) * gain


def _mod_kernel(c_ref, w_ref, b_ref, o_ref):
    a = _silu(c_ref[...]).astype(BF16)
    o_ref[...] = jnp.dot(a, w_ref[...].astype(BF16), preferred_element_type=F32) + b_ref[...]


def _modulation(cvec, w, b):
    n = w.shape[1]
    return pl.pallas_call(
        _mod_kernel,
        grid=(n // MOD_TN,),
        in_specs=[
            pl.BlockSpec((MOD_ROWS, D_MODEL), lambda j: (0, 0)),
            pl.BlockSpec((D_MODEL, MOD_TN), lambda j: (0, j)),
            pl.BlockSpec((1, MOD_TN), lambda j: (0, j)),
        ],
        out_specs=pl.BlockSpec((MOD_ROWS, MOD_TN), lambda j: (0, j)),
        out_shape=jax.ShapeDtypeStruct((MOD_ROWS, n), F32),
        compiler_params=_params(1),
        name="modulation",
    )(cvec, w, b.reshape(1, n))


def _proj_kernel(*refs, head_ops, out_widths, n_sub, sub, use_rope, use_norm):
    x_ref, mod_ref, g_ref, w_ref = refs[:4]
    pos = 4
    if use_rope:
        cos_ref, sin_ref = refs[pos:pos + 2]
        pos += 2
    if use_norm:
        qn_ref, kn_ref = refs[pos:pos + 2]
        pos += 2
    out_refs = refs[pos:]

    mrow = mod_ref[0]
    shift = mrow[:, :D_MODEL]
    gain = g_ref[...] * (1.0 + mrow[:, D_MODEL:2 * D_MODEL])
    n_total = sum(out_widths)
    heads_per_chunk = COL_CHUNK // HEAD_DIM

    place = []
    for o_idx, width in enumerate(out_widths):
        for col in range(0, width, HEAD_DIM):
            place.append((o_idx, col))

    for s in range(n_sub):
        rows = slice(s * sub, (s + 1) * sub)
        xs = x_ref[0, rows, :]
        ms = jnp.mean(xs * xs, axis=1, keepdims=True)
        h = ((xs * lax.rsqrt(ms + EPS)) * gain + shift).astype(BF16)
        if use_rope:
            cs = cos_ref[rows, :]
            sn = sin_ref[rows, :]
        for c in range(n_total // COL_CHUNK):
            acc = jnp.dot(h, w_ref[:, c * COL_CHUNK:(c + 1) * COL_CHUNK],
                          preferred_element_type=F32)
            for hh in range(heads_per_chunk):
                head = c * heads_per_chunk + hh
                t = acc[:, hh * HEAD_DIM:(hh + 1) * HEAD_DIM]
                op = head_ops[head]
                if op == OP_QNORM_ROPE:
                    t = _rms(t, qn_ref[...])
                elif op in (OP_KNORM_ROPE, OP_KNORM):
                    t = _rms(t, kn_ref[...])
                if op in (OP_ROPE, OP_QNORM_ROPE, OP_KNORM_ROPE):
                    t = t * cs + pltpu.roll(t, HEAD_DIM // 2, 1) * sn
                o_idx, col = place[head]
                out_refs[o_idx][0, rows, col:col + HEAD_DIM] = t.astype(out_refs[o_idx].dtype)


def _projection(x, mod3, mod_row, g, w, *, head_ops, out_widths, out_dtypes,
                rope=None, norms=None, name):
    B, L, _ = x.shape
    n_total = w.shape[1]
    assert n_total == sum(out_widths) and n_total % COL_CHUNK == 0
    tm = min(ROW_TILE, L)
    sub = min(SUB_ROWS, tm)
    assert L % tm == 0 and tm % sub == 0
    mod_map = (lambda b, i: (b, 0, 0)) if mod_row is None else (lambda b, i: (mod_row, 0, 0))
    in_specs = [
        pl.BlockSpec((1, tm, D_MODEL), lambda b, i: (b, i, 0)),
        pl.BlockSpec((1, 1, 3 * D_MODEL), mod_map),
        _resident((1, D_MODEL), lambda b, i: (0, 0)),
        _resident((D_MODEL, n_total), lambda b, i: (0, 0)),
    ]
    args = [x, mod3, g.reshape(1, D_MODEL), w]
    if rope is not None:
        in_specs += [pl.BlockSpec((tm, HEAD_DIM), lambda b, i: (i, 0))] * 2
        args += list(rope)
    if norms is not None:
        in_specs += [_resident((1, HEAD_DIM), lambda b, i: (0, 0))] * 2
        args += [n.reshape(1, HEAD_DIM) for n in norms]
    kern = functools.partial(
        _proj_kernel, head_ops=head_ops, out_widths=out_widths, n_sub=tm // sub, sub=sub,
        use_rope=rope is not None, use_norm=norms is not None)
    return pl.pallas_call(
        kern,
        grid=(B, L // tm),
        in_specs=in_specs,
        out_specs=[pl.BlockSpec((1, tm, wd), lambda b, i: (b, i, 0)) for wd in out_widths],
        out_shape=[jax.ShapeDtypeStruct((B, L, wd), dt) for wd, dt in zip(out_widths, out_dtypes)],
        compiler_params=_params(2),
        name=name,
    )(*args)


def _stack_heads(q_ref):
    return jnp.concatenate(
        [q_ref[0, :, g * HEAD_DIM:(g + 1) * HEAD_DIM] for g in range(GROUP)], axis=0)


def _unstack_heads(o, tq):
    return jnp.concatenate([o[g * tq:(g + 1) * tq] for g in range(GROUP)], axis=1)


def _qk(q, k):
    return lax.dot_general(q, k, (((1,), (1,)), ((), ())), preferred_element_type=F32)


def _dense_kernel(q_ref, k_ref, v_ref, kc_ref, vc_ref, z_ref, y_ref, m_sc, l_sc, acc_sc,
                  *, tq, tk, n_steps):
    q = _stack_heads(q_ref)
    m_sc[...] = jnp.full(m_sc.shape, -jnp.inf, F32)
    l_sc[...] = jnp.zeros(l_sc.shape, F32)
    acc_sc[...] = jnp.zeros(acc_sc.shape, F32)

    def attend(k, v):
        s = _qk(q, k)
        m_prev = m_sc[...]
        m_new = jnp.maximum(m_prev, jnp.max(s, axis=1, keepdims=True))
        alpha = jnp.exp2((m_prev - m_new) * EXP2_SCALE)
        p = jnp.exp2((s - m_new) * EXP2_SCALE)
        l_sc[...] = alpha * l_sc[...] + jnp.sum(p, axis=1, keepdims=True)
        acc_sc[...] = alpha * acc_sc[...] + jnp.dot(p.astype(BF16), v, preferred_element_type=F32)
        m_sc[...] = m_new

    def body(j, carry):
        off = pl.multiple_of(j * tk, tk)
        attend(k_ref[0, pl.ds(off, tk), :], v_ref[0, pl.ds(off, tk), :])
        return carry

    lax.fori_loop(0, n_steps, body, 0)
    attend(kc_ref[0], vc_ref[0])
    o = _unstack_heads(acc_sc[...] / l_sc[...], tq)
    y_ref[0] = (o * _silu(z_ref[0])).astype(y_ref.dtype)


def _dense_attention(q, kv, kvc, z):
    B, L, _ = q.shape
    C = kvc.shape[1]
    tq = min(ATTN_TQ, L)
    tk = min(ATTN_TK, L)
    assert L % tq == 0 and L % tk == 0
    rows = GROUP * tq
    gw = GROUP * HEAD_DIM
    k_col, v_col = 0, KV_A_W // HEAD_DIM
    kern = functools.partial(_dense_kernel, tq=tq, tk=tk, n_steps=L // tk)
    return pl.pallas_call(
        kern,
        grid=(B, N_KV_A, L // tq),
        in_specs=[
            pl.BlockSpec((1, tq, gw), lambda b, h, i: (b, i, h)),
            pl.BlockSpec((1, L, HEAD_DIM), lambda b, h, i: (b, 0, k_col + h)),
            pl.BlockSpec((1, L, HEAD_DIM), lambda b, h, i: (b, 0, v_col + h)),
            pl.BlockSpec((1, C, HEAD_DIM), lambda b, h, i: (b, 0, k_col + h)),
            pl.BlockSpec((1, C, HEAD_DIM), lambda b, h, i: (b, 0, v_col + h)),
            pl.BlockSpec((1, tq, gw), lambda b, h, i: (b, i, h)),
        ],
        out_specs=pl.BlockSpec((1, tq, gw), lambda b, h, i: (b, i, h)),
        out_shape=jax.ShapeDtypeStruct((B, L, N_HEADS_A * HEAD_DIM), BF16),
        scratch_shapes=[
            pltpu.VMEM((rows, 1), F32),
            pltpu.VMEM((rows, 1), F32),
            pltpu.VMEM((rows, HEAD_DIM), F32),
        ],
        compiler_params=_params(3),
        name="dense_attention",
    )(q, kv, kv, kvc, kvc, z)


def _window_kernel(sink_ref, q_ref, k_ref, v_ref, kc_ref, vc_ref, z_ref, y_ref, *, tq, seq_len):
    h = pl.program_id(1)
    i = pl.program_id(2)
    rows = GROUP * tq
    band = tq + 2 * WINDOW
    start = i * tq
    kstart = pl.multiple_of(jnp.clip(start - WINDOW, 0, seq_len - band), WINDOW)
    q = _stack_heads(q_ref)
    kb = k_ref[0, pl.ds(kstart, band), :]
    vb = v_ref[0, pl.ds(kstart, band), :]

    s = _qk(q, kb) * ATTN_SCALE
    qpos = start + (lax.broadcasted_iota(jnp.int32, (rows, band), 0) & (tq - 1))
    kpos = kstart + lax.broadcasted_iota(jnp.int32, (rows, band), 1)
    s = jnp.where(jnp.abs(kpos - qpos) <= WINDOW, s, NEG_INF)
    sc = _qk(q, kc_ref[0]) * ATTN_SCALE

    head = lax.broadcasted_iota(jnp.int32, (rows, 1), 0) // tq
    sink = jnp.zeros((rows, 1), F32)
    for g in range(GROUP):
        sink = jnp.where(head == g, sink_ref[h * GROUP + g], sink)

    m = jnp.maximum(jnp.maximum(jnp.max(s, axis=1, keepdims=True),
                                jnp.max(sc, axis=1, keepdims=True)), sink)
    p = jnp.exp(s - m)
    pc = jnp.exp(sc - m)
    denom = (jnp.sum(p, axis=1, keepdims=True) + jnp.sum(pc, axis=1, keepdims=True)
             + jnp.exp(sink - m))
    o = (jnp.dot(p.astype(BF16), vb, preferred_element_type=F32)
         + jnp.dot(pc.astype(BF16), vc_ref[0], preferred_element_type=F32)) / denom
    y_ref[0] = (_unstack_heads(o, tq) * _silu(z_ref[0])).astype(y_ref.dtype)


def _window_attention(q, kv, kvc, z, sink):
    B, L, _ = q.shape
    C = kvc.shape[1]
    tq = min(ATTN_TQ, L - 2 * WINDOW)
    assert L % tq == 0 and tq & (tq - 1) == 0 and L >= tq + 2 * WINDOW
    gw = GROUP * HEAD_DIM
    q_col = N_HEADS_A // GROUP
    k_col = 2 * KV_A_W // HEAD_DIM
    v_col = k_col + KV_B_W // HEAD_DIM
    kern = functools.partial(_window_kernel, tq=tq, seq_len=L)
    return pl.pallas_call(
        kern,
        grid=(B, N_KV_B, L // tq),
        in_specs=[
            pl.BlockSpec(memory_space=pltpu.SMEM),
            pl.BlockSpec((1, tq, gw), lambda b, h, i: (b, i, q_col + h)),
            pl.BlockSpec((1, L, HEAD_DIM), lambda b, h, i: (b, 0, k_col + h)),
            pl.BlockSpec((1, L, HEAD_DIM), lambda b, h, i: (b, 0, v_col + h)),
            pl.BlockSpec((1, C, HEAD_DIM), lambda b, h, i: (b, 0, k_col + h)),
            pl.BlockSpec((1, C, HEAD_DIM), lambda b, h, i: (b, 0, v_col + h)),
            pl.BlockSpec((1, tq, gw), lambda b, h, i: (b, i, q_col + h)),
        ],
        out_specs=pl.BlockSpec((1, tq, gw), lambda b, h, i: (b, i, h)),
        out_shape=jax.ShapeDtypeStruct((B, L, N_HEADS_B * HEAD_DIM), BF16),
        compiler_params=_params(3),
        name="window_attention",
    )(sink, q, kv, kv, kvc, kvc, z)


def _finish_rows(r, x_rows, gate, post_g):
    return x_rows + gate * _rms(r, post_g)


def _out_kernel(ya_ref, yb_ref, w_ref, x_ref, mod_ref, g_ref, o_ref, *, n_sub, sub):
    gate = mod_ref[0][:, 2 * D_MODEL:]
    for s in range(n_sub):
        rows = slice(s * sub, (s + 1) * sub)
        y = jnp.concatenate([ya_ref[0, rows, :], yb_ref[0, rows, :]], axis=1)
        r = jnp.dot(y, w_ref[...], preferred_element_type=F32)
        o_ref[0, rows, :] = _finish_rows(r, x_ref[0, rows, :], gate, g_ref[...])


def _out_projection(ya, yb, w, x, mod3, post_g):
    B, L, _ = x.shape
    tm = min(ROW_TILE, L)
    sub = min(SUB_ROWS, tm)
    wa = ya.shape[2]
    kern = functools.partial(_out_kernel, n_sub=tm // sub, sub=sub)
    return pl.pallas_call(
        kern,
        grid=(B, L // tm),
        in_specs=[
            pl.BlockSpec((1, tm, wa), lambda b, i: (b, i, 0)),
            pl.BlockSpec((1, tm, yb.shape[2]), lambda b, i: (b, i, 0)),
            _resident(w.shape, lambda b, i: (0, 0)),
            pl.BlockSpec((1, tm, D_MODEL), lambda b, i: (b, i, 0)),
            pl.BlockSpec((1, 1, 3 * D_MODEL), lambda b, i: (b, 0, 0)),
            _resident((1, D_MODEL), lambda b, i: (0, 0)),
        ],
        out_specs=pl.BlockSpec((1, tm, D_MODEL), lambda b, i: (b, i, 0)),
        out_shape=jax.ShapeDtypeStruct((B, L, D_MODEL), F32),
        compiler_params=_params(2),
        name="out_projection",
    )(ya, yb, w, x, mod3, post_g.reshape(1, D_MODEL))


def _pool_kernel(u_ref, up_ref, un_ref, gt_ref, pw_ref, ps_ref, w_ref, x_ref, mod_ref, g_ref,
                 o_ref, ubuf, *, tm, n_sub, sub, seq_len):
    i = pl.program_id(1)
    n_i = pl.num_programs(1)
    ubuf[0:POOL_HALO, :] = jnp.where(i > 0, up_ref[0], 0.0)
    ubuf[POOL_HALO:POOL_HALO + tm, :] = u_ref[0]
    ubuf[POOL_HALO + tm:, :] = jnp.where(i < n_i - 1, un_ref[0], 0.0)
    gate = mod_ref[0][:, 2 * D_MODEL:]

    for s in range(n_sub):
        r0 = s * sub
        t = i * tm + r0 + lax.broadcasted_iota(jnp.int32, (sub, 1), 0)
        mixed = []
        for g, w in enumerate(POOL_SIZES):
            cols = slice(g * POOL_GROUP, (g + 1) * POOL_GROUP)
            half = w // 2
            tot = ubuf[POOL_HALO + r0 - half:POOL_HALO + r0 - half + sub, cols]
            for d in range(-half + 1, half):
                tot = tot + ubuf[POOL_HALO + r0 + d:POOL_HALO + r0 + d + sub, cols]
            cnt = jnp.minimum(t + half, seq_len) - jnp.maximum(t - half, 0)
            pooled = tot / cnt.astype(F32) - ubuf[POOL_HALO + r0:POOL_HALO + r0 + sub, cols]
            mixed.append(jnp.dot(pooled.astype(BF16), pw_ref[g], preferred_element_type=F32))
        y = jnp.concatenate(mixed, axis=1) * ps_ref[...]
        y = (y * _silu(gt_ref[0, r0:r0 + sub, :])).astype(BF16)
        r = jnp.dot(y, w_ref[...], preferred_element_type=F32)
        o_ref[0, r0:r0 + sub, :] = _finish_rows(r, x_ref[0, r0:r0 + sub, :], gate, g_ref[...])


def _pool_tail(u, gt, pool_w, pool_scale, w_out, x, mod3, post_g):
    B, L, _ = x.shape
    tm = min(ROW_TILE, L)
    sub = min(SUB_ROWS, tm)
    hb = tm // POOL_HALO
    n_hb = L // POOL_HALO
    kern = functools.partial(_pool_kernel, tm=tm, n_sub=tm // sub, sub=sub, seq_len=L)
    row_spec = pl.BlockSpec((1, tm, D_MODEL), lambda b, i: (b, i, 0))
    return pl.pallas_call(
        kern,
        grid=(B, L // tm),
        in_specs=[
            row_spec,
            pl.BlockSpec((1, POOL_HALO, D_MODEL), lambda b, i: (b, jnp.maximum(i * hb - 1, 0), 0)),
            pl.BlockSpec((1, POOL_HALO, D_MODEL),
                         lambda b, i: (b, jnp.minimum((i + 1) * hb, n_hb - 1), 0)),
            row_spec,
            _resident(pool_w.shape, lambda b, i: (0, 0, 0)),
            _resident((1, D_MODEL), lambda b, i: (0, 0)),
            _resident(w_out.shape, lambda b, i: (0, 0)),
            row_spec,
            pl.BlockSpec((1, 1, 3 * D_MODEL), lambda b, i: (b, 0, 0)),
            _resident((1, D_MODEL), lambda b, i: (0, 0)),
        ],
        out_specs=row_spec,
        out_shape=jax.ShapeDtypeStruct((B, L, D_MODEL), F32),
        scratch_shapes=[pltpu.VMEM((tm + 2 * POOL_HALO, D_MODEL), F32)],
        compiler_params=_params(2),
        name="pool_tail",
    )(u, u, u, gt, pool_w, pool_scale.reshape(1, D_MODEL), w_out, x, mod3,
      post_g.reshape(1, D_MODEL))


def _rope_tables(seq_len):
    rows = seq_len // GRID_W
    row = jnp.broadcast_to(jnp.arange(rows)[:, None], (rows, GRID_W)).reshape(-1).astype(F32)
    col = jnp.broadcast_to(jnp.arange(GRID_W)[None, :], (rows, GRID_W)).reshape(-1).astype(F32)
    inv = ROPE_THETA ** (-jnp.arange(AXIS_FREQS, dtype=F32) / AXIS_FREQS)
    ang = jnp.concatenate([row[:, None] * inv, col[:, None] * inv], axis=-1)
    cos, sin = jnp.cos(ang), jnp.sin(ang)
    return jnp.concatenate([cos, cos], axis=-1), jnp.concatenate([-sin, sin], axis=-1)


def kernel(x, c, ctx, c_ctx, ev_mod_w, ev_mod_b, ev_pre_g, ev_post_g, ev_w_in, ev_q_norm, ev_k_norm,
           ev_sink, ev_w_out, od_mod_w, od_mod_b, od_pre_g, od_post_g, od_w_in, od_pool_w,
           od_pool_scale, od_w_out):
    B, L, D = x.shape
    assert D == D_MODEL and L % GRID_W == 0
    assert ev_mod_w.shape[0] == 1 and od_mod_w.shape[0] == 1
    assert B + 1 <= MOD_ROWS
    ctx_row = B

    cvec = jnp.zeros((MOD_ROWS, D), F32).at[:B].set(c).at[ctx_row].set(c_ctx)
    mod_e = _modulation(cvec, ev_mod_w[0], ev_mod_b[0]).reshape(MOD_ROWS, 1, 3 * D)
    mod_o = _modulation(cvec, od_mod_w[0], od_mod_b[0]).reshape(MOD_ROWS, 1, 3 * D)

    w_in_e = ev_w_in[0].astype(BF16)
    norms = (ev_q_norm[0], ev_k_norm[0])
    kv_ops = ([OP_KNORM_ROPE] * N_KV_A + [OP_RAW] * N_KV_A + [OP_ROPE] * N_KV_B + [OP_RAW] * N_KV_B)
    head_ops = tuple([OP_QNORM_ROPE] * N_HEADS_A + [OP_ROPE] * N_HEADS_B + kv_ops
                     + [OP_RAW] * (Q_W // HEAD_DIM))
    q, kv, z = _projection(
        x, mod_e, None, ev_pre_g[0], w_in_e, head_ops=head_ops,
        out_widths=(Q_W, KV_W, Q_W), out_dtypes=(BF16, BF16, F32),
        rope=_rope_tables(L), norms=norms, name="in_projection_even")
    ctx_ops = tuple([OP_KNORM] * N_KV_A + [OP_RAW] * (KV_W // HEAD_DIM - N_KV_A))
    (kvc,) = _projection(
        ctx, mod_e, ctx_row, ev_pre_g[0], w_in_e[:, Q_W:Q_W + KV_W], head_ops=ctx_ops,
        out_widths=(KV_W,), out_dtypes=(BF16,), norms=norms, name="in_projection_ctx")

    ya = _dense_attention(q, kv, kvc, z)
    yb = _window_attention(q, kv, kvc, z, ev_sink[0])
    x = _out_projection(ya, yb, ev_w_out[0].astype(BF16), x, mod_e, ev_post_g[0])

    u, gt = _projection(
        x, mod_o, None, od_pre_g[0], od_w_in[0].astype(BF16),
        head_ops=(OP_RAW,) * (2 * D // HEAD_DIM), out_widths=(D, D), out_dtypes=(F32, F32),
        name="in_projection_odd")
    return _pool_tail(u, gt, od_pool_w[0].astype(BF16), od_pool_scale[0], od_w_out[0].astype(BF16),
                      x, mod_o, od_post_g[0])
```

```python
import functools
import math

import jax
import jax.numpy as jnp
from jax import lax
from jax.experimental import pallas as pl
from jax.experimental.pallas import tpu as pltpu

D_MODEL = 2048
GRID_W = 64
HEAD_DIM = 128
AXIS_FREQS = HEAD_DIM // 4
ROPE_THETA = 10000.0
N_HEADS_A = 8
N_KV_A = 2
N_HEADS_B = 8
N_KV_B = 2
GROUP = N_HEADS_A // N_KV_A
WINDOW = 128
ATTN_SCALE = HEAD_DIM ** -0.5
NEG_INF = -1e30
Q_W = (N_HEADS_A + N_HEADS_B) * HEAD_DIM
KV_A_W = N_KV_A * HEAD_DIM
KV_B_W = N_KV_B * HEAD_DIM
KV_W = 2 * KV_A_W + 2 * KV_B_W
POOL_SIZES = (2, 4, 8, 16)
POOL_GROUP = D_MODEL // len(POOL_SIZES)
POOL_HALO = 8
EPS = 1e-6

LANES = 128
SUBLANES = 8
VMEM_LIMIT_BYTES = 56 * 1024 * 1024

MOD_ROWS = SUBLANES
MOD_TN = 512
ROW_TILE = 512
SUB_ROWS = 256
COL_CHUNK = 512
ATTN_TQ = 256
ATTN_TK = 512
DENSE_UNROLL = 4
EXP2_SCALE = ATTN_SCALE * math.log2(math.e)
BOUND_SLACK = 1.0 + 2.0 ** -7
SAFE_EXPONENT = 60.0

OP_RAW, OP_ROPE, OP_QNORM_ROPE, OP_KNORM_ROPE, OP_KNORM = range(5)

F32 = jnp.float32
BF16 = jnp.bfloat16


def _params(n_axes):
    return pltpu.CompilerParams(
        dimension_semantics=("arbitrary",) * n_axes,
        vmem_limit_bytes=VMEM_LIMIT_BYTES,
    )


def _resident(block_shape, index_map):
    return pl.BlockSpec(block_shape, index_map, pipeline_mode=pl.Buffered(1))


def _silu(v):
    return v * jax.nn.sigmoid(v)


def _rms(v, gain):
    ms = jnp.mean(v * v, axis=1, keepdims=True)
    return (v * lax.rsqrt(ms + EPS)) * gain


def _mod_kernel(c_ref, w_ref, b_ref, o_ref):
    a = _silu(c_ref[...]).astype(BF16)
    o_ref[...] = jnp.dot(a, w_ref[...].astype(BF16), preferred_element_type=F32) + b_ref[...]


def _modulation(cvec, w, b):
    n = w.shape[1]
    return pl.pallas_call(
        _mod_kernel,
        grid=(n // MOD_TN,),
        in_specs=[
            pl.BlockSpec((MOD_ROWS, D_MODEL), lambda j: (0, 0)),
            pl.BlockSpec((D_MODEL, MOD_TN), lambda j: (0, j)),
            pl.BlockSpec((1, MOD_TN), lambda j: (0, j)),
        ],
        out_specs=pl.BlockSpec((MOD_ROWS, MOD_TN), lambda j: (0, j)),
        out_shape=jax.ShapeDtypeStruct((MOD_ROWS, n), F32),
        compiler_params=_params(1),
        name="modulation",
    )(cvec, w, b.reshape(1, n))


def _proj_kernel(*refs, head_ops, out_widths, n_sub, sub, use_rope, use_norm):
    x_ref, mod_ref, g_ref, w_ref = refs[:4]
    pos = 4
    if use_rope:
        cos_ref, sin_ref = refs[pos:pos + 2]
        pos += 2
    if use_norm:
        qn_ref, kn_ref = refs[pos:pos + 2]
        pos += 2
    out_refs = refs[pos:]

    mrow = mod_ref[0]
    shift = mrow[:, :D_MODEL]
    gain = g_ref[...] * (1.0 + mrow[:, D_MODEL:2 * D_MODEL])
    n_total = sum(out_widths)
    heads_per_chunk = COL_CHUNK // HEAD_DIM

    place = []
    for o_idx, width in enumerate(out_widths):
        for col in range(0, width, HEAD_DIM):
            place.append((o_idx, col))

    for s in range(n_sub):
        rows = slice(s * sub, (s + 1) * sub)
        xs = x_ref[0, rows, :]
        ms = jnp.mean(xs * xs, axis=1, keepdims=True)
        h = ((xs * lax.rsqrt(ms + EPS)) * gain + shift).astype(BF16)
        if use_rope:
            cs = cos_ref[rows, :]
            sn = sin_ref[rows, :]
        for c in range(n_total // COL_CHUNK):
            acc = jnp.dot(h, w_ref[:, c * COL_CHUNK:(c + 1) * COL_CHUNK],
                          preferred_element_type=F32)
            for hh in range(heads_per_chunk):
                head = c * heads_per_chunk + hh
                t = acc[:, hh * HEAD_DIM:(hh + 1) * HEAD_DIM]
                op = head_ops[head]
                if op == OP_QNORM_ROPE:
                    t = _rms(t, qn_ref[...])
                elif op in (OP_KNORM_ROPE, OP_KNORM):
                    t = _rms(t, kn_ref[...])
                if op in (OP_ROPE, OP_QNORM_ROPE, OP_KNORM_ROPE):
                    t = t * cs + pltpu.roll(t, HEAD_DIM // 2, 1) * sn
                o_idx, col = place[head]
                out_refs[o_idx][0, rows, col:col + HEAD_DIM] = t.astype(out_refs[o_idx].dtype)


def _projection(x, mod3, mod_row, g, w, *, head_ops, out_widths, out_dtypes,
                rope=None, norms=None, name):
    B, L, _ = x.shape
    n_total = w.shape[1]
    assert n_total == sum(out_widths) and n_total % COL_CHUNK == 0
    tm = min(ROW_TILE, L)
    sub = min(SUB_ROWS, tm)
    assert L % tm == 0 and tm % sub == 0
    mod_map = (lambda b, i: (b, 0, 0)) if mod_row is None else (lambda b, i: (mod_row, 0, 0))
    in_specs = [
        pl.BlockSpec((1, tm, D_MODEL), lambda b, i: (b, i, 0)),
        pl.BlockSpec((1, 1, 3 * D_MODEL), mod_map),
        _resident((1, D_MODEL), lambda b, i: (0, 0)),
        _resident((D_MODEL, n_total), lambda b, i: (0, 0)),
    ]
    args = [x, mod3, g.reshape(1, D_MODEL), w]
    if rope is not None:
        in_specs += [pl.BlockSpec((tm, HEAD_DIM), lambda b, i: (i, 0))] * 2
        args += list(rope)
    if norms is not None:
        in_specs += [_resident((1, HEAD_DIM), lambda b, i: (0, 0))] * 2
        args += [n.reshape(1, HEAD_DIM) for n in norms]
    kern = functools.partial(
        _proj_kernel, head_ops=head_ops, out_widths=out_widths, n_sub=tm // sub, sub=sub,
        use_rope=rope is not None, use_norm=norms is not None)
    return pl.pallas_call(
        kern,
        grid=(B, L // tm),
        in_specs=in_specs,
        out_specs=[pl.BlockSpec((1, tm, wd), lambda b, i: (b, i, 0)) for wd in out_widths],
        out_shape=[jax.ShapeDtypeStruct((B, L, wd), dt) for wd, dt in zip(out_widths, out_dtypes)],
        compiler_params=_params(2),
        name=name,
    )(*args)


def _stack_heads(q_ref):
    return jnp.concatenate(
        [q_ref[0, :, g * HEAD_DIM:(g + 1) * HEAD_DIM] for g in range(GROUP)], axis=0)


def _unstack_heads(o, tq):
    return jnp.concatenate([o[g * tq:(g + 1) * tq] for g in range(GROUP)], axis=1)


def _qk(q, k):
    return lax.dot_general(q, k, (((1,), (1,)), ((), ())), preferred_element_type=F32)


def _dense_kernel(q_ref, k_ref, v_ref, kc_ref, vc_ref, z_ref, y_ref,
                  vt_sc, vct_sc, k2_sc, m_sc, l_sc, acc_sc, s_sc, *, tq, tk, n_steps):
    nq = GROUP * tq

    @pl.when(pl.program_id(2) == 0)
    def _():
        def key_norm2(k):
            kf = k.astype(F32)
            return jnp.max(jnp.sum(kf * kf, axis=1, keepdims=True))

        def tbody(j, best):
            off = pl.multiple_of(j * tk, tk)
            vt_sc[j] = v_ref[0, pl.ds(off, tk), :].astype(F32).T.astype(BF16)
            return jnp.maximum(best, key_norm2(k_ref[0, pl.ds(off, tk), :]))

        best = lax.fori_loop(0, n_steps, tbody, key_norm2(kc_ref[0]))
        vct_sc[...] = vc_ref[0].astype(F32).T.astype(BF16)
        k2_sc[...] = jnp.full(k2_sc.shape, best, F32)

    l_sc[...] = jnp.zeros(l_sc.shape, F32)
    acc_sc[...] = jnp.zeros(acc_sc.shape, F32)
    q = [q_ref[0, :, g * HEAD_DIM:(g + 1) * HEAD_DIM] for g in range(GROUP)]

    ones = jnp.ones((SUBLANES, HEAD_DIM), F32)
    k2 = k2_sc[0:1, 0:1]
    bound = []
    for g in range(GROUP):
        qf = q[g].astype(F32)
        q2 = _qk(ones, qf * qf)
        bound.append(jnp.sqrt(q2 * k2) * BOUND_SLACK)
    worst = functools.reduce(jnp.maximum, [jnp.max(b) for b in bound])
    safe = worst * (2.0 * ATTN_SCALE) <= SAFE_EXPONENT

    @pl.when(safe)
    def _():
        q_all = jnp.concatenate(q, axis=0)
        shift = jnp.concatenate(bound, axis=1) * EXP2_SCALE

        def scores(k):
            return _qk(k, q_all)

        def consume(s, vt):
            p = jnp.exp2(s.reshape(-1, SUBLANES, nq) * EXP2_SCALE - shift[None])
            l_sc[...] += jnp.sum(p, axis=0)
            acc_sc[...] += jnp.dot(vt, p.reshape(-1, nq).astype(BF16), preferred_element_type=F32)

        def key_tile(j):
            return k_ref[0, pl.ds(pl.multiple_of(j * tk, tk), tk), :]

        def step(j, slot):
            s_sc[1 - slot] = scores(key_tile(j + 1))
            consume(s_sc[slot], vt_sc[j])

        s_sc[0] = scores(key_tile(0))
        n_loop = (n_steps - 2) // DENSE_UNROLL

        def body(i, carry):
            for u in range(DENSE_UNROLL):
                step(i * DENSE_UNROLL + u, u % 2)
            return carry

        lax.fori_loop(0, n_loop, body, 0)
        for j in range(n_loop * DENSE_UNROLL, n_steps - 1):
            step(j, j % 2)
        s_ctx = scores(kc_ref[0])
        consume(s_sc[(n_steps - 1) % 2], vt_sc[n_steps - 1])
        consume(s_ctx, vct_sc[...])

    @pl.when(jnp.logical_not(safe))
    def _():
        m_sc[...] = jnp.full(m_sc.shape, -jnp.inf, F32)

        def attend(k, vt):
            for g in range(GROUP):
                st = _qk(k, q[g])
                m_prev = m_sc[g]
                m_new = jnp.maximum(m_prev, jnp.max(st, axis=0, keepdims=True))
                alpha = jnp.exp2((m_prev - m_new) * EXP2_SCALE)
                p = jnp.exp2((st - m_new) * EXP2_SCALE)
                cols = slice(g * tq, (g + 1) * tq)
                l_sc[0:1, cols] = alpha * l_sc[0:1, cols] + jnp.sum(p, axis=0, keepdims=True)
                acc_sc[:, cols] = alpha * acc_sc[:, cols] + jnp.dot(vt, p.astype(BF16),
                                                                     preferred_element_type=F32)
                m_sc[g] = m_new

        def body(j, carry):
            off = pl.multiple_of(j * tk, tk)
            attend(k_ref[0, pl.ds(off, tk), :], vt_sc[j])
            return carry

        lax.fori_loop(0, n_steps, body, 0)
        attend(kc_ref[0], vct_sc[...])

    o = (acc_sc[...] / jnp.sum(l_sc[...], axis=0, keepdims=True)).T
    o = jnp.concatenate([o[g * tq:(g + 1) * tq] for g in range(GROUP)], axis=1)
    y_ref[0] = (o * _silu(z_ref[0])).astype(y_ref.dtype)


def _dense_attention(q, kv, kvc, z):
    B, L, _ = q.shape
    C = kvc.shape[1]
    tq = min(ATTN_TQ, L)
    tk = min(ATTN_TK, L)
    assert L % tq == 0 and L % tk == 0 and L // tk >= 2 and DENSE_UNROLL % 2 == 0
    gw = GROUP * HEAD_DIM
    k_col, v_col = 0, KV_A_W // HEAD_DIM
    kern = functools.partial(_dense_kernel, tq=tq, tk=tk, n_steps=L // tk)
    return pl.pallas_call(
        kern,
        grid=(B, N_KV_A, L // tq),
        in_specs=[
            pl.BlockSpec((1, tq, gw), lambda b, h, i: (b, i, h)),
            pl.BlockSpec((1, L, HEAD_DIM), lambda b, h, i: (b, 0, k_col + h)),
            pl.BlockSpec((1, L, HEAD_DIM), lambda b, h, i: (b, 0, v_col + h)),
            pl.BlockSpec((1, C, HEAD_DIM), lambda b, h, i: (b, 0, k_col + h)),
            pl.BlockSpec((1, C, HEAD_DIM), lambda b, h, i: (b, 0, v_col + h)),
            pl.BlockSpec((1, tq, gw), lambda b, h, i: (b, i, h)),
        ],
        out_specs=pl.BlockSpec((1, tq, gw), lambda b, h, i: (b, i, h)),
        out_shape=jax.ShapeDtypeStruct((B, L, N_HEADS_A * HEAD_DIM), BF16),
        scratch_shapes=[
            pltpu.VMEM((L // tk, HEAD_DIM, tk), BF16),
            pltpu.VMEM((HEAD_DIM, C), BF16),
            pltpu.VMEM((SUBLANES, LANES), F32),
            pltpu.VMEM((GROUP, 1, tq), F32),
            pltpu.VMEM((SUBLANES, GROUP * tq), F32),
            pltpu.VMEM((HEAD_DIM, GROUP * tq), F32),
            pltpu.VMEM((2, tk, GROUP * tq), F32),
        ],
        compiler_params=_params(3),
        name="dense_attention",
    )(q, kv, kv, kvc, kvc, z)


def _window_kernel(sink_ref, q_ref, k_ref, v_ref, kc_ref, vc_ref, z_ref, y_ref, *, tq, seq_len):
    h = pl.program_id(1)
    i = pl.program_id(2)
    rows = GROUP * tq
    band = tq + 2 * WINDOW
    start = i * tq
    kstart = pl.multiple_of(jnp.clip(start - WINDOW, 0, seq_len - band), WINDOW)
    q = _stack_heads(q_ref)
    kb = k_ref[0, pl.ds(kstart, band), :]
    vb = v_ref[0, pl.ds(kstart, band), :]

    s = _qk(q, kb) * ATTN_SCALE
    qpos = start + (lax.broadcasted_iota(jnp.int32, (rows, band), 0) & (tq - 1))
    kpos = kstart + lax.broadcasted_iota(jnp.int32, (rows, band), 1)
    s = jnp.where(jnp.abs(kpos - qpos) <= WINDOW, s, NEG_INF)
    sc = _qk(q, kc_ref[0]) * ATTN_SCALE

    head = lax.broadcasted_iota(jnp.int32, (rows, 1), 0) // tq
    sink = jnp.zeros((rows, 1), F32)
    for g in range(GROUP):
        sink = jnp.where(head == g, sink_ref[h * GROUP + g], sink)

    m = jnp.maximum(jnp.maximum(jnp.max(s, axis=1, keepdims=True),
                                jnp.max(sc, axis=1, keepdims=True)), sink)
    p = jnp.exp(s - m)
    pc = jnp.exp(sc - m)
    denom = (jnp.sum(p, axis=1, keepdims=True) + jnp.sum(pc, axis=1, keepdims=True)
             + jnp.exp(sink - m))
    o = (jnp.dot(p.astype(BF16), vb, preferred_element_type=F32)
         + jnp.dot(pc.astype(BF16), vc_ref[0], preferred_element_type=F32)) / denom
    y_ref[0] = (_unstack_heads(o, tq) * _silu(z_ref[0])).astype(y_ref.dtype)


def _window_attention(q, kv, kvc, z, sink):
    B, L, _ = q.shape
    C = kvc.shape[1]
    tq = min(ATTN_TQ, L - 2 * WINDOW)
    assert L % tq == 0 and tq & (tq - 1) == 0 and L >= tq + 2 * WINDOW
    gw = GROUP * HEAD_DIM
    q_col = N_HEADS_A // GROUP
    k_col = 2 * KV_A_W // HEAD_DIM
    v_col = k_col + KV_B_W // HEAD_DIM
    kern = functools.partial(_window_kernel, tq=tq, seq_len=L)
    return pl.pallas_call(
        kern,
        grid=(B, N_KV_B, L // tq),
        in_specs=[
            pl.BlockSpec(memory_space=pltpu.SMEM),
            pl.BlockSpec((1, tq, gw), lambda b, h, i: (b, i, q_col + h)),
            pl.BlockSpec((1, L, HEAD_DIM), lambda b, h, i: (b, 0, k_col + h)),
            pl.BlockSpec((1, L, HEAD_DIM), lambda b, h, i: (b, 0, v_col + h)),
            pl.BlockSpec((1, C, HEAD_DIM), lambda b, h, i: (b, 0, k_col + h)),
            pl.BlockSpec((1, C, HEAD_DIM), lambda b, h, i: (b, 0, v_col + h)),
            pl.BlockSpec((1, tq, gw), lambda b, h, i: (b, i, q_col + h)),
        ],
        out_specs=pl.BlockSpec((1, tq, gw), lambda b, h, i: (b, i, h)),
        out_shape=jax.ShapeDtypeStruct((B, L, N_HEADS_B * HEAD_DIM), BF16),
        compiler_params=_params(3),
        name="window_attention",
    )(sink, q, kv, kv, kvc, kvc, z)


def _finish_rows(r, x_rows, gate, post_g):
    return x_rows + gate * _rms(r, post_g)


def _out_kernel(ya_ref, yb_ref, w_ref, x_ref, mod_ref, g_ref, o_ref, *, n_sub, sub):
    gate = mod_ref[0][:, 2 * D_MODEL:]
    for s in range(n_sub):
        rows = slice(s * sub, (s + 1) * sub)
        y = jnp.concatenate([ya_ref[0, rows, :], yb_ref[0, rows, :]], axis=1)
        r = jnp.dot(y, w_ref[...], preferred_element_type=F32)
        o_ref[0, rows, :] = _finish_rows(r, x_ref[0, rows, :], gate, g_ref[...])


def _out_projection(ya, yb, w, x, mod3, post_g):
    B, L, _ = x.shape
    tm = min(ROW_TILE, L)
    sub = min(SUB_ROWS, tm)
    wa = ya.shape[2]
    kern = functools.partial(_out_kernel, n_sub=tm // sub, sub=sub)
    return pl.pallas_call(
        kern,
        grid=(B, L // tm),
        in_specs=[
            pl.BlockSpec((1, tm, wa), lambda b, i: (b, i, 0)),
            pl.BlockSpec((1, tm, yb.shape[2]), lambda b, i: (b, i, 0)),
            _resident(w.shape, lambda b, i: (0, 0)),
            pl.BlockSpec((1, tm, D_MODEL), lambda b, i: (b, i, 0)),
            pl.BlockSpec((1, 1, 3 * D_MODEL), lambda b, i: (b, 0, 0)),
            _resident((1, D_MODEL), lambda b, i: (0, 0)),
        ],
        out_specs=pl.BlockSpec((1, tm, D_MODEL), lambda b, i: (b, i, 0)),
        out_shape=jax.ShapeDtypeStruct((B, L, D_MODEL), F32),
        compiler_params=_params(2),
        name="out_projection",
    )(ya, yb, w, x, mod3, post_g.reshape(1, D_MODEL))


def _pool_kernel(u_ref, up_ref, un_ref, gt_ref, pw_ref, ps_ref, w_ref, x_ref, mod_ref, g_ref,
                 o_ref, ubuf, *, tm, n_sub, sub, seq_len):
    i = pl.program_id(1)
    n_i = pl.num_programs(1)
    ubuf[0:POOL_HALO, :] = jnp.where(i > 0, up_ref[0], 0.0)
    ubuf[POOL_HALO:POOL_HALO + tm, :] = u_ref[0]
    ubuf[POOL_HALO + tm:, :] = jnp.where(i < n_i - 1, un_ref[0], 0.0)
    gate = mod_ref[0][:, 2 * D_MODEL:]

    for s in range(n_sub):
        r0 = s * sub
        t = i * tm + r0 + lax.broadcasted_iota(jnp.int32, (sub, 1), 0)
        mixed = []
        for g, w in enumerate(POOL_SIZES):
            cols = slice(g * POOL_GROUP, (g + 1) * POOL_GROUP)
            half = w // 2
            tot = ubuf[POOL_HALO + r0 - half:POOL_HALO + r0 - half + sub, cols]
            for d in range(-half + 1, half):
                tot = tot + ubuf[POOL_HALO + r0 + d:POOL_HALO + r0 + d + sub, cols]
            cnt = jnp.minimum(t + half, seq_len) - jnp.maximum(t - half, 0)
            pooled = tot / cnt.astype(F32) - ubuf[POOL_HALO + r0:POOL_HALO + r0 + sub, cols]
            mixed.append(jnp.dot(pooled.astype(BF16), pw_ref[g], preferred_element_type=F32))
        y = jnp.concatenate(mixed, axis=1) * ps_ref[...]
        y = (y * _silu(gt_ref[0, r0:r0 + sub, :])).astype(BF16)
        r = jnp.dot(y, w_ref[...], preferred_element_type=F32)
        o_ref[0, r0:r0 + sub, :] = _finish_rows(r, x_ref[0, r0:r0 + sub, :], gate, g_ref[...])


def _pool_tail(u, gt, pool_w, pool_scale, w_out, x, mod3, post_g):
    B, L, _ = x.shape
    tm = min(ROW_TILE, L)
    sub = min(SUB_ROWS, tm)
    hb = tm // POOL_HALO
    n_hb = L // POOL_HALO
    kern = functools.partial(_pool_kernel, tm=tm, n_sub=tm // sub, sub=sub, seq_len=L)
    row_spec = pl.BlockSpec((1, tm, D_MODEL), lambda b, i: (b, i, 0))
    return pl.pallas_call(
        kern,
        grid=(B, L // tm),
        in_specs=[
            row_spec,
            pl.BlockSpec((1, POOL_HALO, D_MODEL), lambda b, i: (b, jnp.maximum(i * hb - 1, 0), 0)),
            pl.BlockSpec((1, POOL_HALO, D_MODEL),
                         lambda b, i: (b, jnp.minimum((i + 1) * hb, n_hb - 1), 0)),
            row_spec,
            _resident(pool_w.shape, lambda b, i: (0, 0, 0)),
            _resident((1, D_MODEL), lambda b, i: (0, 0)),
            _resident(w_out.shape, lambda b, i: (0, 0)),
            row_spec,
            pl.BlockSpec((1, 1, 3 * D_MODEL), lambda b, i: (b, 0, 0)),
            _resident((1, D_MODEL), lambda b, i: (0, 0)),
        ],
        out_specs=row_spec,
        out_shape=jax.ShapeDtypeStruct((B, L, D_MODEL), F32),
        scratch_shapes=[pltpu.VMEM((tm + 2 * POOL_HALO, D_MODEL), F32)],
        compiler_params=_params(2),
        name="pool_tail",
    )(u, u, u, gt, pool_w, pool_scale.reshape(1, D_MODEL), w_out, x, mod3,
      post_g.reshape(1, D_MODEL))


def _rope_tables(seq_len):
    rows = seq_len // GRID_W
    row = jnp.broadcast_to(jnp.arange(rows)[:, None], (rows, GRID_W)).reshape(-1).astype(F32)
    col = jnp.broadcast_to(jnp.arange(GRID_W)[None, :], (rows, GRID_W)).reshape(-1).astype(F32)
    inv = ROPE_THETA ** (-jnp.arange(AXIS_FREQS, dtype=F32) / AXIS_FREQS)
    ang = jnp.concatenate([row[:, None] * inv, col[:, None] * inv], axis=-1)
    cos, sin = jnp.cos(ang), jnp.sin(ang)
    return jnp.concatenate([cos, cos], axis=-1), jnp.concatenate([-sin, sin], axis=-1)


def kernel(x, c, ctx, c_ctx, ev_mod_w, ev_mod_b, ev_pre_g, ev_post_g, ev_w_in, ev_q_norm, ev_k_norm,
           ev_sink, ev_w_out, od_mod_w, od_mod_b, od_pre_g, od_post_g, od_w_in, od_pool_w,
           od_pool_scale, od_w_out):
    B, L, D = x.shape
    assert D == D_MODEL and L % GRID_W == 0
    assert ev_mod_w.shape[0] == 1 and od_mod_w.shape[0] == 1
    assert B + 1 <= MOD_ROWS
    ctx_row = B

    cvec = jnp.zeros((MOD_ROWS, D), F32).at[:B].set(c).at[ctx_row].set(c_ctx)
    mod_e = _modulation(cvec, ev_mod_w[0], ev_mod_b[0]).reshape(MOD_ROWS, 1, 3 * D)
    mod_o = _modulation(cvec, od_mod_w[0], od_mod_b[0]).reshape(MOD_ROWS, 1, 3 * D)

    w_in_e = ev_w_in[0].astype(BF16)
    norms = (ev_q_norm[0], ev_k_norm[0])
    kv_ops = ([OP_KNORM_ROPE] * N_KV_A + [OP_RAW] * N_KV_A + [OP_ROPE] * N_KV_B + [OP_RAW] * N_KV_B)
    head_ops = tuple([OP_QNORM_ROPE] * N_HEADS_A + [OP_ROPE] * N_HEADS_B + kv_ops
                     + [OP_RAW] * (Q_W // HEAD_DIM))
    q, kv, z = _projection(
        x, mod_e, None, ev_pre_g[0], w_in_e, head_ops=head_ops,
        out_widths=(Q_W, KV_W, Q_W), out_dtypes=(BF16, BF16, F32),
        rope=_rope_tables(L), norms=norms, name="in_projection_even")
    ctx_ops = tuple([OP_KNORM] * N_KV_A + [OP_RAW] * (KV_W // HEAD_DIM - N_KV_A))
    (kvc,) = _projection(
        ctx, mod_e, ctx_row, ev_pre_g[0], w_in_e[:, Q_W:Q_W + KV_W], head_ops=ctx_ops,
        out_widths=(KV_W,), out_dtypes=(BF16,), norms=norms, name="in_projection_ctx")

    ya = _dense_attention(q, kv, kvc, z)
    yb = _window_attention(q, kv, kvc, z, ev_sink[0])
    x = _out_projection(ya, yb, ev_w_out[0].astype(BF16), x, mod_e, ev_post_g[0])

    u, gt = _projection(
        x, mod_o, None, od_pre_g[0], od_w_in[0].astype(BF16),
        head_ops=(OP_RAW,) * (2 * D // HEAD_DIM), out_widths=(D, D), out_dtypes=(F32, F32),
        name="in_projection_odd")
    return _pool_tail(u, gt, od_pool_w[0].astype(BF16), od_pool_scale[0], od_w_out[0].astype(BF16),
                      x, mod_o, od_post_g[0])
```

```python
import functools
import math

import jax
import jax.numpy as jnp
from jax import lax
from jax.experimental import pallas as pl
from jax.experimental.pallas import tpu as pltpu

D_MODEL = 2048
GRID_W = 64
HEAD_DIM = 128
AXIS_FREQS = HEAD_DIM // 4
ROPE_THETA = 10000.0
N_HEADS_A = 8
N_KV_A = 2
N_HEADS_B = 8
N_KV_B = 2
GROUP = N_HEADS_A // N_KV_A
WINDOW = 128
ATTN_SCALE = HEAD_DIM ** -0.5
NEG_INF = -1e30
Q_W = (N_HEADS_A + N_HEADS_B) * HEAD_DIM
KV_A_W = N_KV_A * HEAD_DIM
KV_B_W = N_KV_B * HEAD_DIM
KV_W = 2 * KV_A_W + 2 * KV_B_W
POOL_SIZES = (2, 4, 8, 16)
POOL_GROUP = D_MODEL // len(POOL_SIZES)
POOL_HALO = 8
EPS = 1e-6

LANES = 128
SUBLANES = 8
VMEM_LIMIT_BYTES = 56 * 1024 * 1024

MOD_ROWS = SUBLANES
MOD_TN = 512
ROW_TILE = 512
SUB_ROWS = 256
COL_CHUNK = 512
ATTN_TQ = 512
ATTN_TK = 512
DENSE_UNROLL = 4
WINDOW_TQ = 512
WINDOW_SUB = 128
LOG2_E = math.log2(math.e)
EXP2_SCALE = ATTN_SCALE * LOG2_E
BOUND_SLACK = 1.0 + 2.0 ** -7
SAFE_EXPONENT = 60.0

OP_RAW, OP_ROPE, OP_QNORM_ROPE, OP_KNORM_ROPE, OP_KNORM = range(5)

F32 = jnp.float32
BF16 = jnp.bfloat16


def _params(n_axes):
    return pltpu.CompilerParams(
        dimension_semantics=("arbitrary",) * n_axes,
        vmem_limit_bytes=VMEM_LIMIT_BYTES,
    )


def _resident(block_shape, index_map):
    return pl.BlockSpec(block_shape, index_map, pipeline_mode=pl.Buffered(1))


def _silu(v):
    return v * jax.nn.sigmoid(v)


def _rms(v, gain):
    ms = jnp.mean(v * v, axis=1, keepdims=True)
    return (v * lax.rsqrt(ms + EPS)) * gain


def _mod_kernel(c_ref, w_ref, b_ref, o_ref):
    a = _silu(c_ref[...]).astype(BF16)
    o_ref[...] = jnp.dot(a, w_ref[...].astype(BF16), preferred_element_type=F32) + b_ref[...]


def _modulation(cvec, w, b):
    n = w.shape[1]
    return pl.pallas_call(
        _mod_kernel,
        grid=(n // MOD_TN,),
        in_specs=[
            pl.BlockSpec((MOD_ROWS, D_MODEL), lambda j: (0, 0)),
            pl.BlockSpec((D_MODEL, MOD_TN), lambda j: (0, j)),
            pl.BlockSpec((1, MOD_TN), lambda j: (0, j)),
        ],
        out_specs=pl.BlockSpec((MOD_ROWS, MOD_TN), lambda j: (0, j)),
        out_shape=jax.ShapeDtypeStruct((MOD_ROWS, n), F32),
        compiler_params=_params(1),
        name="modulation",
    )(cvec, w, b.reshape(1, n))


def _proj_kernel(*refs, head_ops, out_widths, n_sub, sub, use_rope, use_norm):
    x_ref, mod_ref, g_ref, w_ref = refs[:4]
    pos = 4
    if use_rope:
        cos_ref, sin_ref = refs[pos:pos + 2]
        pos += 2
    if use_norm:
        qn_ref, kn_ref = refs[pos:pos + 2]
        pos += 2
    out_refs = refs[pos:]

    mrow = mod_ref[0]
    shift = mrow[:, :D_MODEL]
    gain = g_ref[...] * (1.0 + mrow[:, D_MODEL:2 * D_MODEL])
    n_total = sum(out_widths)
    heads_per_chunk = COL_CHUNK // HEAD_DIM

    place = []
    for o_idx, width in enumerate(out_widths):
        for col in range(0, width, HEAD_DIM):
            place.append((o_idx, col))

    for s in range(n_sub):
        rows = slice(s * sub, (s + 1) * sub)
        xs = x_ref[0, rows, :]
        ms = jnp.mean(xs * xs, axis=1, keepdims=True)
        h = ((xs * lax.rsqrt(ms + EPS)) * gain + shift).astype(BF16)
        if use_rope:
            cs = cos_ref[rows, :]
            sn = sin_ref[rows, :]
        for c in range(n_total // COL_CHUNK):
            acc = jnp.dot(h, w_ref[:, c * COL_CHUNK:(c + 1) * COL_CHUNK],
                          preferred_element_type=F32)
            for hh in range(heads_per_chunk):
                head = c * heads_per_chunk + hh
                t = acc[:, hh * HEAD_DIM:(hh + 1) * HEAD_DIM]
                op = head_ops[head]
                if op == OP_QNORM_ROPE:
                    t = _rms(t, qn_ref[...])
                elif op in (OP_KNORM_ROPE, OP_KNORM):
                    t = _rms(t, kn_ref[...])
                if op in (OP_ROPE, OP_QNORM_ROPE, OP_KNORM_ROPE):
                    t = t * cs + pltpu.roll(t, HEAD_DIM // 2, 1) * sn
                o_idx, col = place[head]
                out_refs[o_idx][0, rows, col:col + HEAD_DIM] = t.astype(out_refs[o_idx].dtype)


def _projection(x, mod3, mod_row, g, w, *, head_ops, out_widths, out_dtypes,
                rope=None, norms=None, name):
    B, L, _ = x.shape
    n_total = w.shape[1]
    assert n_total == sum(out_widths) and n_total % COL_CHUNK == 0
    tm = min(ROW_TILE, L)
    sub = min(SUB_ROWS, tm)
    assert L % tm == 0 and tm % sub == 0
    mod_map = (lambda b, i: (b, 0, 0)) if mod_row is None else (lambda b, i: (mod_row, 0, 0))
    in_specs = [
        pl.BlockSpec((1, tm, D_MODEL), lambda b, i: (b, i, 0)),
        pl.BlockSpec((1, 1, 3 * D_MODEL), mod_map),
        _resident((1, D_MODEL), lambda b, i: (0, 0)),
        _resident((D_MODEL, n_total), lambda b, i: (0, 0)),
    ]
    args = [x, mod3, g.reshape(1, D_MODEL), w]
    if rope is not None:
        in_specs += [pl.BlockSpec((tm, HEAD_DIM), lambda b, i: (i, 0))] * 2
        args += list(rope)
    if norms is not None:
        in_specs += [_resident((1, HEAD_DIM), lambda b, i: (0, 0))] * 2
        args += [n.reshape(1, HEAD_DIM) for n in norms]
    kern = functools.partial(
        _proj_kernel, head_ops=head_ops, out_widths=out_widths, n_sub=tm // sub, sub=sub,
        use_rope=rope is not None, use_norm=norms is not None)
    return pl.pallas_call(
        kern,
        grid=(B, L // tm),
        in_specs=in_specs,
        out_specs=[pl.BlockSpec((1, tm, wd), lambda b, i: (b, i, 0)) for wd in out_widths],
        out_shape=[jax.ShapeDtypeStruct((B, L, wd), dt) for wd, dt in zip(out_widths, out_dtypes)],
        compiler_params=_params(2),
        name=name,
    )(*args)


def _qk(q, k):
    return lax.dot_general(q, k, (((1,), (1,)), ((), ())), preferred_element_type=F32)


def _dense_kernel(q_ref, k_ref, v_ref, kc_ref, vc_ref, z_ref, y_ref,
                  vt_sc, vct_sc, k2_sc, m_sc, l_sc, acc_sc, s_sc, *, tq, tk, n_steps):
    nq = GROUP * tq

    @pl.when(pl.program_id(2) == 0)
    def _():
        def key_norm2(k):
            kf = k.astype(F32)
            return jnp.max(jnp.sum(kf * kf, axis=1, keepdims=True))

        def tbody(j, best):
            off = pl.multiple_of(j * tk, tk)
            vt_sc[j] = v_ref[0, pl.ds(off, tk), :].astype(F32).T.astype(BF16)
            return jnp.maximum(best, key_norm2(k_ref[0, pl.ds(off, tk), :]))

        best = lax.fori_loop(0, n_steps, tbody, key_norm2(kc_ref[0]))
        vct_sc[...] = vc_ref[0].astype(F32).T.astype(BF16)
        k2_sc[...] = jnp.full(k2_sc.shape, best, F32)

    l_sc[...] = jnp.zeros(l_sc.shape, F32)
    acc_sc[...] = jnp.zeros(acc_sc.shape, F32)
    q = [q_ref[0, :, g * HEAD_DIM:(g + 1) * HEAD_DIM] for g in range(GROUP)]

    ones = jnp.ones((SUBLANES, HEAD_DIM), F32)
    k2 = k2_sc[0:1, 0:1]
    bound = []
    for g in range(GROUP):
        qf = q[g].astype(F32)
        q2 = _qk(ones, qf * qf)
        bound.append(jnp.sqrt(q2 * k2) * BOUND_SLACK)
    worst = functools.reduce(jnp.maximum, [jnp.max(b) for b in bound])
    safe = worst * (2.0 * ATTN_SCALE) <= SAFE_EXPONENT

    @pl.when(safe)
    def _():
        q_all = jnp.concatenate(q, axis=0)
        shift = jnp.concatenate(bound, axis=1) * EXP2_SCALE

        def scores(k):
            return _qk(k, q_all)

        def consume(s, vt):
            p = jnp.exp2(s.reshape(-1, SUBLANES, nq) * EXP2_SCALE - shift[None])
            l_sc[...] += jnp.sum(p, axis=0)
            acc_sc[...] += jnp.dot(vt, p.reshape(-1, nq).astype(BF16), preferred_element_type=F32)

        def key_tile(j):
            return k_ref[0, pl.ds(pl.multiple_of(j * tk, tk), tk), :]

        def step(j, slot):
            s_sc[1 - slot] = scores(key_tile(j + 1))
            consume(s_sc[slot], vt_sc[j])

        s_sc[0] = scores(key_tile(0))
        n_loop = (n_steps - 2) // DENSE_UNROLL

        def body(i, carry):
            for u in range(DENSE_UNROLL):
                step(i * DENSE_UNROLL + u, u % 2)
            return carry

        lax.fori_loop(0, n_loop, body, 0)
        for j in range(n_loop * DENSE_UNROLL, n_steps - 1):
            step(j, j % 2)
        s_ctx = scores(kc_ref[0])
        consume(s_sc[(n_steps - 1) % 2], vt_sc[n_steps - 1])
        consume(s_ctx, vct_sc[...])

    @pl.when(jnp.logical_not(safe))
    def _():
        m_sc[...] = jnp.full(m_sc.shape, -jnp.inf, F32)

        def attend(k, vt):
            for g in range(GROUP):
                st = _qk(k, q[g])
                m_prev = m_sc[g]
                m_new = jnp.maximum(m_prev, jnp.max(st, axis=0, keepdims=True))
                alpha = jnp.exp2((m_prev - m_new) * EXP2_SCALE)
                p = jnp.exp2((st - m_new) * EXP2_SCALE)
                cols = slice(g * tq, (g + 1) * tq)
                l_sc[0:1, cols] = alpha * l_sc[0:1, cols] + jnp.sum(p, axis=0, keepdims=True)
                acc_sc[:, cols] = alpha * acc_sc[:, cols] + jnp.dot(vt, p.astype(BF16),
                                                                     preferred_element_type=F32)
                m_sc[g] = m_new

        def body(j, carry):
            off = pl.multiple_of(j * tk, tk)
            attend(k_ref[0, pl.ds(off, tk), :], vt_sc[j])
            return carry

        lax.fori_loop(0, n_steps, body, 0)
        attend(kc_ref[0], vct_sc[...])

    o = (acc_sc[...] / jnp.sum(l_sc[...], axis=0, keepdims=True)).T
    o = jnp.concatenate([o[g * tq:(g + 1) * tq] for g in range(GROUP)], axis=1)
    y_ref[0] = (o * _silu(z_ref[0])).astype(y_ref.dtype)


def _dense_attention(q, kv, kvc, z):
    B, L, _ = q.shape
    C = kvc.shape[1]
    tq = min(ATTN_TQ, L)
    tk = min(ATTN_TK, L)
    assert L % tq == 0 and L % tk == 0 and L // tk >= 2 and DENSE_UNROLL % 2 == 0
    gw = GROUP * HEAD_DIM
    k_col, v_col = 0, KV_A_W // HEAD_DIM
    kern = functools.partial(_dense_kernel, tq=tq, tk=tk, n_steps=L // tk)
    return pl.pallas_call(
        kern,
        grid=(B, N_KV_A, L // tq),
        in_specs=[
            pl.BlockSpec((1, tq, gw), lambda b, h, i: (b, i, h)),
            pl.BlockSpec((1, L, HEAD_DIM), lambda b, h, i: (b, 0, k_col + h)),
            pl.BlockSpec((1, L, HEAD_DIM), lambda b, h, i: (b, 0, v_col + h)),
            pl.BlockSpec((1, C, HEAD_DIM), lambda b, h, i: (b, 0, k_col + h)),
            pl.BlockSpec((1, C, HEAD_DIM), lambda b, h, i: (b, 0, v_col + h)),
            pl.BlockSpec((1, tq, gw), lambda b, h, i: (b, i, h)),
        ],
        out_specs=pl.BlockSpec((1, tq, gw), lambda b, h, i: (b, i, h)),
        out_shape=jax.ShapeDtypeStruct((B, L, N_HEADS_A * HEAD_DIM), BF16),
        scratch_shapes=[
            pltpu.VMEM((L // tk, HEAD_DIM, tk), BF16),
            pltpu.VMEM((HEAD_DIM, C), BF16),
            pltpu.VMEM((SUBLANES, LANES), F32),
            pltpu.VMEM((GROUP, 1, tq), F32),
            pltpu.VMEM((SUBLANES, GROUP * tq), F32),
            pltpu.VMEM((HEAD_DIM, GROUP * tq), F32),
            pltpu.VMEM((2, tk, GROUP * tq), F32),
        ],
        compiler_params=_params(3),
        name="dense_attention",
    )(q, kv, kv, kvc, kvc, z)


def _window_kernel(sink_ref, q_ref, k_ref, v_ref, kc_ref, vc_ref, z_ref, y_ref, vt_sc, vct_sc,
                   *, tq, sub, seq_len):
    h = pl.program_id(1)
    i = pl.program_id(2)
    band = sub + 2 * WINDOW
    nq = GROUP * sub
    slabs = band // WINDOW

    @pl.when(i == 0)
    def _():
        def tbody(j, carry):
            off = pl.multiple_of(j * WINDOW, WINDOW)
            vt_sc[j] = v_ref[0, pl.ds(off, WINDOW), :].astype(F32).T.astype(BF16)
            return carry
        lax.fori_loop(0, seq_len // WINDOW, tbody, 0)
        vct_sc[...] = vc_ref[0].astype(F32).T.astype(BF16)

    lane_head = lax.broadcasted_iota(jnp.int32, (1, nq), 1) // sub
    sink = jnp.zeros((1, nq), F32)
    for g in range(GROUP):
        sink = jnp.where(lane_head == g, sink_ref[h * GROUP + g], sink)
    sink = sink * LOG2_E
    rel = (lax.broadcasted_iota(jnp.int32, (band, sub), 0)
           - lax.broadcasted_iota(jnp.int32, (band, sub), 1))
    kc = kc_ref[0]
    vct = vct_sc[...]

    for c in range(tq // sub):
        rows = slice(c * sub, (c + 1) * sub)
        start = i * tq + c * sub
        kstart = pl.multiple_of(jnp.clip(start - WINDOW, 0, seq_len - band), WINDOW)
        q_all = jnp.concatenate(
            [q_ref[0, rows, g * HEAD_DIM:(g + 1) * HEAD_DIM] for g in range(GROUP)], axis=0)
        bias = jnp.where(jnp.abs(rel + (kstart - start)) <= WINDOW, 0.0, NEG_INF)
        s = (_qk(k_ref[0, pl.ds(kstart, band), :], q_all) * EXP2_SCALE
             + jnp.concatenate([bias] * GROUP, axis=1))
        sc = _qk(kc, q_all) * EXP2_SCALE
        m = jnp.maximum(jnp.maximum(jnp.max(s, axis=0, keepdims=True),
                                    jnp.max(sc, axis=0, keepdims=True)), sink)
        p = jnp.exp2(s - m)
        pc = jnp.exp2(sc - m)
        denom = (jnp.sum(p, axis=0, keepdims=True) + jnp.sum(pc, axis=0, keepdims=True)
                 + jnp.exp2(sink - m))
        slab0 = kstart // WINDOW
        vbt = jnp.concatenate([vt_sc[slab0 + t] for t in range(slabs)], axis=1)
        o = (jnp.dot(vbt, p.astype(BF16), preferred_element_type=F32)
             + jnp.dot(vct, pc.astype(BF16), preferred_element_type=F32)) / denom
        o = o.T
        o = jnp.concatenate([o[g * sub:(g + 1) * sub] for g in range(GROUP)], axis=1)
        y_ref[0, rows, :] = (o * _silu(z_ref[0, rows, :])).astype(y_ref.dtype)


def _window_attention(q, kv, kvc, z, sink):
    B, L, _ = q.shape
    C = kvc.shape[1]
    sub = min(WINDOW_SUB, L - 2 * WINDOW)
    tq = min(WINDOW_TQ, L)
    assert L % tq == 0 and tq % sub == 0 and L >= sub + 2 * WINDOW and L % WINDOW == 0
    gw = GROUP * HEAD_DIM
    q_col = N_HEADS_A // GROUP
    k_col = 2 * KV_A_W // HEAD_DIM
    v_col = k_col + KV_B_W // HEAD_DIM
    kern = functools.partial(_window_kernel, tq=tq, sub=sub, seq_len=L)
    return pl.pallas_call(
        kern,
        grid=(B, N_KV_B, L // tq),
        in_specs=[
            pl.BlockSpec(memory_space=pltpu.SMEM),
            pl.BlockSpec((1, tq, gw), lambda b, h, i: (b, i, q_col + h)),
            pl.BlockSpec((1, L, HEAD_DIM), lambda b, h, i: (b, 0, k_col + h)),
            pl.BlockSpec((1, L, HEAD_DIM), lambda b, h, i: (b, 0, v_col + h)),
            pl.BlockSpec((1, C, HEAD_DIM), lambda b, h, i: (b, 0, k_col + h)),
            pl.BlockSpec((1, C, HEAD_DIM), lambda b, h, i: (b, 0, v_col + h)),
            pl.BlockSpec((1, tq, gw), lambda b, h, i: (b, i, q_col + h)),
        ],
        out_specs=pl.BlockSpec((1, tq, gw), lambda b, h, i: (b, i, h)),
        out_shape=jax.ShapeDtypeStruct((B, L, N_HEADS_B * HEAD_DIM), BF16),
        scratch_shapes=[
            pltpu.VMEM((L // WINDOW, HEAD_DIM, WINDOW), BF16),
            pltpu.VMEM((HEAD_DIM, C), BF16),
        ],
        compiler_params=_params(3),
        name="window_attention",
    )(sink, q, kv, kv, kvc, kvc, z)


def _finish_rows(r, x_rows, gate, post_g):
    return x_rows + gate * _rms(r, post_g)


def _out_kernel(ya_ref, yb_ref, w_ref, x_ref, mod_ref, g_ref, o_ref, *, n_sub, sub):
    gate = mod_ref[0][:, 2 * D_MODEL:]
    for s in range(n_sub):
        rows = slice(s * sub, (s + 1) * sub)
        y = jnp.concatenate([ya_ref[0, rows, :], yb_ref[0, rows, :]], axis=1)
        r = jnp.dot(y, w_ref[...], preferred_element_type=F32)
        o_ref[0, rows, :] = _finish_rows(r, x_ref[0, rows, :], gate, g_ref[...])


def _out_projection(ya, yb, w, x, mod3, post_g):
    B, L, _ = x.shape
    tm = min(ROW_TILE, L)
    sub = min(SUB_ROWS, tm)
    wa = ya.shape[2]
    kern = functools.partial(_out_kernel, n_sub=tm // sub, sub=sub)
    return pl.pallas_call(
        kern,
        grid=(B, L // tm),
        in_specs=[
            pl.BlockSpec((1, tm, wa), lambda b, i: (b, i, 0)),
            pl.BlockSpec((1, tm, yb.shape[2]), lambda b, i: (b, i, 0)),
            _resident(w.shape, lambda b, i: (0, 0)),
            pl.BlockSpec((1, tm, D_MODEL), lambda b, i: (b, i, 0)),
            pl.BlockSpec((1, 1, 3 * D_MODEL), lambda b, i: (b, 0, 0)),
            _resident((1, D_MODEL), lambda b, i: (0, 0)),
        ],
        out_specs=pl.BlockSpec((1, tm, D_MODEL), lambda b, i: (b, i, 0)),
        out_shape=jax.ShapeDtypeStruct((B, L, D_MODEL), F32),
        compiler_params=_params(2),
        name="out_projection",
    )(ya, yb, w, x, mod3, post_g.reshape(1, D_MODEL))


def _pool_kernel(u_ref, up_ref, un_ref, gt_ref, pw_ref, ps_ref, w_ref, x_ref, mod_ref, g_ref,
                 o_ref, ubuf, *, tm, n_sub, sub, seq_len):
    i = pl.program_id(1)
    n_i = pl.num_programs(1)
    ubuf[0:POOL_HALO, :] = jnp.where(i > 0, up_ref[0], 0.0)
    ubuf[POOL_HALO:POOL_HALO + tm, :] = u_ref[0]
    ubuf[POOL_HALO + tm:, :] = jnp.where(i < n_i - 1, un_ref[0], 0.0)
    gate = mod_ref[0][:, 2 * D_MODEL:]

    for s in range(n_sub):
        r0 = s * sub
        t = i * tm + r0 + lax.broadcasted_iota(jnp.int32, (sub, 1), 0)
        mixed = []
        for g, w in enumerate(POOL_SIZES):
            cols = slice(g * POOL_GROUP, (g + 1) * POOL_GROUP)
            half = w // 2
            ext = ubuf[r0:r0 + sub + 2 * POOL_HALO, cols]
            tot = ext[1:] + ext[:-1]
            lo = 1
            span = 1
            while 2 * span < w:
                tot = tot[:-2 * span] + tot[2 * span:]
                lo += span
                span *= 2
            tot = tot[POOL_HALO - lo:POOL_HALO - lo + sub]
            cnt = jnp.minimum(t + half, seq_len) - jnp.maximum(t - half, 0)
            pooled = tot / cnt.astype(F32) - ubuf[POOL_HALO + r0:POOL_HALO + r0 + sub, cols]
            mixed.append(jnp.dot(pooled.astype(BF16), pw_ref[g], preferred_element_type=F32))
        y = jnp.concatenate(mixed, axis=1) * ps_ref[...]
        y = (y * _silu(gt_ref[0, r0:r0 + sub, :])).astype(BF16)
        r = jnp.dot(y, w_ref[...], preferred_element_type=F32)
        o_ref[0, r0:r0 + sub, :] = _finish_rows(r, x_ref[0, r0:r0 + sub, :], gate, g_ref[...])


def _pool_tail(u, gt, pool_w, pool_scale, w_out, x, mod3, post_g):
    B, L, _ = x.shape
    tm = min(ROW_TILE, L)
    sub = min(SUB_ROWS, tm)
    hb = tm // POOL_HALO
    n_hb = L // POOL_HALO
    kern = functools.partial(_pool_kernel, tm=tm, n_sub=tm // sub, sub=sub, seq_len=L)
    row_spec = pl.BlockSpec((1, tm, D_MODEL), lambda b, i: (b, i, 0))
    return pl.pallas_call(
        kern,
        grid=(B, L // tm),
        in_specs=[
            row_spec,
            pl.BlockSpec((1, POOL_HALO, D_MODEL), lambda b, i: (b, jnp.maximum(i * hb - 1, 0), 0)),
            pl.BlockSpec((1, POOL_HALO, D_MODEL),
                         lambda b, i: (b, jnp.minimum((i + 1) * hb, n_hb - 1), 0)),
            row_spec,
            _resident(pool_w.shape, lambda b, i: (0, 0, 0)),
            _resident((1, D_MODEL), lambda b, i: (0, 0)),
            _resident(w_out.shape, lambda b, i: (0, 0)),
            row_spec,
            pl.BlockSpec((1, 1, 3 * D_MODEL), lambda b, i: (b, 0, 0)),
            _resident((1, D_MODEL), lambda b, i: (0, 0)),
        ],
        out_specs=row_spec,
        out_shape=jax.ShapeDtypeStruct((B, L, D_MODEL), F32),
        scratch_shapes=[pltpu.VMEM((tm + 2 * POOL_HALO, D_MODEL), F32)],
        compiler_params=_params(2),
        name="pool_tail",
    )(u, u, u, gt, pool_w, pool_scale.reshape(1, D_MODEL), w_out, x, mod3,
      post_g.reshape(1, D_MODEL))


def _rope_tables(seq_len):
    rows = seq_len // GRID_W
    row = jnp.broadcast_to(jnp.arange(rows)[:, None], (rows, GRID_W)).reshape(-1).astype(F32)
    col = jnp.broadcast_to(jnp.arange(GRID_W)[None, :], (rows, GRID_W)).reshape(-1).astype(F32)
    inv = ROPE_THETA ** (-jnp.arange(AXIS_FREQS, dtype=F32) / AXIS_FREQS)
    ang = jnp.concatenate([row[:, None] * inv, col[:, None] * inv], axis=-1)
    cos, sin = jnp.cos(ang), jnp.sin(ang)
    return jnp.concatenate([cos, cos], axis=-1), jnp.concatenate([-sin, sin], axis=-1)


def kernel(x, c, ctx, c_ctx, ev_mod_w, ev_mod_b, ev_pre_g, ev_post_g, ev_w_in, ev_q_norm, ev_k_norm,
           ev_sink, ev_w_out, od_mod_w, od_mod_b, od_pre_g, od_post_g, od_w_in, od_pool_w,
           od_pool_scale, od_w_out):
    B, L, D = x.shape
    assert D == D_MODEL and L % GRID_W == 0
    assert ev_mod_w.shape[0] == 1 and od_mod_w.shape[0] == 1
    assert B + 1 <= MOD_ROWS
    ctx_row = B

    cvec = jnp.zeros((MOD_ROWS, D), F32).at[:B].set(c).at[ctx_row].set(c_ctx)
    mod_e = _modulation(cvec, ev_mod_w[0], ev_mod_b[0]).reshape(MOD_ROWS, 1, 3 * D)
    mod_o = _modulation(cvec, od_mod_w[0], od_mod_b[0]).reshape(MOD_ROWS, 1, 3 * D)

    w_in_e = ev_w_in[0].astype(BF16)
    norms = (ev_q_norm[0], ev_k_norm[0])
    kv_ops = ([OP_KNORM_ROPE] * N_KV_A + [OP_RAW] * N_KV_A + [OP_ROPE] * N_KV_B + [OP_RAW] * N_KV_B)
    head_ops = tuple([OP_QNORM_ROPE] * N_HEADS_A + [OP_ROPE] * N_HEADS_B + kv_ops
                     + [OP_RAW] * (Q_W // HEAD_DIM))
    q, kv, z = _projection(
        x, mod_e, None, ev_pre_g[0], w_in_e, head_ops=head_ops,
        out_widths=(Q_W, KV_W, Q_W), out_dtypes=(BF16, BF16, F32),
        rope=_rope_tables(L), norms=norms, name="in_projection_even")
    ctx_ops = tuple([OP_KNORM] * N_KV_A + [OP_RAW] * (KV_W // HEAD_DIM - N_KV_A))
    (kvc,) = _projection(
        ctx, mod_e, ctx_row, ev_pre_g[0], w_in_e[:, Q_W:Q_W + KV_W], head_ops=ctx_ops,
        out_widths=(KV_W,), out_dtypes=(BF16,), norms=norms, name="in_projection_ctx")

    ya = _dense_attention(q, kv, kvc, z)
    yb = _window_attention(q, kv, kvc, z, ev_sink[0])
    x = _out_projection(ya, yb, ev_w_out[0].astype(BF16), x, mod_e, ev_post_g[0])

    u, gt = _projection(
        x, mod_o, None, od_pre_g[0], od_w_in[0].astype(BF16),
        head_ops=(OP_RAW,) * (2 * D // HEAD_DIM), out_widths=(D, D), out_dtypes=(F32, F32),
        name="in_projection_odd")
    return _pool_tail(u, gt, od_pool_w[0].astype(BF16), od_pool_scale[0], od_w_out[0].astype(BF16),
                      x, mod_o, od_post_g[0])
```

```python
import functools
import math

import jax
import jax.numpy as jnp
from jax import lax
from jax.experimental import pallas as pl
from jax.experimental.pallas import tpu as pltpu

D_MODEL = 2048
GRID_W = 64
HEAD_DIM = 128
AXIS_FREQS = HEAD_DIM // 4
ROPE_THETA = 10000.0
N_HEADS_A = 8
N_KV_A = 2
N_HEADS_B = 8
N_KV_B = 2
GROUP = N_HEADS_A // N_KV_A
WINDOW = 128
ATTN_SCALE = HEAD_DIM ** -0.5
NEG_INF = -1e30
Q_W = (N_HEADS_A + N_HEADS_B) * HEAD_DIM
KV_A_W = N_KV_A * HEAD_DIM
KV_B_W = N_KV_B * HEAD_DIM
KV_W = 2 * KV_A_W + 2 * KV_B_W
POOL_SIZES = (2, 4, 8, 16)
POOL_GROUP = D_MODEL // len(POOL_SIZES)
POOL_HALO = 8
EPS = 1e-6

LANES = 128
SUBLANES = 8
VMEM_LIMIT_BYTES = 56 * 1024 * 1024

MOD_ROWS = SUBLANES
MOD_TN = 512
ROW_TILE = 512
SUB_ROWS = 256
COL_CHUNK = 512
ATTN_TQ = 1024
ATTN_TK = 512
DENSE_UNROLL = 4
WINDOW_TQ = 512
WINDOW_SUB = 128
LOG2_E = math.log2(math.e)
EXP2_SCALE = ATTN_SCALE * LOG2_E
BOUND_SLACK = 1.0 + 2.0 ** -7
SAFE_LOGIT = 50.0

OP_RAW, OP_ROPE, OP_QNORM_ROPE, OP_KNORM_ROPE, OP_KNORM = range(5)

F32 = jnp.float32
BF16 = jnp.bfloat16


def _params(n_axes):
    return pltpu.CompilerParams(
        dimension_semantics=("arbitrary",) * n_axes,
        vmem_limit_bytes=VMEM_LIMIT_BYTES,
    )


def _resident(block_shape, index_map):
    return pl.BlockSpec(block_shape, index_map, pipeline_mode=pl.Buffered(1))


def _silu(v):
    return v * jax.nn.sigmoid(v)


def _rms(v, gain):
    ms = jnp.mean(v * v, axis=1, keepdims=True)
    return (v * lax.rsqrt(ms + EPS)) * gain


def _mod_kernel(c_ref, w_ref, b_ref, o_ref):
    a = _silu(c_ref[...]).astype(BF16)
    o_ref[...] = jnp.dot(a, w_ref[...].astype(BF16), preferred_element_type=F32) + b_ref[...]


def _modulation(cvec, w, b):
    n = w.shape[1]
    return pl.pallas_call(
        _mod_kernel,
        grid=(n // MOD_TN,),
        in_specs=[
            pl.BlockSpec((MOD_ROWS, D_MODEL), lambda j: (0, 0)),
            pl.BlockSpec((D_MODEL, MOD_TN), lambda j: (0, j)),
            pl.BlockSpec((1, MOD_TN), lambda j: (0, j)),
        ],
        out_specs=pl.BlockSpec((MOD_ROWS, MOD_TN), lambda j: (0, j)),
        out_shape=jax.ShapeDtypeStruct((MOD_ROWS, n), F32),
        compiler_params=_params(1),
        name="modulation",
    )(cvec, w, b.reshape(1, n))


def _proj_kernel(*refs, head_ops, out_widths, n_sub, sub, use_rope, use_norm):
    x_ref, mod_ref, g_ref, w_ref = refs[:4]
    pos = 4
    if use_rope:
        cos_ref, sin_ref = refs[pos:pos + 2]
        pos += 2
    if use_norm:
        qn_ref, kn_ref = refs[pos:pos + 2]
        pos += 2
    out_refs = refs[pos:]

    mrow = mod_ref[0]
    shift = mrow[:, :D_MODEL]
    gain = g_ref[...] * (1.0 + mrow[:, D_MODEL:2 * D_MODEL])
    n_total = sum(out_widths)
    heads_per_chunk = COL_CHUNK // HEAD_DIM

    place = []
    for o_idx, width in enumerate(out_widths):
        for col in range(0, width, HEAD_DIM):
            place.append((o_idx, col))

    for s in range(n_sub):
        rows = slice(s * sub, (s + 1) * sub)
        xs = x_ref[0, rows, :]
        ms = jnp.mean(xs * xs, axis=1, keepdims=True)
        h = ((xs * lax.rsqrt(ms + EPS)) * gain + shift).astype(BF16)
        if use_rope:
            cs = cos_ref[rows, :]
            sn = sin_ref[rows, :]
        for c in range(n_total // COL_CHUNK):
            acc = jnp.dot(h, w_ref[:, c * COL_CHUNK:(c + 1) * COL_CHUNK],
                          preferred_element_type=F32)
            for hh in range(heads_per_chunk):
                head = c * heads_per_chunk + hh
                t = acc[:, hh * HEAD_DIM:(hh + 1) * HEAD_DIM]
                op = head_ops[head]
                if op == OP_QNORM_ROPE:
                    t = _rms(t, qn_ref[...])
                elif op in (OP_KNORM_ROPE, OP_KNORM):
                    t = _rms(t, kn_ref[...])
                if op in (OP_ROPE, OP_QNORM_ROPE, OP_KNORM_ROPE):
                    t = t * cs + pltpu.roll(t, HEAD_DIM // 2, 1) * sn
                o_idx, col = place[head]
                out_refs[o_idx][0, rows, col:col + HEAD_DIM] = t.astype(out_refs[o_idx].dtype)


def _projection(x, mod3, mod_row, g, w, *, head_ops, out_widths, out_dtypes,
                rope=None, norms=None, name):
    B, L, _ = x.shape
    n_total = w.shape[1]
    assert n_total == sum(out_widths) and n_total % COL_CHUNK == 0
    tm = min(ROW_TILE, L)
    sub = min(SUB_ROWS, tm)
    assert L % tm == 0 and tm % sub == 0
    mod_map = (lambda b, i: (b, 0, 0)) if mod_row is None else (lambda b, i: (mod_row, 0, 0))
    in_specs = [
        pl.BlockSpec((1, tm, D_MODEL), lambda b, i: (b, i, 0)),
        pl.BlockSpec((1, 1, 3 * D_MODEL), mod_map),
        _resident((1, D_MODEL), lambda b, i: (0, 0)),
        _resident((D_MODEL, n_total), lambda b, i: (0, 0)),
    ]
    args = [x, mod3, g.reshape(1, D_MODEL), w]
    if rope is not None:
        in_specs += [pl.BlockSpec((tm, HEAD_DIM), lambda b, i: (i, 0))] * 2
        args += list(rope)
    if norms is not None:
        in_specs += [_resident((1, HEAD_DIM), lambda b, i: (0, 0))] * 2
        args += [n.reshape(1, HEAD_DIM) for n in norms]
    kern = functools.partial(
        _proj_kernel, head_ops=head_ops, out_widths=out_widths, n_sub=tm // sub, sub=sub,
        use_rope=rope is not None, use_norm=norms is not None)
    return pl.pallas_call(
        kern,
        grid=(B, L // tm),
        in_specs=in_specs,
        out_specs=[pl.BlockSpec((1, tm, wd), lambda b, i: (b, i, 0)) for wd in out_widths],
        out_shape=[jax.ShapeDtypeStruct((B, L, wd), dt) for wd, dt in zip(out_widths, out_dtypes)],
        compiler_params=_params(2),
        name=name,
    )(*args)


def _qk(q, k):
    return lax.dot_general(q, k, (((1,), (1,)), ((), ())), preferred_element_type=F32)


def _dense_kernel(q_ref, k_ref, v_ref, kc_ref, vc_ref, z_ref, y_ref,
                  vt_sc, vct_sc, k2_sc, m_sc, l_sc, acc_sc, *, tq, tk, n_steps):
    nq = GROUP * tq

    @pl.when(pl.program_id(2) == 0)
    def _():
        def key_norm2(k):
            kf = k.astype(F32)
            return jnp.max(jnp.sum(kf * kf, axis=1, keepdims=True))

        def tbody(j, best):
            off = pl.multiple_of(j * tk, tk)
            vt_sc[j] = v_ref[0, pl.ds(off, tk), :].astype(F32).T.astype(BF16)
            return jnp.maximum(best, key_norm2(k_ref[0, pl.ds(off, tk), :]))

        k2_sc[0] = lax.fori_loop(0, n_steps, tbody, key_norm2(kc_ref[0]))
        vct_sc[...] = vc_ref[0].astype(F32).T.astype(BF16)

    l_sc[...] = jnp.zeros(l_sc.shape, F32)
    acc_sc[...] = jnp.zeros(acc_sc.shape, F32)
    q = [q_ref[0, :, g * HEAD_DIM:(g + 1) * HEAD_DIM] for g in range(GROUP)]

    q2 = functools.reduce(jnp.maximum, [
        jnp.max(jnp.sum(qg.astype(F32) * qg.astype(F32), axis=1, keepdims=True)) for qg in q])
    safe = q2 * k2_sc[0] * (ATTN_SCALE * ATTN_SCALE * BOUND_SLACK) <= SAFE_LOGIT * SAFE_LOGIT

    @pl.when(safe)
    def _():
        q_all = jnp.concatenate(q, axis=0)

        def attend(k, vt):
            p = jnp.exp2(_qk(k, q_all) * EXP2_SCALE)
            l_sc[...] += jnp.sum(p.reshape(-1, SUBLANES, nq), axis=0)
            acc_sc[...] += jnp.dot(vt, p.astype(BF16), preferred_element_type=F32)

        def key_tile(j):
            return k_ref[0, pl.ds(pl.multiple_of(j * tk, tk), tk), :]

        n_loop = n_steps // DENSE_UNROLL

        def body(i, carry):
            for u in range(DENSE_UNROLL):
                j = i * DENSE_UNROLL + u
                attend(key_tile(j), vt_sc[j])
            return carry

        lax.fori_loop(0, n_loop, body, 0)
        for j in range(n_loop * DENSE_UNROLL, n_steps):
            attend(key_tile(j), vt_sc[j])
        attend(kc_ref[0], vct_sc[...])

    @pl.when(jnp.logical_not(safe))
    def _():
        m_sc[...] = jnp.full(m_sc.shape, -jnp.inf, F32)

        def attend(k, vt):
            for g in range(GROUP):
                st = _qk(k, q[g])
                m_prev = m_sc[g]
                m_new = jnp.maximum(m_prev, jnp.max(st, axis=0, keepdims=True))
                alpha = jnp.exp2((m_prev - m_new) * EXP2_SCALE)
                p = jnp.exp2((st - m_new) * EXP2_SCALE)
                cols = slice(g * tq, (g + 1) * tq)
                l_sc[0:1, cols] = alpha * l_sc[0:1, cols] + jnp.sum(p, axis=0, keepdims=True)
                acc_sc[:, cols] = alpha * acc_sc[:, cols] + jnp.dot(vt, p.astype(BF16),
                                                                     preferred_element_type=F32)
                m_sc[g] = m_new

        def body(j, carry):
            off = pl.multiple_of(j * tk, tk)
            attend(k_ref[0, pl.ds(off, tk), :], vt_sc[j])
            return carry

        lax.fori_loop(0, n_steps, body, 0)
        attend(kc_ref[0], vct_sc[...])

    o = (acc_sc[...] / jnp.sum(l_sc[...], axis=0, keepdims=True)).T
    o = jnp.concatenate([o[g * tq:(g + 1) * tq] for g in range(GROUP)], axis=1)
    y_ref[0] = (o * _silu(z_ref[0])).astype(y_ref.dtype)


def _dense_attention(q, kv, kvc, z):
    B, L, _ = q.shape
    C = kvc.shape[1]
    tq = min(ATTN_TQ, L)
    tk = min(ATTN_TK, L)
    assert L % tq == 0 and L % tk == 0
    gw = GROUP * HEAD_DIM
    k_col, v_col = 0, KV_A_W // HEAD_DIM
    kern = functools.partial(_dense_kernel, tq=tq, tk=tk, n_steps=L // tk)
    return pl.pallas_call(
        kern,
        grid=(B, N_KV_A, L // tq),
        in_specs=[
            pl.BlockSpec((1, tq, gw), lambda b, h, i: (b, i, h)),
            pl.BlockSpec((1, L, HEAD_DIM), lambda b, h, i: (b, 0, k_col + h)),
            pl.BlockSpec((1, L, HEAD_DIM), lambda b, h, i: (b, 0, v_col + h)),
            pl.BlockSpec((1, C, HEAD_DIM), lambda b, h, i: (b, 0, k_col + h)),
            pl.BlockSpec((1, C, HEAD_DIM), lambda b, h, i: (b, 0, v_col + h)),
            pl.BlockSpec((1, tq, gw), lambda b, h, i: (b, i, h)),
        ],
        out_specs=pl.BlockSpec((1, tq, gw), lambda b, h, i: (b, i, h)),
        out_shape=jax.ShapeDtypeStruct((B, L, N_HEADS_A * HEAD_DIM), BF16),
        scratch_shapes=[
            pltpu.VMEM((L // tk, HEAD_DIM, tk), BF16),
            pltpu.VMEM((HEAD_DIM, C), BF16),
            pltpu.SMEM((1,), F32),
            pltpu.VMEM((GROUP, 1, tq), F32),
            pltpu.VMEM((SUBLANES, GROUP * tq), F32),
            pltpu.VMEM((HEAD_DIM, GROUP * tq), F32),
        ],
        compiler_params=_params(3),
        name="dense_attention",
    )(q, kv, kv, kvc, kvc, z)


def _window_kernel(sink_ref, q_ref, k_ref, v_ref, kc_ref, vc_ref, z_ref, y_ref, vt_sc, vct_sc, k2_sc,
                   *, tq, sub, seq_len):
    h = pl.program_id(1)
    i = pl.program_id(2)
    band = sub + 2 * WINDOW
    nq = GROUP * sub
    slabs = band // WINDOW
    n_sub = tq // sub

    @pl.when(i == 0)
    def _():
        def key_norm2(k):
            kf = k.astype(F32)
            return jnp.max(jnp.sum(kf * kf, axis=1, keepdims=True))

        def tbody(j, best):
            off = pl.multiple_of(j * WINDOW, WINDOW)
            vt_sc[j] = v_ref[0, pl.ds(off, WINDOW), :].astype(F32).T.astype(BF16)
            return jnp.maximum(best, key_norm2(k_ref[0, pl.ds(off, WINDOW), :]))

        k2_sc[0] = lax.fori_loop(0, seq_len // WINDOW, tbody, key_norm2(kc_ref[0]))
        vct_sc[...] = vc_ref[0].astype(F32).T.astype(BF16)

    lane_head = lax.broadcasted_iota(jnp.int32, (1, nq), 1) // sub
    sink = jnp.zeros((1, nq), F32)
    sink_max = sink_ref[h * GROUP]
    for g in range(GROUP):
        sink = jnp.where(lane_head == g, sink_ref[h * GROUP + g], sink)
        sink_max = jnp.maximum(sink_max, sink_ref[h * GROUP + g])
    sink = sink * LOG2_E
    rel = (lax.broadcasted_iota(jnp.int32, (band, sub), 0)
           - lax.broadcasted_iota(jnp.int32, (band, sub), 1))
    kc = kc_ref[0]
    vct = vct_sc[...]

    q_all = [jnp.concatenate([q_ref[0, c * sub:(c + 1) * sub, g * HEAD_DIM:(g + 1) * HEAD_DIM]
                              for g in range(GROUP)], axis=0) for c in range(n_sub)]
    q2 = functools.reduce(jnp.maximum, [
        jnp.max(jnp.sum(qc.astype(F32) * qc.astype(F32), axis=1, keepdims=True)) for qc in q_all])
    safe = jnp.logical_and(
        q2 * k2_sc[0] * (ATTN_SCALE * ATTN_SCALE * BOUND_SLACK) <= SAFE_LOGIT * SAFE_LOGIT,
        sink_max <= SAFE_LOGIT)

    def sub_tile(c, exact_max):
        rows = slice(c * sub, (c + 1) * sub)
        start = i * tq + c * sub
        kstart = pl.multiple_of(jnp.clip(start - WINDOW, 0, seq_len - band), WINDOW)
        bias = jnp.where(jnp.abs(rel + (kstart - start)) <= WINDOW, 0.0, NEG_INF)
        s = (_qk(k_ref[0, pl.ds(kstart, band), :], q_all[c]) * EXP2_SCALE
             + jnp.concatenate([bias] * GROUP, axis=1))
        sc = _qk(kc, q_all[c]) * EXP2_SCALE
        if exact_max:
            m = jnp.maximum(jnp.maximum(jnp.max(s, axis=0, keepdims=True),
                                        jnp.max(sc, axis=0, keepdims=True)), sink)
            s, sc, sink_c = s - m, sc - m, sink - m
        else:
            sink_c = sink
        p = jnp.exp2(s)
        pc = jnp.exp2(sc)
        denom = (jnp.sum(p, axis=0, keepdims=True) + jnp.sum(pc, axis=0, keepdims=True)
                 + jnp.exp2(sink_c))
        slab0 = kstart // WINDOW
        vbt = jnp.concatenate([vt_sc[slab0 + t] for t in range(slabs)], axis=1)
        o = (jnp.dot(vbt, p.astype(BF16), preferred_element_type=F32)
             + jnp.dot(vct, pc.astype(BF16), preferred_element_type=F32)) / denom
        o = o.T
        o = jnp.concatenate([o[g * sub:(g + 1) * sub] for g in range(GROUP)], axis=1)
        y_ref[0, rows, :] = (o * _silu(z_ref[0, rows, :])).astype(y_ref.dtype)

    @pl.when(safe)
    def _():
        for c in range(n_sub):
            sub_tile(c, exact_max=False)

    @pl.when(jnp.logical_not(safe))
    def _():
        for c in range(n_sub):
            sub_tile(c, exact_max=True)


def _window_attention(q, kv, kvc, z, sink):
    B, L, _ = q.shape
    C = kvc.shape[1]
    sub = min(WINDOW_SUB, L - 2 * WINDOW)
    tq = min(WINDOW_TQ, L)
    assert L % tq == 0 and tq % sub == 0 and L >= sub + 2 * WINDOW and L % WINDOW == 0
    gw = GROUP * HEAD_DIM
    q_col = N_HEADS_A // GROUP
    k_col = 2 * KV_A_W // HEAD_DIM
    v_col = k_col + KV_B_W // HEAD_DIM
    kern = functools.partial(_window_kernel, tq=tq, sub=sub, seq_len=L)
    return pl.pallas_call(
        kern,
        grid=(B, N_KV_B, L // tq),
        in_specs=[
            pl.BlockSpec(memory_space=pltpu.SMEM),
            pl.BlockSpec((1, tq, gw), lambda b, h, i: (b, i, q_col + h)),
            pl.BlockSpec((1, L, HEAD_DIM), lambda b, h, i: (b, 0, k_col + h)),
            pl.BlockSpec((1, L, HEAD_DIM), lambda b, h, i: (b, 0, v_col + h)),
            pl.BlockSpec((1, C, HEAD_DIM), lambda b, h, i: (b, 0, k_col + h)),
            pl.BlockSpec((1, C, HEAD_DIM), lambda b, h, i: (b, 0, v_col + h)),
            pl.BlockSpec((1, tq, gw), lambda b, h, i: (b, i, q_col + h)),
        ],
        out_specs=pl.BlockSpec((1, tq, gw), lambda b, h, i: (b, i, h)),
        out_shape=jax.ShapeDtypeStruct((B, L, N_HEADS_B * HEAD_DIM), BF16),
        scratch_shapes=[
            pltpu.VMEM((L // WINDOW, HEAD_DIM, WINDOW), BF16),
            pltpu.VMEM((HEAD_DIM, C), BF16),
            pltpu.SMEM((1,), F32),
        ],
        compiler_params=_params(3),
        name="window_attention",
    )(sink, q, kv, kv, kvc, kvc, z)


def _finish_rows(r, x_rows, gate, post_g):
    return x_rows + gate * _rms(r, post_g)


def _out_kernel(ya_ref, yb_ref, w_ref, x_ref, mod_ref, g_ref, o_ref, *, n_sub, sub):
    gate = mod_ref[0][:, 2 * D_MODEL:]
    for s in range(n_sub):
        rows = slice(s * sub, (s + 1) * sub)
        y = jnp.concatenate([ya_ref[0, rows, :], yb_ref[0, rows, :]], axis=1)
        r = jnp.dot(y, w_ref[...], preferred_element_type=F32)
        o_ref[0, rows, :] = _finish_rows(r, x_ref[0, rows, :], gate, g_ref[...])


def _out_projection(ya, yb, w, x, mod3, post_g):
    B, L, _ = x.shape
    tm = min(ROW_TILE, L)
    sub = min(SUB_ROWS, tm)
    wa = ya.shape[2]
    kern = functools.partial(_out_kernel, n_sub=tm // sub, sub=sub)
    return pl.pallas_call(
        kern,
        grid=(B, L // tm),
        in_specs=[
            pl.BlockSpec((1, tm, wa), lambda b, i: (b, i, 0)),
            pl.BlockSpec((1, tm, yb.shape[2]), lambda b, i: (b, i, 0)),
            _resident(w.shape, lambda b, i: (0, 0)),
            pl.BlockSpec((1, tm, D_MODEL), lambda b, i: (b, i, 0)),
            pl.BlockSpec((1, 1, 3 * D_MODEL), lambda b, i: (b, 0, 0)),
            _resident((1, D_MODEL), lambda b, i: (0, 0)),
        ],
        out_specs=pl.BlockSpec((1, tm, D_MODEL), lambda b, i: (b, i, 0)),
        out_shape=jax.ShapeDtypeStruct((B, L, D_MODEL), F32),
        compiler_params=_params(2),
        name="out_projection",
    )(ya, yb, w, x, mod3, post_g.reshape(1, D_MODEL))


def _pool_kernel(u_ref, up_ref, un_ref, gt_ref, pw_ref, ps_ref, w_ref, x_ref, mod_ref, g_ref,
                 o_ref, ubuf, *, tm, n_sub, sub, seq_len):
    i = pl.program_id(1)
    n_i = pl.num_programs(1)
    ubuf[0:POOL_HALO, :] = jnp.where(i > 0, up_ref[0], 0.0)
    ubuf[POOL_HALO:POOL_HALO + tm, :] = u_ref[0]
    ubuf[POOL_HALO + tm:, :] = jnp.where(i < n_i - 1, un_ref[0], 0.0)
    gate = mod_ref[0][:, 2 * D_MODEL:]

    for s in range(n_sub):
        r0 = s * sub
        t = i * tm + r0 + lax.broadcasted_iota(jnp.int32, (sub, 1), 0)
        mixed = []
        for g, w in enumerate(POOL_SIZES):
            cols = slice(g * POOL_GROUP, (g + 1) * POOL_GROUP)
            half = w // 2
            ext = ubuf[r0:r0 + sub + 2 * POOL_HALO, cols]
            tot = ext[1:] + ext[:-1]
            lo = 1
            span = 1
            while 2 * span < w:
                tot = tot[:-2 * span] + tot[2 * span:]
                lo += span
                span *= 2
            tot = tot[POOL_HALO - lo:POOL_HALO - lo + sub]
            cnt = jnp.minimum(t + half, seq_len) - jnp.maximum(t - half, 0)
            pooled = tot / cnt.astype(F32) - ubuf[POOL_HALO + r0:POOL_HALO + r0 + sub, cols]
            mixed.append(jnp.dot(pooled.astype(BF16), pw_ref[g], preferred_element_type=F32))
        y = jnp.concatenate(mixed, axis=1) * ps_ref[...]
        y = (y * _silu(gt_ref[0, r0:r0 + sub, :])).astype(BF16)
        r = jnp.dot(y, w_ref[...], preferred_element_type=F32)
        o_ref[0, r0:r0 + sub, :] = _finish_rows(r, x_ref[0, r0:r0 + sub, :], gate, g_ref[...])


def _pool_tail(u, gt, pool_w, pool_scale, w_out, x, mod3, post_g):
    B, L, _ = x.shape
    tm = min(ROW_TILE, L)
    sub = min(SUB_ROWS, tm)
    hb = tm // POOL_HALO
    n_hb = L // POOL_HALO
    kern = functools.partial(_pool_kernel, tm=tm, n_sub=tm // sub, sub=sub, seq_len=L)
    row_spec = pl.BlockSpec((1, tm, D_MODEL), lambda b, i: (b, i, 0))
    return pl.pallas_call(
        kern,
        grid=(B, L // tm),
        in_specs=[
            row_spec,
            pl.BlockSpec((1, POOL_HALO, D_MODEL), lambda b, i: (b, jnp.maximum(i * hb - 1, 0), 0)),
            pl.BlockSpec((1, POOL_HALO, D_MODEL),
                         lambda b, i: (b, jnp.minimum((i + 1) * hb, n_hb - 1), 0)),
            row_spec,
            _resident(pool_w.shape, lambda b, i: (0, 0, 0)),
            _resident((1, D_MODEL), lambda b, i: (0, 0)),
            _resident(w_out.shape, lambda b, i: (0, 0)),
            row_spec,
            pl.BlockSpec((1, 1, 3 * D_MODEL), lambda b, i: (b, 0, 0)),
            _resident((1, D_MODEL), lambda b, i: (0, 0)),
        ],
        out_specs=row_spec,
        out_shape=jax.ShapeDtypeStruct((B, L, D_MODEL), F32),
        scratch_shapes=[pltpu.VMEM((tm + 2 * POOL_HALO, D_MODEL), F32)],
        compiler_params=_params(2),
        name="pool_tail",
    )(u, u, u, gt, pool_w, pool_scale.reshape(1, D_MODEL), w_out, x, mod3,
      post_g.reshape(1, D_MODEL))


def _rope_tables(seq_len):
    rows = seq_len // GRID_W
    row = jnp.broadcast_to(jnp.arange(rows)[:, None], (rows, GRID_W)).reshape(-1).astype(F32)
    col = jnp.broadcast_to(jnp.arange(GRID_W)[None, :], (rows, GRID_W)).reshape(-1).astype(F32)
    inv = ROPE_THETA ** (-jnp.arange(AXIS_FREQS, dtype=F32) / AXIS_FREQS)
    ang = jnp.concatenate([row[:, None] * inv, col[:, None] * inv], axis=-1)
    cos, sin = jnp.cos(ang), jnp.sin(ang)
    return jnp.concatenate([cos, cos], axis=-1), jnp.concatenate([-sin, sin], axis=-1)


def kernel(x, c, ctx, c_ctx, ev_mod_w, ev_mod_b, ev_pre_g, ev_post_g, ev_w_in, ev_q_norm, ev_k_norm,
           ev_sink, ev_w_out, od_mod_w, od_mod_b, od_pre_g, od_post_g, od_w_in, od_pool_w,
           od_pool_scale, od_w_out):
    B, L, D = x.shape
    assert D == D_MODEL and L % GRID_W == 0
    assert ev_mod_w.shape[0] == 1 and od_mod_w.shape[0] == 1
    assert B + 1 <= MOD_ROWS
    ctx_row = B

    cvec = jnp.zeros((MOD_ROWS, D), F32).at[:B].set(c).at[ctx_row].set(c_ctx)
    mod_e = _modulation(cvec, ev_mod_w[0], ev_mod_b[0]).reshape(MOD_ROWS, 1, 3 * D)
    mod_o = _modulation(cvec, od_mod_w[0], od_mod_b[0]).reshape(MOD_ROWS, 1, 3 * D)

    w_in_e = ev_w_in[0].astype(BF16)
    norms = (ev_q_norm[0], ev_k_norm[0])
    kv_ops = ([OP_KNORM_ROPE] * N_KV_A + [OP_RAW] * N_KV_A + [OP_ROPE] * N_KV_B + [OP_RAW] * N_KV_B)
    head_ops = tuple([OP_QNORM_ROPE] * N_HEADS_A + [OP_ROPE] * N_HEADS_B + kv_ops
                     + [OP_RAW] * (Q_W // HEAD_DIM))
    q, kv, z = _projection(
        x, mod_e, None, ev_pre_g[0], w_in_e, head_ops=head_ops,
        out_widths=(Q_W, KV_W, Q_W), out_dtypes=(BF16, BF16, F32),
        rope=_rope_tables(L), norms=norms, name="in_projection_even")
    ctx_ops = tuple([OP_KNORM] * N_KV_A + [OP_RAW] * (KV_W // HEAD_DIM - N_KV_A))
    (kvc,) = _projection(
        ctx, mod_e, ctx_row, ev_pre_g[0], w_in_e[:, Q_W:Q_W + KV_W], head_ops=ctx_ops,
        out_widths=(KV_W,), out_dtypes=(BF16,), norms=norms, name="in_projection_ctx")

    ya = _dense_attention(q, kv, kvc, z)
    yb = _window_attention(q, kv, kvc, z, ev_sink[0])
    x = _out_projection(ya, yb, ev_w_out[0].astype(BF16), x, mod_e, ev_post_g[0])

    u, gt = _projection(
        x, mod_o, None, od_pre_g[0], od_w_in[0].astype(BF16),
        head_ops=(OP_RAW,) * (2 * D // HEAD_DIM), out_widths=(D, D), out_dtypes=(F32, F32),
        name="in_projection_odd")
    return _pool_tail(u, gt, od_pool_w[0].astype(BF16), od_pool_scale[0], od_w_out[0].astype(BF16),
                      x, mod_o, od_post_g[0])
```

```python
import functools
import math

import jax
import jax.numpy as jnp
from jax import lax
from jax.experimental import pallas as pl
from jax.experimental.pallas import tpu as pltpu

D_MODEL = 2048
GRID_W = 64
HEAD_DIM = 128
AXIS_FREQS = HEAD_DIM // 4
ROPE_THETA = 10000.0
N_HEADS_A = 8
N_KV_A = 2
N_HEADS_B = 8
N_KV_B = 2
GROUP = N_HEADS_A // N_KV_A
WINDOW = 128
ATTN_SCALE = HEAD_DIM ** -0.5
NEG_INF = -1e30
Q_W = (N_HEADS_A + N_HEADS_B) * HEAD_DIM
KV_A_W = N_KV_A * HEAD_DIM
KV_B_W = N_KV_B * HEAD_DIM
KV_W = 2 * KV_A_W + 2 * KV_B_W
POOL_SIZES = (2, 4, 8, 16)
POOL_GROUP = D_MODEL // len(POOL_SIZES)
POOL_HALO = 8
EPS = 1e-6

LANES = 128
SUBLANES = 8
VMEM_LIMIT_BYTES = 56 * 1024 * 1024

MOD_ROWS = SUBLANES
MOD_TN = 512
ROW_TILE = 512
SUB_ROWS = 256
COL_CHUNK = 512
ATTN_TQ = 1024
ATTN_TK = 512
DENSE_UNROLL = 4
WINDOW_TQ = 512
WINDOW_SUB = 128
LOG2_E = math.log2(math.e)
EXP2_SCALE = ATTN_SCALE * LOG2_E
BOUND_SLACK = 1.0 + 2.0 ** -7
SAFE_LOGIT = 50.0

OP_RAW, OP_ROPE, OP_QNORM_ROPE, OP_KNORM_ROPE, OP_KNORM = range(5)
TO_COLS, TO_HEAD, TO_SLABS_T = range(3)
SLAB = LANES

F32 = jnp.float32
BF16 = jnp.bfloat16


def _params(n_axes):
    return pltpu.CompilerParams(
        dimension_semantics=("arbitrary",) * n_axes,
        vmem_limit_bytes=VMEM_LIMIT_BYTES,
    )


def _resident(block_shape, index_map):
    return pl.BlockSpec(block_shape, index_map, pipeline_mode=pl.Buffered(1))


def _silu(v):
    return v * jax.nn.sigmoid(v)


def _rms(v, gain):
    ms = jnp.mean(v * v, axis=1, keepdims=True)
    return (v * lax.rsqrt(ms + EPS)) * gain


def _mod_kernel(c_ref, w_ref, b_ref, o_ref):
    a = _silu(c_ref[...]).astype(BF16)
    o_ref[...] = jnp.dot(a, w_ref[...].astype(BF16), preferred_element_type=F32) + b_ref[...]


def _modulation(cvec, w, b):
    n = w.shape[1]
    return pl.pallas_call(
        _mod_kernel,
        grid=(n // MOD_TN,),
        in_specs=[
            pl.BlockSpec((MOD_ROWS, D_MODEL), lambda j: (0, 0)),
            pl.BlockSpec((D_MODEL, MOD_TN), lambda j: (0, j)),
            pl.BlockSpec((1, MOD_TN), lambda j: (0, j)),
        ],
        out_specs=pl.BlockSpec((MOD_ROWS, MOD_TN), lambda j: (0, j)),
        out_shape=jax.ShapeDtypeStruct((MOD_ROWS, n), F32),
        compiler_params=_params(1),
        name="modulation",
    )(cvec, w, b.reshape(1, n))


def _proj_kernel(*refs, plan, n_sub, sub, use_rope, use_norm):
    x_ref, mod_ref, g_ref, w_ref = refs[:4]
    pos = 4
    if use_rope:
        cos_ref, sin_ref = refs[pos:pos + 2]
        pos += 2
    if use_norm:
        qn_ref, kn_ref = refs[pos:pos + 2]
        pos += 2
    out_refs = refs[pos:]

    mrow = mod_ref[0]
    shift = mrow[:, :D_MODEL]
    gain = g_ref[...] * (1.0 + mrow[:, D_MODEL:2 * D_MODEL])
    heads_per_chunk = COL_CHUNK // HEAD_DIM
    slabs_per_sub = sub // SLAB

    for s in range(n_sub):
        rows = slice(s * sub, (s + 1) * sub)
        xs = x_ref[0, rows, :]
        ms = jnp.mean(xs * xs, axis=1, keepdims=True)
        h = ((xs * lax.rsqrt(ms + EPS)) * gain + shift).astype(BF16)
        if use_rope:
            cs = cos_ref[rows, :]
            sn = sin_ref[rows, :]
        for c in range(len(plan) // heads_per_chunk):
            acc = jnp.dot(h, w_ref[:, c * COL_CHUNK:(c + 1) * COL_CHUNK],
                          preferred_element_type=F32)
            for hh in range(heads_per_chunk):
                op, o_idx, kind, slot = plan[c * heads_per_chunk + hh]
                t = acc[:, hh * HEAD_DIM:(hh + 1) * HEAD_DIM]
                if op == OP_QNORM_ROPE:
                    t = _rms(t, qn_ref[...])
                elif op in (OP_KNORM_ROPE, OP_KNORM):
                    t = _rms(t, kn_ref[...])
                if op in (OP_ROPE, OP_QNORM_ROPE, OP_KNORM_ROPE):
                    t = t * cs + pltpu.roll(t, HEAD_DIM // 2, 1) * sn
                out = out_refs[o_idx]
                if kind == TO_COLS:
                    out[0, rows, slot:slot + HEAD_DIM] = t.astype(out.dtype)
                elif kind == TO_HEAD:
                    out[0, slot, rows, :] = t.astype(out.dtype)
                else:
                    for r in range(slabs_per_sub):
                        out[0, slot, s * slabs_per_sub + r] = (
                            t[r * SLAB:(r + 1) * SLAB, :].T.astype(out.dtype))


def _projection(x, mod3, mod_row, g, w, *, plan, outs, rope=None, norms=None, name):
    B, L, _ = x.shape
    n_total = w.shape[1]
    assert n_total == len(plan) * HEAD_DIM and n_total % COL_CHUNK == 0
    tm = min(ROW_TILE, L)
    sub = min(SUB_ROWS, tm)
    assert L % tm == 0 and tm % sub == 0 and sub % SLAB == 0
    mod_map = (lambda b, i: (b, 0, 0)) if mod_row is None else (lambda b, i: (mod_row, 0, 0))
    in_specs = [
        pl.BlockSpec((1, tm, D_MODEL), lambda b, i: (b, i, 0)),
        pl.BlockSpec((1, 1, 3 * D_MODEL), mod_map),
        _resident((1, D_MODEL), lambda b, i: (0, 0)),
        _resident((D_MODEL, n_total), lambda b, i: (0, 0)),
    ]
    args = [x, mod3, g.reshape(1, D_MODEL), w]
    if rope is not None:
        in_specs += [pl.BlockSpec((tm, HEAD_DIM), lambda b, i: (i, 0))] * 2
        args += list(rope)
    if norms is not None:
        in_specs += [_resident((1, HEAD_DIM), lambda b, i: (0, 0))] * 2
        args += [n.reshape(1, HEAD_DIM) for n in norms]
    out_specs, out_shape = [], []
    for kind, n, dt in outs:
        if kind == TO_COLS:
            out_specs.append(pl.BlockSpec((1, tm, n), lambda b, i: (b, i, 0)))
            out_shape.append(jax.ShapeDtypeStruct((B, L, n), dt))
        elif kind == TO_HEAD:
            out_specs.append(pl.BlockSpec((1, n, tm, HEAD_DIM), lambda b, i: (b, 0, i, 0)))
            out_shape.append(jax.ShapeDtypeStruct((B, n, L, HEAD_DIM), dt))
        else:
            out_specs.append(pl.BlockSpec((1, n, tm // SLAB, HEAD_DIM, SLAB),
                                          lambda b, i: (b, 0, i, 0, 0)))
            out_shape.append(jax.ShapeDtypeStruct((B, n, L // SLAB, HEAD_DIM, SLAB), dt))
    kern = functools.partial(
        _proj_kernel, plan=plan, n_sub=tm // sub, sub=sub,
        use_rope=rope is not None, use_norm=norms is not None)
    return pl.pallas_call(
        kern,
        grid=(B, L // tm),
        in_specs=in_specs,
        out_specs=out_specs,
        out_shape=out_shape,
        compiler_params=_params(2),
        name=name,
    )(*args)


def _qk(q, k):
    return lax.dot_general(q, k, (((1,), (1,)), ((), ())), preferred_element_type=F32)


def _key_norm2(k):
    kf = k.astype(F32)
    return jnp.max(jnp.sum(kf * kf, axis=1, keepdims=True))


def _concat_slabs(vt_ref, first, count):
    return jnp.concatenate([vt_ref[0, 0, first + t] for t in range(count)], axis=1)


def _dense_kernel(q_ref, k_ref, vt_ref, kc_ref, vct_ref, z_ref, y_ref,
                  k2_sc, m_sc, l_sc, acc_sc, *, tq, tk, n_steps):
    nq = GROUP * tq
    slabs = tk // SLAB

    def key_tile(j):
        return k_ref[0, 0, pl.ds(pl.multiple_of(j * tk, tk), tk), :]

    def vt_tile(j):
        return _concat_slabs(vt_ref, j * slabs, slabs)

    kc = kc_ref[0, 0]
    vct = _concat_slabs(vct_ref, 0, vct_ref.shape[2])

    @pl.when(pl.program_id(2) == 0)
    def _():
        k2_sc[0] = lax.fori_loop(
            0, n_steps, lambda j, best: jnp.maximum(best, _key_norm2(key_tile(j))), _key_norm2(kc))

    l_sc[...] = jnp.zeros(l_sc.shape, F32)
    acc_sc[...] = jnp.zeros(acc_sc.shape, F32)
    q = [q_ref[0, :, g * HEAD_DIM:(g + 1) * HEAD_DIM] for g in range(GROUP)]

    q2 = functools.reduce(jnp.maximum, [_key_norm2(qg) for qg in q])
    safe = q2 * k2_sc[0] * (ATTN_SCALE * ATTN_SCALE * BOUND_SLACK) <= SAFE_LOGIT * SAFE_LOGIT

    @pl.when(safe)
    def _():
        q_all = jnp.concatenate(q, axis=0)

        def attend(k, vt):
            p = jnp.exp2(_qk(k, q_all) * EXP2_SCALE)
            l_sc[...] += jnp.sum(p.reshape(-1, SUBLANES, nq), axis=0)
            acc_sc[...] += jnp.dot(vt, p.astype(BF16), preferred_element_type=F32)

        n_loop = n_steps // DENSE_UNROLL

        def body(i, carry):
            for u in range(DENSE_UNROLL):
                j = i * DENSE_UNROLL + u
                attend(key_tile(j), vt_tile(j))
            return carry

        lax.fori_loop(0, n_loop, body, 0)
        for j in range(n_loop * DENSE_UNROLL, n_steps):
            attend(key_tile(j), vt_tile(j))
        attend(kc, vct)

    @pl.when(jnp.logical_not(safe))
    def _():
        m_sc[...] = jnp.full(m_sc.shape, -jnp.inf, F32)

        def attend(k, vt):
            for g in range(GROUP):
                st = _qk(k, q[g])
                m_prev = m_sc[g]
                m_new = jnp.maximum(m_prev, jnp.max(st, axis=0, keepdims=True))
                alpha = jnp.exp2((m_prev - m_new) * EXP2_SCALE)
                p = jnp.exp2((st - m_new) * EXP2_SCALE)
                cols = slice(g * tq, (g + 1) * tq)
                l_sc[0:1, cols] = alpha * l_sc[0:1, cols] + jnp.sum(p, axis=0, keepdims=True)
                acc_sc[:, cols] = alpha * acc_sc[:, cols] + jnp.dot(vt, p.astype(BF16),
                                                                     preferred_element_type=F32)
                m_sc[g] = m_new

        def body(j, carry):
            attend(key_tile(j), vt_tile(j))
            return carry

        lax.fori_loop(0, n_steps, body, 0)
        attend(kc, vct)

    o = (acc_sc[...] / jnp.sum(l_sc[...], axis=0, keepdims=True)).T
    o = jnp.concatenate([o[g * tq:(g + 1) * tq] for g in range(GROUP)], axis=1)
    y_ref[0] = (o * _silu(z_ref[0])).astype(y_ref.dtype)


def _kv_specs(L, C, head0):
    return [
        pl.BlockSpec((1, 1, L, HEAD_DIM), lambda b, h, i: (b, head0 + h, 0, 0)),
        pl.BlockSpec((1, 1, L // SLAB, HEAD_DIM, SLAB), lambda b, h, i: (b, head0 + h, 0, 0, 0)),
        pl.BlockSpec((1, 1, C, HEAD_DIM), lambda b, h, i: (b, head0 + h, 0, 0)),
        pl.BlockSpec((1, 1, C // SLAB, HEAD_DIM, SLAB), lambda b, h, i: (b, head0 + h, 0, 0, 0)),
    ]


def _dense_attention(q, k, vt, kc, vct, z):
    B, L, _ = q.shape
    C = kc.shape[2]
    tq = min(ATTN_TQ, L)
    tk = min(ATTN_TK, L)
    assert L % tq == 0 and L % tk == 0 and tk % SLAB == 0 and C % SLAB == 0
    gw = GROUP * HEAD_DIM
    kern = functools.partial(_dense_kernel, tq=tq, tk=tk, n_steps=L // tk)
    qz_spec = pl.BlockSpec((1, tq, gw), lambda b, h, i: (b, i, h))
    return pl.pallas_call(
        kern,
        grid=(B, N_KV_A, L // tq),
        in_specs=[qz_spec] + _kv_specs(L, C, 0) + [qz_spec],
        out_specs=qz_spec,
        out_shape=jax.ShapeDtypeStruct((B, L, N_HEADS_A * HEAD_DIM), BF16),
        scratch_shapes=[
            pltpu.SMEM((1,), F32),
            pltpu.VMEM((GROUP, 1, tq), F32),
            pltpu.VMEM((SUBLANES, GROUP * tq), F32),
            pltpu.VMEM((HEAD_DIM, GROUP * tq), F32),
        ],
        compiler_params=_params(3),
        name="dense_attention",
    )(q, k, vt, kc, vct, z)


def _window_kernel(sink_ref, q_ref, k_ref, vt_ref, kc_ref, vct_ref, z_ref, y_ref, k2_sc,
                   *, tq, sub, seq_len):
    h = pl.program_id(1)
    i = pl.program_id(2)
    band = sub + 2 * WINDOW
    nq = GROUP * sub
    n_sub = tq // sub

    kc = kc_ref[0, 0]
    vct = _concat_slabs(vct_ref, 0, vct_ref.shape[2])

    @pl.when(i == 0)
    def _():
        def kbody(j, best):
            off = pl.multiple_of(j * ATTN_TK, ATTN_TK)
            return jnp.maximum(best, _key_norm2(k_ref[0, 0, pl.ds(off, ATTN_TK), :]))

        k2_sc[0] = lax.fori_loop(0, seq_len // ATTN_TK, kbody, _key_norm2(kc))

    lane_head = lax.broadcasted_iota(jnp.int32, (1, nq), 1) // sub
    sink = jnp.zeros((1, nq), F32)
    sink_max = sink_ref[h * GROUP]
    for g in range(GROUP):
        sink = jnp.where(lane_head == g, sink_ref[h * GROUP + g], sink)
        sink_max = jnp.maximum(sink_max, sink_ref[h * GROUP + g])
    sink = sink * LOG2_E
    rel = (lax.broadcasted_iota(jnp.int32, (band, sub), 0)
           - lax.broadcasted_iota(jnp.int32, (band, sub), 1))
    q_all = [jnp.concatenate([q_ref[0, c * sub:(c + 1) * sub, g * HEAD_DIM:(g + 1) * HEAD_DIM]
                              for g in range(GROUP)], axis=0) for c in range(n_sub)]
    q2 = functools.reduce(jnp.maximum, [_key_norm2(qc) for qc in q_all])
    safe = jnp.logical_and(
        q2 * k2_sc[0] * (ATTN_SCALE * ATTN_SCALE * BOUND_SLACK) <= SAFE_LOGIT * SAFE_LOGIT,
        sink_max <= SAFE_LOGIT)

    def sub_tile(c, exact_max):
        rows = slice(c * sub, (c + 1) * sub)
        start = i * tq + c * sub
        kstart = pl.multiple_of(jnp.clip(start - WINDOW, 0, seq_len - band), WINDOW)
        bias = jnp.where(jnp.abs(rel + (kstart - start)) <= WINDOW, 0.0, NEG_INF)
        s = (_qk(k_ref[0, 0, pl.ds(kstart, band), :], q_all[c]) * EXP2_SCALE
             + jnp.concatenate([bias] * GROUP, axis=1))
        sc = _qk(kc, q_all[c]) * EXP2_SCALE
        if exact_max:
            m = jnp.maximum(jnp.maximum(jnp.max(s, axis=0, keepdims=True),
                                        jnp.max(sc, axis=0, keepdims=True)), sink)
            s, sc, sink_c = s - m, sc - m, sink - m
        else:
            sink_c = sink
        p = jnp.exp2(s)
        pc = jnp.exp2(sc)
        denom = (jnp.sum(p, axis=0, keepdims=True) + jnp.sum(pc, axis=0, keepdims=True)
                 + jnp.exp2(sink_c))
        vbt = _concat_slabs(vt_ref, kstart // SLAB, band // SLAB)
        o = (jnp.dot(vbt, p.astype(BF16), preferred_element_type=F32)
             + jnp.dot(vct, pc.astype(BF16), preferred_element_type=F32)) / denom
        o = o.T
        o = jnp.concatenate([o[g * sub:(g + 1) * sub] for g in range(GROUP)], axis=1)
        y_ref[0, rows, :] = (o * _silu(z_ref[0, rows, :])).astype(y_ref.dtype)

    @pl.when(safe)
    def _():
        for c in range(n_sub):
            sub_tile(c, exact_max=False)

    @pl.when(jnp.logical_not(safe))
    def _():
        for c in range(n_sub):
            sub_tile(c, exact_max=True)


def _window_attention(q, k, vt, kc, vct, z, sink):
    B, L, _ = q.shape
    C = kc.shape[2]
    sub = min(WINDOW_SUB, L - 2 * WINDOW)
    tq = min(WINDOW_TQ, L)
    assert L % tq == 0 and tq % sub == 0 and L >= sub + 2 * WINDOW and L % ATTN_TK == 0
    assert WINDOW % SLAB == 0 and sub % SLAB == 0 and C % SLAB == 0
    gw = GROUP * HEAD_DIM
    q_col = N_HEADS_A // GROUP
    kern = functools.partial(_window_kernel, tq=tq, sub=sub, seq_len=L)
    qz_spec = pl.BlockSpec((1, tq, gw), lambda b, h, i: (b, i, q_col + h))
    return pl.pallas_call(
        kern,
        grid=(B, N_KV_B, L // tq),
        in_specs=[pl.BlockSpec(memory_space=pltpu.SMEM), qz_spec] + _kv_specs(L, C, N_KV_A) + [qz_spec],
        out_specs=pl.BlockSpec((1, tq, gw), lambda b, h, i: (b, i, h)),
        out_shape=jax.ShapeDtypeStruct((B, L, N_HEADS_B * HEAD_DIM), BF16),
        scratch_shapes=[pltpu.SMEM((1,), F32)],
        compiler_params=_params(3),
        name="window_attention",
    )(sink, q, k, vt, kc, vct, z)


def _finish_rows(r, x_rows, gate, post_g):
    return x_rows + gate * _rms(r, post_g)


def _out_kernel(ya_ref, yb_ref, w_ref, x_ref, mod_ref, g_ref, o_ref, *, n_sub, sub):
    gate = mod_ref[0][:, 2 * D_MODEL:]
    for s in range(n_sub):
        rows = slice(s * sub, (s + 1) * sub)
        y = jnp.concatenate([ya_ref[0, rows, :], yb_ref[0, rows, :]], axis=1)
        r = jnp.dot(y, w_ref[...], preferred_element_type=F32)
        o_ref[0, rows, :] = _finish_rows(r, x_ref[0, rows, :], gate, g_ref[...])


def _out_projection(ya, yb, w, x, mod3, post_g):
    B, L, _ = x.shape
    tm = min(ROW_TILE, L)
    sub = min(SUB_ROWS, tm)
    wa = ya.shape[2]
    kern = functools.partial(_out_kernel, n_sub=tm // sub, sub=sub)
    return pl.pallas_call(
        kern,
        grid=(B, L // tm),
        in_specs=[
            pl.BlockSpec((1, tm, wa), lambda b, i: (b, i, 0)),
            pl.BlockSpec((1, tm, yb.shape[2]), lambda b, i: (b, i, 0)),
            _resident(w.shape, lambda b, i: (0, 0)),
            pl.BlockSpec((1, tm, D_MODEL), lambda b, i: (b, i, 0)),
            pl.BlockSpec((1, 1, 3 * D_MODEL), lambda b, i: (b, 0, 0)),
            _resident((1, D_MODEL), lambda b, i: (0, 0)),
        ],
        out_specs=pl.BlockSpec((1, tm, D_MODEL), lambda b, i: (b, i, 0)),
        out_shape=jax.ShapeDtypeStruct((B, L, D_MODEL), F32),
        compiler_params=_params(2),
        name="out_projection",
    )(ya, yb, w, x, mod3, post_g.reshape(1, D_MODEL))


def _pool_kernel(u_ref, up_ref, un_ref, gt_ref, pw_ref, ps_ref, w_ref, x_ref, mod_ref, g_ref,
                 o_ref, ubuf, *, tm, n_sub, sub, seq_len):
    i = pl.program_id(1)
    n_i = pl.num_programs(1)
    ubuf[0:POOL_HALO, :] = jnp.where(i > 0, up_ref[0], 0.0)
    ubuf[POOL_HALO:POOL_HALO + tm, :] = u_ref[0]
    ubuf[POOL_HALO + tm:, :] = jnp.where(i < n_i - 1, un_ref[0], 0.0)
    gate = mod_ref[0][:, 2 * D_MODEL:]

    for s in range(n_sub):
        r0 = s * sub
        t = i * tm + r0 + lax.broadcasted_iota(jnp.int32, (sub, 1), 0)
        mixed = []
        for g, w in enumerate(POOL_SIZES):
            cols = slice(g * POOL_GROUP, (g + 1) * POOL_GROUP)
            half = w // 2
            ext = ubuf[r0:r0 + sub + 2 * POOL_HALO, cols]
            tot = ext[1:] + ext[:-1]
            lo = 1
            span = 1
            while 2 * span < w:
                tot = tot[:-2 * span] + tot[2 * span:]
                lo += span
                span *= 2
            tot = tot[POOL_HALO - lo:POOL_HALO - lo + sub]
            cnt = jnp.minimum(t + half, seq_len) - jnp.maximum(t - half, 0)
            pooled = tot / cnt.astype(F32) - ubuf[POOL_HALO + r0:POOL_HALO + r0 + sub, cols]
            mixed.append(jnp.dot(pooled.astype(BF16), pw_ref[g], preferred_element_type=F32))
        y = jnp.concatenate(mixed, axis=1) * ps_ref[...]
        y = (y * _silu(gt_ref[0, r0:r0 + sub, :])).astype(BF16)
        r = jnp.dot(y, w_ref[...], preferred_element_type=F32)
        o_ref[0, r0:r0 + sub, :] = _finish_rows(r, x_ref[0, r0:r0 + sub, :], gate, g_ref[...])


def _pool_tail(u, gt, pool_w, pool_scale, w_out, x, mod3, post_g):
    B, L, _ = x.shape
    tm = min(ROW_TILE, L)
    sub = min(SUB_ROWS, tm)
    hb = tm // POOL_HALO
    n_hb = L // POOL_HALO
    kern = functools.partial(_pool_kernel, tm=tm, n_sub=tm // sub, sub=sub, seq_len=L)
    row_spec = pl.BlockSpec((1, tm, D_MODEL), lambda b, i: (b, i, 0))
    return pl.pallas_call(
        kern,
        grid=(B, L // tm),
        in_specs=[
            row_spec,
            pl.BlockSpec((1, POOL_HALO, D_MODEL), lambda b, i: (b, jnp.maximum(i * hb - 1, 0), 0)),
            pl.BlockSpec((1, POOL_HALO, D_MODEL),
                         lambda b, i: (b, jnp.minimum((i + 1) * hb, n_hb - 1), 0)),
            row_spec,
            _resident(pool_w.shape, lambda b, i: (0, 0, 0)),
            _resident((1, D_MODEL), lambda b, i: (0, 0)),
            _resident(w_out.shape, lambda b, i: (0, 0)),
            row_spec,
            pl.BlockSpec((1, 1, 3 * D_MODEL), lambda b, i: (b, 0, 0)),
            _resident((1, D_MODEL), lambda b, i: (0, 0)),
        ],
        out_specs=row_spec,
        out_shape=jax.ShapeDtypeStruct((B, L, D_MODEL), F32),
        scratch_shapes=[pltpu.VMEM((tm + 2 * POOL_HALO, D_MODEL), F32)],
        compiler_params=_params(2),
        name="pool_tail",
    )(u, u, u, gt, pool_w, pool_scale.reshape(1, D_MODEL), w_out, x, mod3,
      post_g.reshape(1, D_MODEL))


def _rope_tables(seq_len):
    rows = seq_len // GRID_W
    row = jnp.broadcast_to(jnp.arange(rows)[:, None], (rows, GRID_W)).reshape(-1).astype(F32)
    col = jnp.broadcast_to(jnp.arange(GRID_W)[None, :], (rows, GRID_W)).reshape(-1).astype(F32)
    inv = ROPE_THETA ** (-jnp.arange(AXIS_FREQS, dtype=F32) / AXIS_FREQS)
    ang = jnp.concatenate([row[:, None] * inv, col[:, None] * inv], axis=-1)
    cos, sin = jnp.cos(ang), jnp.sin(ang)
    return jnp.concatenate([cos, cos], axis=-1), jnp.concatenate([-sin, sin], axis=-1)


def kernel(x, c, ctx, c_ctx, ev_mod_w, ev_mod_b, ev_pre_g, ev_post_g, ev_w_in, ev_q_norm, ev_k_norm,
           ev_sink, ev_w_out, od_mod_w, od_mod_b, od_pre_g, od_post_g, od_w_in, od_pool_w,
           od_pool_scale, od_w_out):
    B, L, D = x.shape
    assert D == D_MODEL and L % GRID_W == 0
    assert ev_mod_w.shape[0] == 1 and od_mod_w.shape[0] == 1
    assert B + 1 <= MOD_ROWS
    ctx_row = B

    cvec = jnp.zeros((MOD_ROWS, D), F32).at[:B].set(c).at[ctx_row].set(c_ctx)
    mod_e = _modulation(cvec, ev_mod_w[0], ev_mod_b[0]).reshape(MOD_ROWS, 1, 3 * D)
    mod_o = _modulation(cvec, od_mod_w[0], od_mod_b[0]).reshape(MOD_ROWS, 1, 3 * D)

    w_in_e = ev_w_in[0].astype(BF16)
    norms = (ev_q_norm[0], ev_k_norm[0])
    n_kv = N_KV_A + N_KV_B

    def kv_plan(k_out, v_out, ka_op, kb_op):
        return ([(ka_op, k_out, TO_HEAD, h) for h in range(N_KV_A)]
                + [(OP_RAW, v_out, TO_SLABS_T, h) for h in range(N_KV_A)]
                + [(kb_op, k_out, TO_HEAD, N_KV_A + h) for h in range(N_KV_B)]
                + [(OP_RAW, v_out, TO_SLABS_T, N_KV_A + h) for h in range(N_KV_B)])

    plan = tuple([(OP_QNORM_ROPE, 0, TO_COLS, h * HEAD_DIM) for h in range(N_HEADS_A)]
                 + [(OP_ROPE, 0, TO_COLS, (N_HEADS_A + h) * HEAD_DIM) for h in range(N_HEADS_B)]
                 + kv_plan(1, 2, OP_KNORM_ROPE, OP_ROPE)
                 + [(OP_RAW, 3, TO_COLS, h * HEAD_DIM) for h in range(Q_W // HEAD_DIM)])
    q, k, vt, z = _projection(
        x, mod_e, None, ev_pre_g[0], w_in_e, plan=plan,
        outs=((TO_COLS, Q_W, BF16), (TO_HEAD, n_kv, BF16), (TO_SLABS_T, n_kv, BF16), (TO_COLS, Q_W, F32)),
        rope=_rope_tables(L), norms=norms, name="in_projection_even")
    kc, vct = _projection(
        ctx, mod_e, ctx_row, ev_pre_g[0], w_in_e[:, Q_W:Q_W + KV_W],
        plan=tuple(kv_plan(0, 1, OP_KNORM, OP_RAW)),
        outs=((TO_HEAD, n_kv, BF16), (TO_SLABS_T, n_kv, BF16)), norms=norms, name="in_projection_ctx")

    ya = _dense_attention(q, k, vt, kc, vct, z)
    yb = _window_attention(q, k, vt, kc, vct, z, ev_sink[0])
    x = _out_projection(ya, yb, ev_w_out[0].astype(BF16), x, mod_e, ev_post_g[0])

    u, gt = _projection(
        x, mod_o, None, od_pre_g[0], od_w_in[0].astype(BF16),
        plan=tuple((OP_RAW, col // D, TO_COLS, col % D) for col in range(0, 2 * D, HEAD_DIM)),
        outs=((TO_COLS, D, F32), (TO_COLS, D, F32)), name="in_projection_odd")
    return _pool_tail(u, gt, od_pool_w[0].astype(BF16), od_pool_scale[0], od_w_out[0].astype(BF16),
                      x, mod_o, od_post_g[0])
```

```python
import functools
import math

import jax
import jax.numpy as jnp
from jax import lax
from jax.experimental import pallas as pl
from jax.experimental.pallas import tpu as pltpu

D_MODEL = 2048
GRID_W = 64
HEAD_DIM = 128
AXIS_FREQS = HEAD_DIM // 4
ROPE_THETA = 10000.0
N_HEADS_A = 8
N_KV_A = 2
N_HEADS_B = 8
N_KV_B = 2
GROUP = N_HEADS_A // N_KV_A
WINDOW = 128
ATTN_SCALE = HEAD_DIM ** -0.5
NEG_INF = -1e30
Q_W = (N_HEADS_A + N_HEADS_B) * HEAD_DIM
KV_A_W = N_KV_A * HEAD_DIM
KV_B_W = N_KV_B * HEAD_DIM
KV_W = 2 * KV_A_W + 2 * KV_B_W
POOL_SIZES = (2, 4, 8, 16)
POOL_GROUP = D_MODEL // len(POOL_SIZES)
POOL_HALO = 8
EPS = 1e-6

LANES = 128
SUBLANES = 8
VMEM_LIMIT_BYTES = 56 * 1024 * 1024

MOD_ROWS = SUBLANES
MOD_TN = 512
ROW_TILE = 512
SUB_ROWS = 256
COL_CHUNK = 512
ATTN_TQ = 1024
ATTN_TK = 512
DENSE_UNROLL = 4
WINDOW_TQ = 1024
WINDOW_SUB = 128
LOG2_E = math.log2(math.e)
EXP2_SCALE = ATTN_SCALE * LOG2_E
BOUND_SLACK = 1.0 + 2.0 ** -7
SAFE_LOGIT = 50.0

OP_RAW, OP_ROPE, OP_QNORM_ROPE, OP_KNORM_ROPE, OP_KNORM = range(5)
TO_COLS, TO_HEAD, TO_SLABS_T = range(3)
SLAB = LANES

F32 = jnp.float32
BF16 = jnp.bfloat16


def _params(n_axes):
    return pltpu.CompilerParams(
        dimension_semantics=("arbitrary",) * n_axes,
        vmem_limit_bytes=VMEM_LIMIT_BYTES,
    )


def _resident(block_shape, index_map):
    return pl.BlockSpec(block_shape, index_map, pipeline_mode=pl.Buffered(1))


def _silu(v):
    return v * jax.nn.sigmoid(v)


def _rms(v, gain):
    ms = jnp.mean(v * v, axis=1, keepdims=True)
    return (v * lax.rsqrt(ms + EPS)) * gain


def _mod_kernel(c_ref, w_ref, b_ref, o_ref):
    a = _silu(c_ref[...]).astype(BF16)
    o_ref[...] = jnp.dot(a, w_ref[...].astype(BF16), preferred_element_type=F32) + b_ref[...]


def _modulation(cvec, w, b):
    n = w.shape[1]
    return pl.pallas_call(
        _mod_kernel,
        grid=(n // MOD_TN,),
        in_specs=[
            pl.BlockSpec((MOD_ROWS, D_MODEL), lambda j: (0, 0)),
            pl.BlockSpec((D_MODEL, MOD_TN), lambda j: (0, j)),
            pl.BlockSpec((1, MOD_TN), lambda j: (0, j)),
        ],
        out_specs=pl.BlockSpec((MOD_ROWS, MOD_TN), lambda j: (0, j)),
        out_shape=jax.ShapeDtypeStruct((MOD_ROWS, n), F32),
        compiler_params=_params(1),
        name="modulation",
    )(cvec, w, b.reshape(1, n))


def _proj_kernel(*refs, plan, n_sub, sub, use_rope, use_norm):
    x_ref, mod_ref, g_ref, w_ref = refs[:4]
    pos = 4
    if use_rope:
        cos_ref, sin_ref = refs[pos:pos + 2]
        pos += 2
    if use_norm:
        qn_ref, kn_ref = refs[pos:pos + 2]
        pos += 2
    out_refs = refs[pos:]

    mrow = mod_ref[0]
    shift = mrow[:, :D_MODEL]
    gain = g_ref[...] * (1.0 + mrow[:, D_MODEL:2 * D_MODEL])
    heads_per_chunk = COL_CHUNK // HEAD_DIM
    slabs_per_sub = sub // SLAB

    for s in range(n_sub):
        rows = slice(s * sub, (s + 1) * sub)
        xs = x_ref[0, rows, :]
        ms = jnp.mean(xs * xs, axis=1, keepdims=True)
        h = ((xs * lax.rsqrt(ms + EPS)) * gain + shift).astype(BF16)
        if use_rope:
            cs = cos_ref[rows, :]
            sn = sin_ref[rows, :]
        for c in range(len(plan) // heads_per_chunk):
            acc = jnp.dot(h, w_ref[:, c * COL_CHUNK:(c + 1) * COL_CHUNK],
                          preferred_element_type=F32)
            for hh in range(heads_per_chunk):
                op, o_idx, kind, slot = plan[c * heads_per_chunk + hh]
                t = acc[:, hh * HEAD_DIM:(hh + 1) * HEAD_DIM]
                if op == OP_QNORM_ROPE:
                    t = _rms(t, qn_ref[...])
                elif op in (OP_KNORM_ROPE, OP_KNORM):
                    t = _rms(t, kn_ref[...])
                if op in (OP_ROPE, OP_QNORM_ROPE, OP_KNORM_ROPE):
                    t = t * cs + pltpu.roll(t, HEAD_DIM // 2, 1) * sn
                out = out_refs[o_idx]
                if kind == TO_COLS:
                    out[0, rows, slot:slot + HEAD_DIM] = t.astype(out.dtype)
                elif kind == TO_HEAD:
                    out[0, slot, rows, :] = t.astype(out.dtype)
                else:
                    for r in range(slabs_per_sub):
                        out[0, slot, s * slabs_per_sub + r] = (
                            t[r * SLAB:(r + 1) * SLAB, :].T.astype(out.dtype))


def _projection(x, mod3, mod_row, g, w, *, plan, outs, rope=None, norms=None, name):
    B, L, _ = x.shape
    n_total = w.shape[1]
    assert n_total == len(plan) * HEAD_DIM and n_total % COL_CHUNK == 0
    tm = min(ROW_TILE, L)
    sub = min(SUB_ROWS, tm)
    assert L % tm == 0 and tm % sub == 0 and sub % SLAB == 0
    mod_map = (lambda b, i: (b, 0, 0)) if mod_row is None else (lambda b, i: (mod_row, 0, 0))
    in_specs = [
        pl.BlockSpec((1, tm, D_MODEL), lambda b, i: (b, i, 0)),
        pl.BlockSpec((1, 1, 3 * D_MODEL), mod_map),
        _resident((1, D_MODEL), lambda b, i: (0, 0)),
        _resident((D_MODEL, n_total), lambda b, i: (0, 0)),
    ]
    args = [x, mod3, g.reshape(1, D_MODEL), w]
    if rope is not None:
        in_specs += [pl.BlockSpec((tm, HEAD_DIM), lambda b, i: (i, 0))] * 2
        args += list(rope)
    if norms is not None:
        in_specs += [_resident((1, HEAD_DIM), lambda b, i: (0, 0))] * 2
        args += [n.reshape(1, HEAD_DIM) for n in norms]
    out_specs, out_shape = [], []
    for kind, n, dt in outs:
        if kind == TO_COLS:
            out_specs.append(pl.BlockSpec((1, tm, n), lambda b, i: (b, i, 0)))
            out_shape.append(jax.ShapeDtypeStruct((B, L, n), dt))
        elif kind == TO_HEAD:
            out_specs.append(pl.BlockSpec((1, n, tm, HEAD_DIM), lambda b, i: (b, 0, i, 0)))
            out_shape.append(jax.ShapeDtypeStruct((B, n, L, HEAD_DIM), dt))
        else:
            out_specs.append(pl.BlockSpec((1, n, tm // SLAB, HEAD_DIM, SLAB),
                                          lambda b, i: (b, 0, i, 0, 0)))
            out_shape.append(jax.ShapeDtypeStruct((B, n, L // SLAB, HEAD_DIM, SLAB), dt))
    kern = functools.partial(
        _proj_kernel, plan=plan, n_sub=tm // sub, sub=sub,
        use_rope=rope is not None, use_norm=norms is not None)
    return pl.pallas_call(
        kern,
        grid=(B, L // tm),
        in_specs=in_specs,
        out_specs=out_specs,
        out_shape=out_shape,
        compiler_params=_params(2),
        name=name,
    )(*args)


def _qk(q, k):
    return lax.dot_general(q, k, (((1,), (1,)), ((), ())), preferred_element_type=F32)


def _key_norm2(k):
    kf = k.astype(F32)
    return jnp.max(jnp.sum(kf * kf, axis=1, keepdims=True))


def _concat_slabs(vt_ref, first, count):
    return jnp.concatenate([vt_ref[0, 0, first + t] for t in range(count)], axis=1)


def _dense_kernel(q_ref, k_ref, vt_ref, kc_ref, vct_ref, z_ref, y_ref,
                  k2_sc, m_sc, l_sc, acc_sc, *, tq, tk, n_steps):
    nq = GROUP * tq
    slabs = tk // SLAB

    def key_tile(j):
        return k_ref[0, 0, pl.ds(pl.multiple_of(j * tk, tk), tk), :]

    def vt_tile(j):
        return _concat_slabs(vt_ref, j * slabs, slabs)

    kc = kc_ref[0, 0]
    vct = _concat_slabs(vct_ref, 0, vct_ref.shape[2])

    @pl.when(pl.program_id(2) == 0)
    def _():
        k2_sc[0] = lax.fori_loop(
            0, n_steps, lambda j, best: jnp.maximum(best, _key_norm2(key_tile(j))), _key_norm2(kc))

    l_sc[...] = jnp.zeros(l_sc.shape, F32)
    acc_sc[...] = jnp.zeros(acc_sc.shape, F32)
    q = [q_ref[0, :, g * HEAD_DIM:(g + 1) * HEAD_DIM] for g in range(GROUP)]

    q2 = functools.reduce(jnp.maximum, [_key_norm2(qg) for qg in q])
    safe = q2 * k2_sc[0] * (ATTN_SCALE * ATTN_SCALE * BOUND_SLACK) <= SAFE_LOGIT * SAFE_LOGIT

    @pl.when(safe)
    def _():
        q_all = jnp.concatenate(q, axis=0)

        def attend(k, vt):
            p = jnp.exp2(_qk(k, q_all) * EXP2_SCALE)
            l_sc[...] += jnp.sum(p.reshape(-1, SUBLANES, nq), axis=0)
            acc_sc[...] += jnp.dot(vt, p.astype(BF16), preferred_element_type=F32)

        n_loop = n_steps // DENSE_UNROLL

        def body(i, carry):
            for u in range(DENSE_UNROLL):
                j = i * DENSE_UNROLL + u
                attend(key_tile(j), vt_tile(j))
            return carry

        lax.fori_loop(0, n_loop, body, 0)
        for j in range(n_loop * DENSE_UNROLL, n_steps):
            attend(key_tile(j), vt_tile(j))
        attend(kc, vct)

    @pl.when(jnp.logical_not(safe))
    def _():
        m_sc[...] = jnp.full(m_sc.shape, -jnp.inf, F32)

        def attend(k, vt):
            for g in range(GROUP):
                st = _qk(k, q[g])
                m_prev = m_sc[g]
                m_new = jnp.maximum(m_prev, jnp.max(st, axis=0, keepdims=True))
                alpha = jnp.exp2((m_prev - m_new) * EXP2_SCALE)
                p = jnp.exp2((st - m_new) * EXP2_SCALE)
                cols = slice(g * tq, (g + 1) * tq)
                l_sc[0:1, cols] = alpha * l_sc[0:1, cols] + jnp.sum(p, axis=0, keepdims=True)
                acc_sc[:, cols] = alpha * acc_sc[:, cols] + jnp.dot(vt, p.astype(BF16),
                                                                     preferred_element_type=F32)
                m_sc[g] = m_new

        def body(j, carry):
            attend(key_tile(j), vt_tile(j))
            return carry

        lax.fori_loop(0, n_steps, body, 0)
        attend(kc, vct)

    o = (acc_sc[...] / jnp.sum(l_sc[...], axis=0, keepdims=True)).T
    o = jnp.concatenate([o[g * tq:(g + 1) * tq] for g in range(GROUP)], axis=1)
    y_ref[0] = (o * _silu(z_ref[0])).astype(y_ref.dtype)


def _kv_specs(L, C, head0):
    return [
        pl.BlockSpec((1, 1, L, HEAD_DIM), lambda b, h, i: (b, head0 + h, 0, 0)),
        pl.BlockSpec((1, 1, L // SLAB, HEAD_DIM, SLAB), lambda b, h, i: (b, head0 + h, 0, 0, 0)),
        pl.BlockSpec((1, 1, C, HEAD_DIM), lambda b, h, i: (b, head0 + h, 0, 0)),
        pl.BlockSpec((1, 1, C // SLAB, HEAD_DIM, SLAB), lambda b, h, i: (b, head0 + h, 0, 0, 0)),
    ]


def _dense_attention(q, k, vt, kc, vct, z):
    B, L, _ = q.shape
    C = kc.shape[2]
    tq = min(ATTN_TQ, L)
    tk = min(ATTN_TK, L)
    assert L % tq == 0 and L % tk == 0 and tk % SLAB == 0 and C % SLAB == 0
    gw = GROUP * HEAD_DIM
    kern = functools.partial(_dense_kernel, tq=tq, tk=tk, n_steps=L // tk)
    qz_spec = pl.BlockSpec((1, tq, gw), lambda b, h, i: (b, i, h))
    return pl.pallas_call(
        kern,
        grid=(B, N_KV_A, L // tq),
        in_specs=[qz_spec] + _kv_specs(L, C, 0) + [qz_spec],
        out_specs=qz_spec,
        out_shape=jax.ShapeDtypeStruct((B, L, N_HEADS_A * HEAD_DIM), BF16),
        scratch_shapes=[
            pltpu.SMEM((1,), F32),
            pltpu.VMEM((GROUP, 1, tq), F32),
            pltpu.VMEM((SUBLANES, GROUP * tq), F32),
            pltpu.VMEM((HEAD_DIM, GROUP * tq), F32),
        ],
        compiler_params=_params(3),
        name="dense_attention",
    )(q, k, vt, kc, vct, z)


def _window_kernel(sink_ref, q_ref, k_ref, vt_ref, kc_ref, vct_ref, z_ref, y_ref, k2_sc,
                   *, tq, sub, seq_len):
    h = pl.program_id(1)
    i = pl.program_id(2)
    band = sub + 2 * WINDOW
    nq = GROUP * sub
    n_sub = tq // sub

    kc = kc_ref[0, 0]
    vct = _concat_slabs(vct_ref, 0, vct_ref.shape[2])

    @pl.when(i == 0)
    def _():
        def kbody(j, best):
            off = pl.multiple_of(j * ATTN_TK, ATTN_TK)
            return jnp.maximum(best, _key_norm2(k_ref[0, 0, pl.ds(off, ATTN_TK), :]))

        k2_sc[0] = lax.fori_loop(0, seq_len // ATTN_TK, kbody, _key_norm2(kc))

    lane_head = lax.broadcasted_iota(jnp.int32, (1, nq), 1) // sub
    sink = jnp.zeros((1, nq), F32)
    sink_max = sink_ref[h * GROUP]
    for g in range(GROUP):
        sink = jnp.where(lane_head == g, sink_ref[h * GROUP + g], sink)
        sink_max = jnp.maximum(sink_max, sink_ref[h * GROUP + g])
    sink = sink * LOG2_E
    rel = (lax.broadcasted_iota(jnp.int32, (band, sub), 0)
           - lax.broadcasted_iota(jnp.int32, (band, sub), 1))
    q_all = [jnp.concatenate([q_ref[0, c * sub:(c + 1) * sub, g * HEAD_DIM:(g + 1) * HEAD_DIM]
                              for g in range(GROUP)], axis=0) for c in range(n_sub)]
    q2 = functools.reduce(jnp.maximum, [_key_norm2(qc) for qc in q_all])
    safe = jnp.logical_and(
        q2 * k2_sc[0] * (ATTN_SCALE * ATTN_SCALE * BOUND_SLACK) <= SAFE_LOGIT * SAFE_LOGIT,
        sink_max <= SAFE_LOGIT)

    def sub_tile(c, exact_max):
        rows = slice(c * sub, (c + 1) * sub)
        start = i * tq + c * sub
        kstart = pl.multiple_of(jnp.clip(start - WINDOW, 0, seq_len - band), WINDOW)
        bias = jnp.where(jnp.abs(rel + (kstart - start)) <= WINDOW, 0.0, NEG_INF)
        s = (_qk(k_ref[0, 0, pl.ds(kstart, band), :], q_all[c]) * EXP2_SCALE
             + jnp.concatenate([bias] * GROUP, axis=1))
        sc = _qk(kc, q_all[c]) * EXP2_SCALE
        if exact_max:
            m = jnp.maximum(jnp.maximum(jnp.max(s, axis=0, keepdims=True),
                                        jnp.max(sc, axis=0, keepdims=True)), sink)
            s, sc, sink_c = s - m, sc - m, sink - m
        else:
            sink_c = sink
        p = jnp.exp2(s)
        pc = jnp.exp2(sc)
        denom = (jnp.sum(p, axis=0, keepdims=True) + jnp.sum(pc, axis=0, keepdims=True)
                 + jnp.exp2(sink_c))
        vbt = _concat_slabs(vt_ref, kstart // SLAB, band // SLAB)
        o = (jnp.dot(vbt, p.astype(BF16), preferred_element_type=F32)
             + jnp.dot(vct, pc.astype(BF16), preferred_element_type=F32)) / denom
        o = o.T
        o = jnp.concatenate([o[g * sub:(g + 1) * sub] for g in range(GROUP)], axis=1)
        y_ref[0, rows, :] = (o * _silu(z_ref[0, rows, :])).astype(y_ref.dtype)

    @pl.when(safe)
    def _():
        for c in range(n_sub):
            sub_tile(c, exact_max=False)

    @pl.when(jnp.logical_not(safe))
    def _():
        for c in range(n_sub):
            sub_tile(c, exact_max=True)


def _window_attention(q, k, vt, kc, vct, z, sink):
    B, L, _ = q.shape
    C = kc.shape[2]
    sub = min(WINDOW_SUB, L - 2 * WINDOW)
    tq = min(WINDOW_TQ, L)
    assert L % tq == 0 and tq % sub == 0 and L >= sub + 2 * WINDOW and L % ATTN_TK == 0
    assert WINDOW % SLAB == 0 and sub % SLAB == 0 and C % SLAB == 0
    gw = GROUP * HEAD_DIM
    q_col = N_HEADS_A // GROUP
    kern = functools.partial(_window_kernel, tq=tq, sub=sub, seq_len=L)
    qz_spec = pl.BlockSpec((1, tq, gw), lambda b, h, i: (b, i, q_col + h))
    return pl.pallas_call(
        kern,
        grid=(B, N_KV_B, L // tq),
        in_specs=[pl.BlockSpec(memory_space=pltpu.SMEM), qz_spec] + _kv_specs(L, C, N_KV_A) + [qz_spec],
        out_specs=pl.BlockSpec((1, tq, gw), lambda b, h, i: (b, i, h)),
        out_shape=jax.ShapeDtypeStruct((B, L, N_HEADS_B * HEAD_DIM), BF16),
        scratch_shapes=[pltpu.SMEM((1,), F32)],
        compiler_params=_params(3),
        name="window_attention",
    )(sink, q, k, vt, kc, vct, z)


def _finish_rows(r, x_rows, gate, post_g):
    return x_rows + gate * _rms(r, post_g)


def _out_kernel(ya_ref, yb_ref, w_ref, x_ref, mod_ref, g_ref, o_ref, *, n_sub, sub):
    gate = mod_ref[0][:, 2 * D_MODEL:]
    for s in range(n_sub):
        rows = slice(s * sub, (s + 1) * sub)
        y = jnp.concatenate([ya_ref[0, rows, :], yb_ref[0, rows, :]], axis=1)
        r = jnp.dot(y, w_ref[...], preferred_element_type=F32)
        o_ref[0, rows, :] = _finish_rows(r, x_ref[0, rows, :], gate, g_ref[...])


def _out_projection(ya, yb, w, x, mod3, post_g):
    B, L, _ = x.shape
    tm = min(ROW_TILE, L)
    sub = min(SUB_ROWS, tm)
    wa = ya.shape[2]
    kern = functools.partial(_out_kernel, n_sub=tm // sub, sub=sub)
    return pl.pallas_call(
        kern,
        grid=(B, L // tm),
        in_specs=[
            pl.BlockSpec((1, tm, wa), lambda b, i: (b, i, 0)),
            pl.BlockSpec((1, tm, yb.shape[2]), lambda b, i: (b, i, 0)),
            _resident(w.shape, lambda b, i: (0, 0)),
            pl.BlockSpec((1, tm, D_MODEL), lambda b, i: (b, i, 0)),
            pl.BlockSpec((1, 1, 3 * D_MODEL), lambda b, i: (b, 0, 0)),
            _resident((1, D_MODEL), lambda b, i: (0, 0)),
        ],
        out_specs=pl.BlockSpec((1, tm, D_MODEL), lambda b, i: (b, i, 0)),
        out_shape=jax.ShapeDtypeStruct((B, L, D_MODEL), F32),
        compiler_params=_params(2),
        name="out_projection",
    )(ya, yb, w, x, mod3, post_g.reshape(1, D_MODEL))


def _pool_kernel(u_ref, up_ref, un_ref, gt_ref, pw_ref, ps_ref, w_ref, x_ref, mod_ref, g_ref,
                 o_ref, *, tm, n_sub, sub, seq_len):
    i = pl.program_id(1)
    gate = mod_ref[0][:, 2 * D_MODEL:]

    for s in range(n_sub):
        r0 = s * sub
        rows = slice(r0, r0 + sub)
        t = i * tm + r0 + lax.broadcasted_iota(jnp.int32, (sub, 1), 0)
        mixed = []
        for g, w in enumerate(POOL_SIZES):
            cols = slice(g * POOL_GROUP, (g + 1) * POOL_GROUP)
            half = w // 2
            if s == 0:
                before = jnp.where(i > 0, up_ref[0, :, cols], 0.0)
            else:
                before = u_ref[0, r0 - POOL_HALO:r0, cols]
            if s == n_sub - 1:
                after = jnp.where(i < pl.num_programs(1) - 1, un_ref[0, :, cols], 0.0)
            else:
                after = u_ref[0, r0 + sub:r0 + sub + POOL_HALO, cols]
            u = u_ref[0, rows, cols]
            ext = jnp.concatenate([before, u, after], axis=0)
            tot = ext[1:] + ext[:-1]
            lo = 1
            span = 1
            while 2 * span < w:
                tot = tot[:-2 * span] + tot[2 * span:]
                lo += span
                span *= 2
            tot = tot[POOL_HALO - lo:POOL_HALO - lo + sub]
            cnt = jnp.minimum(t + half, seq_len) - jnp.maximum(t - half, 0)
            pooled = tot / cnt.astype(F32) - u
            mixed.append(jnp.dot(pooled.astype(BF16), pw_ref[g], preferred_element_type=F32))
        y = jnp.concatenate(mixed, axis=1) * ps_ref[...]
        y = (y * _silu(gt_ref[0, rows, :])).astype(BF16)
        r = jnp.dot(y, w_ref[...], preferred_element_type=F32)
        o_ref[0, rows, :] = _finish_rows(r, x_ref[0, rows, :], gate, g_ref[...])


def _pool_tail(u, gt, pool_w, pool_scale, w_out, x, mod3, post_g):
    B, L, _ = x.shape
    tm = min(ROW_TILE, L)
    sub = min(SUB_ROWS, tm)
    hb = tm // POOL_HALO
    n_hb = L // POOL_HALO
    kern = functools.partial(_pool_kernel, tm=tm, n_sub=tm // sub, sub=sub, seq_len=L)
    row_spec = pl.BlockSpec((1, tm, D_MODEL), lambda b, i: (b, i, 0))
    return pl.pallas_call(
        kern,
        grid=(B, L // tm),
        in_specs=[
            row_spec,
            pl.BlockSpec((1, POOL_HALO, D_MODEL), lambda b, i: (b, jnp.maximum(i * hb - 1, 0), 0)),
            pl.BlockSpec((1, POOL_HALO, D_MODEL),
                         lambda b, i: (b, jnp.minimum((i + 1) * hb, n_hb - 1), 0)),
            row_spec,
            _resident(pool_w.shape, lambda b, i: (0, 0, 0)),
            _resident((1, D_MODEL), lambda b, i: (0, 0)),
            _resident(w_out.shape, lambda b, i: (0, 0)),
            row_spec,
            pl.BlockSpec((1, 1, 3 * D_MODEL), lambda b, i: (b, 0, 0)),
            _resident((1, D_MODEL), lambda b, i: (0, 0)),
        ],
        out_specs=row_spec,
        out_shape=jax.ShapeDtypeStruct((B, L, D_MODEL), F32),
        compiler_params=_params(2),
        name="pool_tail",
    )(u, u, u, gt, pool_w, pool_scale.reshape(1, D_MODEL), w_out, x, mod3,
      post_g.reshape(1, D_MODEL))


def _rope_tables(seq_len):
    rows = seq_len // GRID_W
    inv = ROPE_THETA ** (-jnp.arange(AXIS_FREQS, dtype=F32) / AXIS_FREQS)
    row_ang = jnp.arange(rows, dtype=F32)[:, None] * inv
    col_ang = jnp.arange(GRID_W, dtype=F32)[:, None] * inv

    def table(fn, sign):
        shape = (rows, GRID_W, AXIS_FREQS)
        half = jnp.concatenate([jnp.broadcast_to(fn(row_ang)[:, None, :], shape),
                                jnp.broadcast_to(fn(col_ang)[None, :, :], shape)], axis=-1)
        return jnp.concatenate([sign * half, half], axis=-1).reshape(seq_len, HEAD_DIM)

    return table(jnp.cos, 1.0), table(jnp.sin, -1.0)


def kernel(x, c, ctx, c_ctx, ev_mod_w, ev_mod_b, ev_pre_g, ev_post_g, ev_w_in, ev_q_norm, ev_k_norm,
           ev_sink, ev_w_out, od_mod_w, od_mod_b, od_pre_g, od_post_g, od_w_in, od_pool_w,
           od_pool_scale, od_w_out):
    B, L, D = x.shape
    assert D == D_MODEL and L % GRID_W == 0
    assert ev_mod_w.shape[0] == 1 and od_mod_w.shape[0] == 1
    assert B + 1 <= MOD_ROWS
    ctx_row = B

    cvec = jnp.zeros((MOD_ROWS, D), F32).at[:B].set(c).at[ctx_row].set(c_ctx)
    mod_e = _modulation(cvec, ev_mod_w[0], ev_mod_b[0]).reshape(MOD_ROWS, 1, 3 * D)
    mod_o = _modulation(cvec, od_mod_w[0], od_mod_b[0]).reshape(MOD_ROWS, 1, 3 * D)

    w_in_e = ev_w_in[0].astype(BF16)
    norms = (ev_q_norm[0], ev_k_norm[0])
    n_kv = N_KV_A + N_KV_B

    def kv_plan(k_out, v_out, ka_op, kb_op):
        return ([(ka_op, k_out, TO_HEAD, h) for h in range(N_KV_A)]
                + [(OP_RAW, v_out, TO_SLABS_T, h) for h in range(N_KV_A)]
                + [(kb_op, k_out, TO_HEAD, N_KV_A + h) for h in range(N_KV_B)]
                + [(OP_RAW, v_out, TO_SLABS_T, N_KV_A + h) for h in range(N_KV_B)])

    plan = tuple([(OP_QNORM_ROPE, 0, TO_COLS, h * HEAD_DIM) for h in range(N_HEADS_A)]
                 + [(OP_ROPE, 0, TO_COLS, (N_HEADS_A + h) * HEAD_DIM) for h in range(N_HEADS_B)]
                 + kv_plan(1, 2, OP_KNORM_ROPE, OP_ROPE)
                 + [(OP_RAW, 3, TO_COLS, h * HEAD_DIM) for h in range(Q_W // HEAD_DIM)])
    q, k, vt, z = _projection(
        x, mod_e, None, ev_pre_g[0], w_in_e, plan=plan,
        outs=((TO_COLS, Q_W, BF16), (TO_HEAD, n_kv, BF16), (TO_SLABS_T, n_kv, BF16), (TO_COLS, Q_W, F32)),
        rope=_rope_tables(L), norms=norms, name="in_projection_even")
    kc, vct = _projection(
        ctx, mod_e, ctx_row, ev_pre_g[0], w_in_e[:, Q_W:Q_W + KV_W],
        plan=tuple(kv_plan(0, 1, OP_KNORM, OP_RAW)),
        outs=((TO_HEAD, n_kv, BF16), (TO_SLABS_T, n_kv, BF16)), norms=norms, name="in_projection_ctx")

    ya = _dense_attention(q, k, vt, kc, vct, z)
    yb = _window_attention(q, k, vt, kc, vct, z, ev_sink[0])
    x = _out_projection(ya, yb, ev_w_out[0].astype(BF16), x, mod_e, ev_post_g[0])

    u, gt = _projection(
        x, mod_o, None, od_pre_g[0], od_w_in[0].astype(BF16),
        plan=tuple((OP_RAW, col // D, TO_COLS, col % D) for col in range(0, 2 * D, HEAD_DIM)),
        outs=((TO_COLS, D, F32), (TO_COLS, D, F32)), name="in_projection_odd")
    return _pool_tail(u, gt, od_pool_w[0].astype(BF16), od_pool_scale[0], od_w_out[0].astype(BF16),
                      x, mod_o, od_post_g[0])
```

```python
import functools
import math

import jax
import jax.numpy as jnp
from jax import lax
from jax.experimental import pallas as pl
from jax.experimental.pallas import tpu as pltpu

D_MODEL = 2048
GRID_W = 64
HEAD_DIM = 128
AXIS_FREQS = HEAD_DIM // 4
ROPE_THETA = 10000.0
N_HEADS_A = 8
N_KV_A = 2
N_HEADS_B = 8
N_KV_B = 2
GROUP = N_HEADS_A // N_KV_A
WINDOW = 128
ATTN_SCALE = HEAD_DIM ** -0.5
NEG_INF = -1e30
Q_W = (N_HEADS_A + N_HEADS_B) * HEAD_DIM
KV_A_W = N_KV_A * HEAD_DIM
KV_B_W = N_KV_B * HEAD_DIM
KV_W = 2 * KV_A_W + 2 * KV_B_W
POOL_SIZES = (2, 4, 8, 16)
POOL_GROUP = D_MODEL // len(POOL_SIZES)
POOL_HALO = 8
EPS = 1e-6

LANES = 128
SUBLANES = 8
VMEM_LIMIT_BYTES = 56 * 1024 * 1024

MOD_ROWS = SUBLANES
MOD_TN = 512
ROW_TILE = 512
SUB_ROWS = 256
COL_CHUNK = 512
ATTN_TQ = 1024
ATTN_TK = 1024
DENSE_UNROLL = 2
WINDOW_TQ = 1024
WINDOW_SUB = 128
LOG2_E = math.log2(math.e)
EXP2_SCALE = ATTN_SCALE * LOG2_E
BOUND_SLACK = 1.0 + 2.0 ** -7
SAFE_LOGIT = 50.0

OP_RAW, OP_ROPE, OP_QNORM_ROPE, OP_KNORM_ROPE, OP_KNORM = range(5)
TO_COLS, TO_HEAD, TO_SLABS_T = range(3)
SLAB = LANES

F32 = jnp.float32
BF16 = jnp.bfloat16


def _params(n_axes):
    return pltpu.CompilerParams(
        dimension_semantics=("arbitrary",) * n_axes,
        vmem_limit_bytes=VMEM_LIMIT_BYTES,
    )


def _resident(block_shape, index_map):
    return pl.BlockSpec(block_shape, index_map, pipeline_mode=pl.Buffered(1))


def _silu(v):
    return v * jax.nn.sigmoid(v)


def _rms(v, gain):
    ms = jnp.mean(v * v, axis=1, keepdims=True)
    return (v * lax.rsqrt(ms + EPS)) * gain


def _mod_kernel(c_ref, w_ref, b_ref, o_ref):
    a = _silu(c_ref[...]).astype(BF16)
    o_ref[...] = jnp.dot(a, w_ref[...].astype(BF16), preferred_element_type=F32) + b_ref[...]


def _modulation(cvec, w, b):
    n = w.shape[1]
    return pl.pallas_call(
        _mod_kernel,
        grid=(n // MOD_TN,),
        in_specs=[
            pl.BlockSpec((MOD_ROWS, D_MODEL), lambda j: (0, 0)),
            pl.BlockSpec((D_MODEL, MOD_TN), lambda j: (0, j)),
            pl.BlockSpec((1, MOD_TN), lambda j: (0, j)),
        ],
        out_specs=pl.BlockSpec((MOD_ROWS, MOD_TN), lambda j: (0, j)),
        out_shape=jax.ShapeDtypeStruct((MOD_ROWS, n), F32),
        compiler_params=_params(1),
        name="modulation",
    )(cvec, w, b.reshape(1, n))


def _proj_kernel(*refs, plan, n_sub, sub, use_rope, use_norm):
    x_ref, mod_ref, g_ref, w_ref = refs[:4]
    pos = 4
    if use_rope:
        cos_ref, sin_ref = refs[pos:pos + 2]
        pos += 2
    if use_norm:
        qn_ref, kn_ref = refs[pos:pos + 2]
        pos += 2
    out_refs = refs[pos:]

    mrow = mod_ref[0]
    shift = mrow[:, :D_MODEL]
    gain = g_ref[...] * (1.0 + mrow[:, D_MODEL:2 * D_MODEL])
    heads_per_chunk = COL_CHUNK // HEAD_DIM
    slabs_per_sub = sub // SLAB

    for s in range(n_sub):
        rows = slice(s * sub, (s + 1) * sub)
        xs = x_ref[0, rows, :]
        ms = jnp.mean(xs * xs, axis=1, keepdims=True)
        h = ((xs * lax.rsqrt(ms + EPS)) * gain + shift).astype(BF16)
        if use_rope:
            cs = cos_ref[rows, :]
            sn = sin_ref[rows, :]
        for c in range(len(plan) // heads_per_chunk):
            acc = jnp.dot(h, w_ref[:, c * COL_CHUNK:(c + 1) * COL_CHUNK],
                          preferred_element_type=F32)
            for hh in range(heads_per_chunk):
                op, o_idx, kind, slot = plan[c * heads_per_chunk + hh]
                t = acc[:, hh * HEAD_DIM:(hh + 1) * HEAD_DIM]
                if op == OP_QNORM_ROPE:
                    t = _rms(t, qn_ref[...])
                elif op in (OP_KNORM_ROPE, OP_KNORM):
                    t = _rms(t, kn_ref[...])
                if op in (OP_ROPE, OP_QNORM_ROPE, OP_KNORM_ROPE):
                    t = t * cs + pltpu.roll(t, HEAD_DIM // 2, 1) * sn
                out = out_refs[o_idx]
                if kind == TO_COLS:
                    out[0, rows, slot:slot + HEAD_DIM] = t.astype(out.dtype)
                elif kind == TO_HEAD:
                    out[0, slot, rows, :] = t.astype(out.dtype)
                else:
                    for r in range(slabs_per_sub):
                        out[0, slot, s * slabs_per_sub + r] = (
                            t[r * SLAB:(r + 1) * SLAB, :].T.astype(out.dtype))


def _projection(x, mod3, mod_row, g, w, *, plan, outs, rope=None, norms=None, name):
    B, L, _ = x.shape
    n_total = w.shape[1]
    assert n_total == len(plan) * HEAD_DIM and n_total % COL_CHUNK == 0
    tm = min(ROW_TILE, L)
    sub = min(SUB_ROWS, tm)
    assert L % tm == 0 and tm % sub == 0 and sub % SLAB == 0
    mod_map = (lambda b, i: (b, 0, 0)) if mod_row is None else (lambda b, i: (mod_row, 0, 0))
    in_specs = [
        pl.BlockSpec((1, tm, D_MODEL), lambda b, i: (b, i, 0)),
        pl.BlockSpec((1, 1, 3 * D_MODEL), mod_map),
        _resident((1, D_MODEL), lambda b, i: (0, 0)),
        _resident((D_MODEL, n_total), lambda b, i: (0, 0)),
    ]
    args = [x, mod3, g.reshape(1, D_MODEL), w]
    if rope is not None:
        in_specs += [pl.BlockSpec((tm, HEAD_DIM), lambda b, i: (i, 0))] * 2
        args += list(rope)
    if norms is not None:
        in_specs += [_resident((1, HEAD_DIM), lambda b, i: (0, 0))] * 2
        args += [n.reshape(1, HEAD_DIM) for n in norms]
    out_specs, out_shape = [], []
    for kind, n, dt in outs:
        if kind == TO_COLS:
            out_specs.append(pl.BlockSpec((1, tm, n), lambda b, i: (b, i, 0)))
            out_shape.append(jax.ShapeDtypeStruct((B, L, n), dt))
        elif kind == TO_HEAD:
            out_specs.append(pl.BlockSpec((1, n, tm, HEAD_DIM), lambda b, i: (b, 0, i, 0)))
            out_shape.append(jax.ShapeDtypeStruct((B, n, L, HEAD_DIM), dt))
        else:
            out_specs.append(pl.BlockSpec((1, n, tm // SLAB, HEAD_DIM, SLAB),
                                          lambda b, i: (b, 0, i, 0, 0)))
            out_shape.append(jax.ShapeDtypeStruct((B, n, L // SLAB, HEAD_DIM, SLAB), dt))
    kern = functools.partial(
        _proj_kernel, plan=plan, n_sub=tm // sub, sub=sub,
        use_rope=rope is not None, use_norm=norms is not None)
    return pl.pallas_call(
        kern,
        grid=(B, L // tm),
        in_specs=in_specs,
        out_specs=out_specs,
        out_shape=out_shape,
        compiler_params=_params(2),
        name=name,
    )(*args)


def _qk(q, k):
    return lax.dot_general(q, k, (((1,), (1,)), ((), ())), preferred_element_type=F32)


def _key_norm2(k):
    kf = k.astype(F32)
    return jnp.max(jnp.sum(kf * kf, axis=1, keepdims=True))


def _concat_slabs(vt_ref, first, count):
    return jnp.concatenate([vt_ref[0, 0, first + t] for t in range(count)], axis=1)


def _dense_kernel(q_ref, k_ref, vt_ref, kc_ref, vct_ref, z_ref, y_ref,
                  k2_sc, m_sc, l_sc, acc_sc, *, tq, tk, n_steps):
    nq = GROUP * tq
    slabs = tk // SLAB

    def key_tile(j):
        return k_ref[0, 0, pl.ds(pl.multiple_of(j * tk, tk), tk), :]

    def vt_tile(j):
        return _concat_slabs(vt_ref, j * slabs, slabs)

    kc = kc_ref[0, 0]
    vct = _concat_slabs(vct_ref, 0, vct_ref.shape[2])

    @pl.when(pl.program_id(2) == 0)
    def _():
        k2_sc[0] = lax.fori_loop(
            0, n_steps, lambda j, best: jnp.maximum(best, _key_norm2(key_tile(j))), _key_norm2(kc))

    l_sc[...] = jnp.zeros(l_sc.shape, F32)
    acc_sc[...] = jnp.zeros(acc_sc.shape, F32)
    q = [q_ref[0, :, g * HEAD_DIM:(g + 1) * HEAD_DIM] for g in range(GROUP)]

    q2 = functools.reduce(jnp.maximum, [_key_norm2(qg) for qg in q])
    safe = q2 * k2_sc[0] * (ATTN_SCALE * ATTN_SCALE * BOUND_SLACK) <= SAFE_LOGIT * SAFE_LOGIT

    @pl.when(safe)
    def _():
        q_all = jnp.concatenate(q, axis=0)

        def attend(k, vt):
            p = jnp.exp2(_qk(k, q_all) * EXP2_SCALE)
            l_sc[...] += jnp.sum(p.reshape(-1, SUBLANES, nq), axis=0)
            acc_sc[...] += jnp.dot(vt, p.astype(BF16), preferred_element_type=F32)

        n_loop = n_steps // DENSE_UNROLL

        def body(i, carry):
            for u in range(DENSE_UNROLL):
                j = i * DENSE_UNROLL + u
                attend(key_tile(j), vt_tile(j))
            return carry

        lax.fori_loop(0, n_loop, body, 0)
        for j in range(n_loop * DENSE_UNROLL, n_steps):
            attend(key_tile(j), vt_tile(j))
        attend(kc, vct)

    @pl.when(jnp.logical_not(safe))
    def _():
        m_sc[...] = jnp.full(m_sc.shape, -jnp.inf, F32)

        def attend(k, vt):
            for g in range(GROUP):
                st = _qk(k, q[g])
                m_prev = m_sc[g]
                m_new = jnp.maximum(m_prev, jnp.max(st, axis=0, keepdims=True))
                alpha = jnp.exp2((m_prev - m_new) * EXP2_SCALE)
                p = jnp.exp2((st - m_new) * EXP2_SCALE)
                cols = slice(g * tq, (g + 1) * tq)
                l_sc[0:1, cols] = alpha * l_sc[0:1, cols] + jnp.sum(p, axis=0, keepdims=True)
                acc_sc[:, cols] = alpha * acc_sc[:, cols] + jnp.dot(vt, p.astype(BF16),
                                                                     preferred_element_type=F32)
                m_sc[g] = m_new

        def body(j, carry):
            attend(key_tile(j), vt_tile(j))
            return carry

        lax.fori_loop(0, n_steps, body, 0)
        attend(kc, vct)

    o = (acc_sc[...] / jnp.sum(l_sc[...], axis=0, keepdims=True)).T
    o = jnp.concatenate([o[g * tq:(g + 1) * tq] for g in range(GROUP)], axis=1)
    y_ref[0] = (o * _silu(z_ref[0])).astype(y_ref.dtype)


def _kv_specs(L, C, head0):
    return [
        pl.BlockSpec((1, 1, L, HEAD_DIM), lambda b, h, i: (b, head0 + h, 0, 0)),
        pl.BlockSpec((1, 1, L // SLAB, HEAD_DIM, SLAB), lambda b, h, i: (b, head0 + h, 0, 0, 0)),
        pl.BlockSpec((1, 1, C, HEAD_DIM), lambda b, h, i: (b, head0 + h, 0, 0)),
        pl.BlockSpec((1, 1, C // SLAB, HEAD_DIM, SLAB), lambda b, h, i: (b, head0 + h, 0, 0, 0)),
    ]


def _dense_attention(q, k, vt, kc, vct, z):
    B, L, _ = q.shape
    C = kc.shape[2]
    tq = min(ATTN_TQ, L)
    tk = min(ATTN_TK, L)
    assert L % tq == 0 and L % tk == 0 and tk % SLAB == 0 and C % SLAB == 0
    gw = GROUP * HEAD_DIM
    kern = functools.partial(_dense_kernel, tq=tq, tk=tk, n_steps=L // tk)
    qz_spec = pl.BlockSpec((1, tq, gw), lambda b, h, i: (b, i, h))
    return pl.pallas_call(
        kern,
        grid=(B, N_KV_A, L // tq),
        in_specs=[qz_spec] + _kv_specs(L, C, 0) + [qz_spec],
        out_specs=qz_spec,
        out_shape=jax.ShapeDtypeStruct((B, L, N_HEADS_A * HEAD_DIM), BF16),
        scratch_shapes=[
            pltpu.SMEM((1,), F32),
            pltpu.VMEM((GROUP, 1, tq), F32),
            pltpu.VMEM((SUBLANES, GROUP * tq), F32),
            pltpu.VMEM((HEAD_DIM, GROUP * tq), F32),
        ],
        compiler_params=_params(3),
        name="dense_attention",
    )(q, k, vt, kc, vct, z)


def _window_kernel(sink_ref, q_ref, k_ref, vt_ref, kc_ref, vct_ref, z_ref, y_ref, k2_sc,
                   *, tq, sub, seq_len):
    h = pl.program_id(1)
    i = pl.program_id(2)
    band = sub + 2 * WINDOW
    nq = GROUP * sub
    n_sub = tq // sub

    kc = kc_ref[0, 0]
    vct = _concat_slabs(vct_ref, 0, vct_ref.shape[2])

    @pl.when(i == 0)
    def _():
        def kbody(j, best):
            off = pl.multiple_of(j * ATTN_TK, ATTN_TK)
            return jnp.maximum(best, _key_norm2(k_ref[0, 0, pl.ds(off, ATTN_TK), :]))

        k2_sc[0] = lax.fori_loop(0, seq_len // ATTN_TK, kbody, _key_norm2(kc))

    lane_head = lax.broadcasted_iota(jnp.int32, (1, nq), 1) // sub
    sink = jnp.zeros((1, nq), F32)
    sink_max = sink_ref[h * GROUP]
    for g in range(GROUP):
        sink = jnp.where(lane_head == g, sink_ref[h * GROUP + g], sink)
        sink_max = jnp.maximum(sink_max, sink_ref[h * GROUP + g])
    sink = sink * LOG2_E
    rel = (lax.broadcasted_iota(jnp.int32, (band, sub), 0)
           - lax.broadcasted_iota(jnp.int32, (band, sub), 1))
    q_all = [jnp.concatenate([q_ref[0, c * sub:(c + 1) * sub, g * HEAD_DIM:(g + 1) * HEAD_DIM]
                              for g in range(GROUP)], axis=0) for c in range(n_sub)]
    q2 = functools.reduce(jnp.maximum, [_key_norm2(qc) for qc in q_all])
    safe = jnp.logical_and(
        q2 * k2_sc[0] * (ATTN_SCALE * ATTN_SCALE * BOUND_SLACK) <= SAFE_LOGIT * SAFE_LOGIT,
        sink_max <= SAFE_LOGIT)

    def sub_tile(c, exact_max):
        rows = slice(c * sub, (c + 1) * sub)
        start = i * tq + c * sub
        kstart = pl.multiple_of(jnp.clip(start - WINDOW, 0, seq_len - band), WINDOW)
        bias = jnp.where(jnp.abs(rel + (kstart - start)) <= WINDOW, 0.0, NEG_INF)
        s = (_qk(k_ref[0, 0, pl.ds(kstart, band), :], q_all[c]) * EXP2_SCALE
             + jnp.concatenate([bias] * GROUP, axis=1))
        sc = _qk(kc, q_all[c]) * EXP2_SCALE
        if exact_max:
            m = jnp.maximum(jnp.maximum(jnp.max(s, axis=0, keepdims=True),
                                        jnp.max(sc, axis=0, keepdims=True)), sink)
            s, sc, sink_c = s - m, sc - m, sink - m
        else:
            sink_c = sink
        p = jnp.exp2(s)
        pc = jnp.exp2(sc)
        denom = (jnp.sum(p, axis=0, keepdims=True) + jnp.sum(pc, axis=0, keepdims=True)
                 + jnp.exp2(sink_c))
        vbt = _concat_slabs(vt_ref, kstart // SLAB, band // SLAB)
        o = (jnp.dot(vbt, p.astype(BF16), preferred_element_type=F32)
             + jnp.dot(vct, pc.astype(BF16), preferred_element_type=F32)) / denom
        o = o.T
        o = jnp.concatenate([o[g * sub:(g + 1) * sub] for g in range(GROUP)], axis=1)
        y_ref[0, rows, :] = (o * _silu(z_ref[0, rows, :])).astype(y_ref.dtype)

    @pl.when(safe)
    def _():
        for c in range(n_sub):
            sub_tile(c, exact_max=False)

    @pl.when(jnp.logical_not(safe))
    def _():
        for c in range(n_sub):
            sub_tile(c, exact_max=True)


def _window_attention(q, k, vt, kc, vct, z, sink):
    B, L, _ = q.shape
    C = kc.shape[2]
    sub = min(WINDOW_SUB, L - 2 * WINDOW)
    tq = min(WINDOW_TQ, L)
    assert L % tq == 0 and tq % sub == 0 and L >= sub + 2 * WINDOW and L % ATTN_TK == 0
    assert WINDOW % SLAB == 0 and sub % SLAB == 0 and C % SLAB == 0
    gw = GROUP * HEAD_DIM
    q_col = N_HEADS_A // GROUP
    kern = functools.partial(_window_kernel, tq=tq, sub=sub, seq_len=L)
    qz_spec = pl.BlockSpec((1, tq, gw), lambda b, h, i: (b, i, q_col + h))
    return pl.pallas_call(
        kern,
        grid=(B, N_KV_B, L // tq),
        in_specs=[pl.BlockSpec(memory_space=pltpu.SMEM), qz_spec] + _kv_specs(L, C, N_KV_A) + [qz_spec],
        out_specs=pl.BlockSpec((1, tq, gw), lambda b, h, i: (b, i, h)),
        out_shape=jax.ShapeDtypeStruct((B, L, N_HEADS_B * HEAD_DIM), BF16),
        scratch_shapes=[pltpu.SMEM((1,), F32)],
        compiler_params=_params(3),
        name="window_attention",
    )(sink, q, k, vt, kc, vct, z)


def _finish_rows(r, x_rows, gate, post_g):
    return x_rows + gate * _rms(r, post_g)


def _out_kernel(ya_ref, yb_ref, w_ref, x_ref, mod_ref, g_ref, o_ref, *, n_sub, sub):
    gate = mod_ref[0][:, 2 * D_MODEL:]
    for s in range(n_sub):
        rows = slice(s * sub, (s + 1) * sub)
        y = jnp.concatenate([ya_ref[0, rows, :], yb_ref[0, rows, :]], axis=1)
        r = jnp.dot(y, w_ref[...], preferred_element_type=F32)
        o_ref[0, rows, :] = _finish_rows(r, x_ref[0, rows, :], gate, g_ref[...])


def _out_projection(ya, yb, w, x, mod3, post_g):
    B, L, _ = x.shape
    tm = min(ROW_TILE, L)
    sub = min(SUB_ROWS, tm)
    wa = ya.shape[2]
    kern = functools.partial(_out_kernel, n_sub=tm // sub, sub=sub)
    return pl.pallas_call(
        kern,
        grid=(B, L // tm),
        in_specs=[
            pl.BlockSpec((1, tm, wa), lambda b, i: (b, i, 0)),
            pl.BlockSpec((1, tm, yb.shape[2]), lambda b, i: (b, i, 0)),
            _resident(w.shape, lambda b, i: (0, 0)),
            pl.BlockSpec((1, tm, D_MODEL), lambda b, i: (b, i, 0)),
            pl.BlockSpec((1, 1, 3 * D_MODEL), lambda b, i: (b, 0, 0)),
            _resident((1, D_MODEL), lambda b, i: (0, 0)),
        ],
        out_specs=pl.BlockSpec((1, tm, D_MODEL), lambda b, i: (b, i, 0)),
        out_shape=jax.ShapeDtypeStruct((B, L, D_MODEL), F32),
        compiler_params=_params(2),
        name="out_projection",
    )(ya, yb, w, x, mod3, post_g.reshape(1, D_MODEL))


def _pool_kernel(u_ref, up_ref, un_ref, gt_ref, pw_ref, ps_ref, w_ref, x_ref, mod_ref, g_ref,
                 o_ref, *, tm, n_sub, sub, seq_len):
    i = pl.program_id(1)
    gate = mod_ref[0][:, 2 * D_MODEL:]

    for s in range(n_sub):
        r0 = s * sub
        rows = slice(r0, r0 + sub)
        t = i * tm + r0 + lax.broadcasted_iota(jnp.int32, (sub, 1), 0)
        mixed = []
        for g, w in enumerate(POOL_SIZES):
            cols = slice(g * POOL_GROUP, (g + 1) * POOL_GROUP)
            half = w // 2
            if s == 0:
                before = jnp.where(i > 0, up_ref[0, :, cols], 0.0)
            else:
                before = u_ref[0, r0 - POOL_HALO:r0, cols]
            if s == n_sub - 1:
                after = jnp.where(i < pl.num_programs(1) - 1, un_ref[0, :, cols], 0.0)
            else:
                after = u_ref[0, r0 + sub:r0 + sub + POOL_HALO, cols]
            u = u_ref[0, rows, cols]
            ext = jnp.concatenate([before, u, after], axis=0)
            tot = ext[1:] + ext[:-1]
            lo = 1
            span = 1
            while 2 * span < w:
                tot = tot[:-2 * span] + tot[2 * span:]
                lo += span
                span *= 2
            tot = tot[POOL_HALO - lo:POOL_HALO - lo + sub]
            cnt = jnp.minimum(t + half, seq_len) - jnp.maximum(t - half, 0)
            pooled = tot / cnt.astype(F32) - u
            mixed.append(jnp.dot(pooled.astype(BF16), pw_ref[g], preferred_element_type=F32))
        y = jnp.concatenate(mixed, axis=1) * ps_ref[...]
        y = (y * _silu(gt_ref[0, rows, :])).astype(BF16)
        r = jnp.dot(y, w_ref[...], preferred_element_type=F32)
        o_ref[0, rows, :] = _finish_rows(r, x_ref[0, rows, :], gate, g_ref[...])


def _pool_tail(u, gt, pool_w, pool_scale, w_out, x, mod3, post_g):
    B, L, _ = x.shape
    tm = min(ROW_TILE, L)
    sub = min(SUB_ROWS, tm)
    hb = tm // POOL_HALO
    n_hb = L // POOL_HALO
    kern = functools.partial(_pool_kernel, tm=tm, n_sub=tm // sub, sub=sub, seq_len=L)
    row_spec = pl.BlockSpec((1, tm, D_MODEL), lambda b, i: (b, i, 0))
    return pl.pallas_call(
        kern,
        grid=(B, L // tm),
        in_specs=[
            row_spec,
            pl.BlockSpec((1, POOL_HALO, D_MODEL), lambda b, i: (b, jnp.maximum(i * hb - 1, 0), 0)),
            pl.BlockSpec((1, POOL_HALO, D_MODEL),
                         lambda b, i: (b, jnp.minimum((i + 1) * hb, n_hb - 1), 0)),
            row_spec,
            _resident(pool_w.shape, lambda b, i: (0, 0, 0)),
            _resident((1, D_MODEL), lambda b, i: (0, 0)),
            _resident(w_out.shape, lambda b, i: (0, 0)),
            row_spec,
            pl.BlockSpec((1, 1, 3 * D_MODEL), lambda b, i: (b, 0, 0)),
            _resident((1, D_MODEL), lambda b, i: (0, 0)),
        ],
        out_specs=row_spec,
        out_shape=jax.ShapeDtypeStruct((B, L, D_MODEL), F32),
        compiler_params=_params(2),
        name="pool_tail",
    )(u, u, u, gt, pool_w, pool_scale.reshape(1, D_MODEL), w_out, x, mod3,
      post_g.reshape(1, D_MODEL))


def _rope_tables(seq_len):
    rows = seq_len // GRID_W
    inv = ROPE_THETA ** (-jnp.arange(AXIS_FREQS, dtype=F32) / AXIS_FREQS)
    row_ang = jnp.arange(rows, dtype=F32)[:, None] * inv
    col_ang = jnp.arange(GRID_W, dtype=F32)[:, None] * inv

    def table(fn, sign):
        shape = (rows, GRID_W, AXIS_FREQS)
        half = jnp.concatenate([jnp.broadcast_to(fn(row_ang)[:, None, :], shape),
                                jnp.broadcast_to(fn(col_ang)[None, :, :], shape)], axis=-1)
        return jnp.concatenate([sign * half, half], axis=-1).reshape(seq_len, HEAD_DIM)

    return table(jnp.cos, 1.0), table(jnp.sin, -1.0)


def kernel(x, c, ctx, c_ctx, ev_mod_w, ev_mod_b, ev_pre_g, ev_post_g, ev_w_in, ev_q_norm, ev_k_norm,
           ev_sink, ev_w_out, od_mod_w, od_mod_b, od_pre_g, od_post_g, od_w_in, od_pool_w,
           od_pool_scale, od_w_out):
    B, L, D = x.shape
    assert D == D_MODEL and L % GRID_W == 0
    assert ev_mod_w.shape[0] == 1 and od_mod_w.shape[0] == 1
    assert B + 1 <= MOD_ROWS
    ctx_row = B

    cvec = jnp.zeros((MOD_ROWS, D), F32).at[:B].set(c).at[ctx_row].set(c_ctx)
    mod_e = _modulation(cvec, ev_mod_w[0], ev_mod_b[0]).reshape(MOD_ROWS, 1, 3 * D)
    mod_o = _modulation(cvec, od_mod_w[0], od_mod_b[0]).reshape(MOD_ROWS, 1, 3 * D)

    w_in_e = ev_w_in[0].astype(BF16)
    norms = (ev_q_norm[0], ev_k_norm[0])
    n_kv = N_KV_A + N_KV_B

    def kv_plan(k_out, v_out, ka_op, kb_op):
        return ([(ka_op, k_out, TO_HEAD, h) for h in range(N_KV_A)]
                + [(OP_RAW, v_out, TO_SLABS_T, h) for h in range(N_KV_A)]
                + [(kb_op, k_out, TO_HEAD, N_KV_A + h) for h in range(N_KV_B)]
                + [(OP_RAW, v_out, TO_SLABS_T, N_KV_A + h) for h in range(N_KV_B)])

    plan = tuple([(OP_QNORM_ROPE, 0, TO_COLS, h * HEAD_DIM) for h in range(N_HEADS_A)]
                 + [(OP_ROPE, 0, TO_COLS, (N_HEADS_A + h) * HEAD_DIM) for h in range(N_HEADS_B)]
                 + kv_plan(1, 2, OP_KNORM_ROPE, OP_ROPE)
                 + [(OP_RAW, 3, TO_COLS, h * HEAD_DIM) for h in range(Q_W // HEAD_DIM)])
    q, k, vt, z = _projection(
        x, mod_e, None, ev_pre_g[0], w_in_e, plan=plan,
        outs=((TO_COLS, Q_W, BF16), (TO_HEAD, n_kv, BF16), (TO_SLABS_T, n_kv, BF16), (TO_COLS, Q_W, F32)),
        rope=_rope_tables(L), norms=norms, name="in_projection_even")
    kc, vct = _projection(
        ctx, mod_e, ctx_row, ev_pre_g[0], w_in_e[:, Q_W:Q_W + KV_W],
        plan=tuple(kv_plan(0, 1, OP_KNORM, OP_RAW)),
        outs=((TO_HEAD, n_kv, BF16), (TO_SLABS_T, n_kv, BF16)), norms=norms, name="in_projection_ctx")

    ya = _dense_attention(q, k, vt, kc, vct, z)
    yb = _window_attention(q, k, vt, kc, vct, z, ev_sink[0])
    x = _out_projection(ya, yb, ev_w_out[0].astype(BF16), x, mod_e, ev_post_g[0])

    u, gt = _projection(
        x, mod_o, None, od_pre_g[0], od_w_in[0].astype(BF16),
        plan=tuple((OP_RAW, col // D, TO_COLS, col % D) for col in range(0, 2 * D, HEAD_DIM)),
        outs=((TO_COLS, D, F32), (TO_COLS, D, F32)), name="in_projection_odd")
    return _pool_tail(u, gt, od_pool_w[0].astype(BF16), od_pool_scale[0], od_w_out[0].astype(BF16),
                      x, mod_o, od_post_g[0])
```

```python
import functools
import math

import jax
import jax.numpy as jnp
from jax import lax
from jax.experimental import pallas as pl
from jax.experimental.pallas import tpu as pltpu

D_MODEL = 2048
GRID_W = 64
HEAD_DIM = 128
AXIS_FREQS = HEAD_DIM // 4
ROPE_THETA = 10000.0
N_HEADS_A = 8
N_KV_A = 2
N_HEADS_B = 8
N_KV_B = 2
GROUP = N_HEADS_A // N_KV_A
WINDOW = 128
ATTN_SCALE = HEAD_DIM ** -0.5
NEG_INF = -1e30
Q_W = (N_HEADS_A + N_HEADS_B) * HEAD_DIM
KV_A_W = N_KV_A * HEAD_DIM
KV_B_W = N_KV_B * HEAD_DIM
KV_W = 2 * KV_A_W + 2 * KV_B_W
POOL_SIZES = (2, 4, 8, 16)
POOL_GROUP = D_MODEL // len(POOL_SIZES)
POOL_HALO = 8
EPS = 1e-6

LANES = 128
SUBLANES = 8
VMEM_LIMIT_BYTES = 56 * 1024 * 1024

MOD_ROWS = SUBLANES
MOD_TN = 512
ROW_TILE = 512
SUB_ROWS = 256
COL_CHUNK = 512
ATTN_TQ = 1024
ATTN_TK = 1024
DENSE_UNROLL = 2
WINDOW_TQ = 1024
WINDOW_SUB = 128
LOG2_E = math.log2(math.e)
EXP2_SCALE = ATTN_SCALE * LOG2_E
BOUND_SLACK = 1.0 + 2.0 ** -7
SAFE_LOGIT = 50.0

OP_RAW, OP_ROPE, OP_QNORM_ROPE, OP_KNORM_ROPE, OP_KNORM, OP_SILU = range(6)
TO_COLS, TO_HEAD, TO_SLABS_T = range(3)
SLAB = LANES

F32 = jnp.float32
BF16 = jnp.bfloat16


def _params(n_axes):
    return pltpu.CompilerParams(
        dimension_semantics=("arbitrary",) * n_axes,
        vmem_limit_bytes=VMEM_LIMIT_BYTES,
    )


def _resident(block_shape, index_map):
    return pl.BlockSpec(block_shape, index_map, pipeline_mode=pl.Buffered(1))


def _silu(v):
    return v * jax.nn.sigmoid(v)


def _rms(v, gain):
    ms = jnp.mean(v * v, axis=1, keepdims=True)
    return (v * lax.rsqrt(ms + EPS)) * gain


def _mod_kernel(c_ref, w_ref, b_ref, o_ref):
    a = _silu(c_ref[...]).astype(BF16)
    o_ref[...] = jnp.dot(a, w_ref[...].astype(BF16), preferred_element_type=F32) + b_ref[...]


def _modulation(cvec, w, b):
    n = w.shape[1]
    return pl.pallas_call(
        _mod_kernel,
        grid=(n // MOD_TN,),
        in_specs=[
            pl.BlockSpec((MOD_ROWS, D_MODEL), lambda j: (0, 0)),
            pl.BlockSpec((D_MODEL, MOD_TN), lambda j: (0, j)),
            pl.BlockSpec((1, MOD_TN), lambda j: (0, j)),
        ],
        out_specs=pl.BlockSpec((MOD_ROWS, MOD_TN), lambda j: (0, j)),
        out_shape=jax.ShapeDtypeStruct((MOD_ROWS, n), F32),
        compiler_params=_params(1),
        name="modulation",
    )(cvec, w, b.reshape(1, n))


def _proj_kernel(*refs, plan, n_sub, sub, use_rope, use_norm):
    x_ref, mod_ref, g_ref, w_ref = refs[:4]
    pos = 4
    if use_rope:
        cos_ref, sin_ref = refs[pos:pos + 2]
        pos += 2
    if use_norm:
        qn_ref, kn_ref = refs[pos:pos + 2]
        pos += 2
    out_refs = refs[pos:]

    mrow = mod_ref[0]
    shift = mrow[:, :D_MODEL]
    gain = g_ref[...] * (1.0 + mrow[:, D_MODEL:2 * D_MODEL])
    heads_per_chunk = COL_CHUNK // HEAD_DIM
    slabs_per_sub = sub // SLAB

    for s in range(n_sub):
        rows = slice(s * sub, (s + 1) * sub)
        xs = x_ref[0, rows, :]
        ms = jnp.mean(xs * xs, axis=1, keepdims=True)
        h = ((xs * lax.rsqrt(ms + EPS)) * gain + shift).astype(BF16)
        if use_rope:
            cs = cos_ref[rows, :]
            sn = sin_ref[rows, :]
        for c in range(len(plan) // heads_per_chunk):
            acc = jnp.dot(h, w_ref[:, c * COL_CHUNK:(c + 1) * COL_CHUNK],
                          preferred_element_type=F32)
            for hh in range(heads_per_chunk):
                op, o_idx, kind, slot = plan[c * heads_per_chunk + hh]
                t = acc[:, hh * HEAD_DIM:(hh + 1) * HEAD_DIM]
                if op == OP_SILU:
                    t = _silu(t)
                elif op == OP_QNORM_ROPE:
                    t = _rms(t, qn_ref[...])
                elif op in (OP_KNORM_ROPE, OP_KNORM):
                    t = _rms(t, kn_ref[...])
                if op in (OP_ROPE, OP_QNORM_ROPE, OP_KNORM_ROPE):
                    t = t * cs + pltpu.roll(t, HEAD_DIM // 2, 1) * sn
                out = out_refs[o_idx]
                if kind == TO_COLS:
                    out[0, rows, slot:slot + HEAD_DIM] = t.astype(out.dtype)
                elif kind == TO_HEAD:
                    out[0, slot, rows, :] = t.astype(out.dtype)
                else:
                    for r in range(slabs_per_sub):
                        out[0, slot, s * slabs_per_sub + r] = (
                            t[r * SLAB:(r + 1) * SLAB, :].T.astype(out.dtype))


def _projection(x, mod3, mod_row, g, w, *, plan, outs, rope=None, norms=None, name):
    B, L, _ = x.shape
    n_total = w.shape[1]
    assert n_total == len(plan) * HEAD_DIM and n_total % COL_CHUNK == 0
    tm = min(ROW_TILE, L)
    sub = min(SUB_ROWS, tm)
    assert L % tm == 0 and tm % sub == 0 and sub % SLAB == 0
    mod_map = (lambda b, i: (b, 0, 0)) if mod_row is None else (lambda b, i: (mod_row, 0, 0))
    in_specs = [
        pl.BlockSpec((1, tm, D_MODEL), lambda b, i: (b, i, 0)),
        pl.BlockSpec((1, 1, 3 * D_MODEL), mod_map),
        _resident((1, D_MODEL), lambda b, i: (0, 0)),
        _resident((D_MODEL, n_total), lambda b, i: (0, 0)),
    ]
    args = [x, mod3, g.reshape(1, D_MODEL), w]
    if rope is not None:
        in_specs += [pl.BlockSpec((tm, HEAD_DIM), lambda b, i: (i, 0))] * 2
        args += list(rope)
    if norms is not None:
        in_specs += [_resident((1, HEAD_DIM), lambda b, i: (0, 0))] * 2
        args += [n.reshape(1, HEAD_DIM) for n in norms]
    out_specs, out_shape = [], []
    for kind, n, dt in outs:
        if kind == TO_COLS:
            out_specs.append(pl.BlockSpec((1, tm, n), lambda b, i: (b, i, 0)))
            out_shape.append(jax.ShapeDtypeStruct((B, L, n), dt))
        elif kind == TO_HEAD:
            out_specs.append(pl.BlockSpec((1, n, tm, HEAD_DIM), lambda b, i: (b, 0, i, 0)))
            out_shape.append(jax.ShapeDtypeStruct((B, n, L, HEAD_DIM), dt))
        else:
            out_specs.append(pl.BlockSpec((1, n, tm // SLAB, HEAD_DIM, SLAB),
                                          lambda b, i: (b, 0, i, 0, 0)))
            out_shape.append(jax.ShapeDtypeStruct((B, n, L // SLAB, HEAD_DIM, SLAB), dt))
    kern = functools.partial(
        _proj_kernel, plan=plan, n_sub=tm // sub, sub=sub,
        use_rope=rope is not None, use_norm=norms is not None)
    return pl.pallas_call(
        kern,
        grid=(B, L // tm),
        in_specs=in_specs,
        out_specs=out_specs,
        out_shape=out_shape,
        compiler_params=_params(2),
        name=name,
    )(*args)


def _qk(q, k):
    return lax.dot_general(q, k, (((1,), (1,)), ((), ())), preferred_element_type=F32)


def _key_norm2(k):
    kf = k.astype(F32)
    return jnp.max(jnp.sum(kf * kf, axis=1, keepdims=True))


def _concat_slabs(vt_ref, first, count):
    return jnp.concatenate([vt_ref[0, 0, first + t] for t in range(count)], axis=1)


def _dense_kernel(q_ref, k_ref, vt_ref, kc_ref, vct_ref, gz_ref, y_ref,
                  k2_sc, m_sc, l_sc, acc_sc, *, tq, tk, n_steps):
    nq = GROUP * tq
    slabs = tk // SLAB

    def key_tile(j):
        return k_ref[0, 0, pl.ds(pl.multiple_of(j * tk, tk), tk), :]

    def vt_tile(j):
        return _concat_slabs(vt_ref, j * slabs, slabs)

    kc = kc_ref[0, 0]
    vct = _concat_slabs(vct_ref, 0, vct_ref.shape[2])

    @pl.when(pl.program_id(2) == 0)
    def _():
        k2_sc[0] = lax.fori_loop(
            0, n_steps, lambda j, best: jnp.maximum(best, _key_norm2(key_tile(j))), _key_norm2(kc))

    l_sc[...] = jnp.zeros(l_sc.shape, F32)
    acc_sc[...] = jnp.zeros(acc_sc.shape, F32)
    q = [q_ref[0, :, g * HEAD_DIM:(g + 1) * HEAD_DIM] for g in range(GROUP)]

    q2 = functools.reduce(jnp.maximum, [_key_norm2(qg) for qg in q])
    safe = q2 * k2_sc[0] * (ATTN_SCALE * ATTN_SCALE * BOUND_SLACK) <= SAFE_LOGIT * SAFE_LOGIT

    @pl.when(safe)
    def _():
        q_all = jnp.concatenate(q, axis=0)

        def attend(k, vt):
            p = jnp.exp2(_qk(k, q_all) * EXP2_SCALE)
            l_sc[...] += jnp.sum(p.reshape(-1, SUBLANES, nq), axis=0)
            acc_sc[...] += jnp.dot(vt, p.astype(BF16), preferred_element_type=F32)

        n_loop = n_steps // DENSE_UNROLL

        def body(i, carry):
            for u in range(DENSE_UNROLL):
                j = i * DENSE_UNROLL + u
                attend(key_tile(j), vt_tile(j))
            return carry

        lax.fori_loop(0, n_loop, body, 0)
        for j in range(n_loop * DENSE_UNROLL, n_steps):
            attend(key_tile(j), vt_tile(j))
        attend(kc, vct)

    @pl.when(jnp.logical_not(safe))
    def _():
        m_sc[...] = jnp.full(m_sc.shape, -jnp.inf, F32)

        def attend(k, vt):
            for g in range(GROUP):
                st = _qk(k, q[g])
                m_prev = m_sc[g]
                m_new = jnp.maximum(m_prev, jnp.max(st, axis=0, keepdims=True))
                alpha = jnp.exp2((m_prev - m_new) * EXP2_SCALE)
                p = jnp.exp2((st - m_new) * EXP2_SCALE)
                cols = slice(g * tq, (g + 1) * tq)
                l_sc[0:1, cols] = alpha * l_sc[0:1, cols] + jnp.sum(p, axis=0, keepdims=True)
                acc_sc[:, cols] = alpha * acc_sc[:, cols] + jnp.dot(vt, p.astype(BF16),
                                                                     preferred_element_type=F32)
                m_sc[g] = m_new

        def body(j, carry):
            attend(key_tile(j), vt_tile(j))
            return carry

        lax.fori_loop(0, n_steps, body, 0)
        attend(kc, vct)

    o = (acc_sc[...] / jnp.sum(l_sc[...], axis=0, keepdims=True)).T
    o = jnp.concatenate([o[g * tq:(g + 1) * tq] for g in range(GROUP)], axis=1)
    y_ref[0] = (o * gz_ref[0]).astype(y_ref.dtype)


def _kv_specs(L, C, head0):
    return [
        pl.BlockSpec((1, 1, L, HEAD_DIM), lambda b, h, i: (b, head0 + h, 0, 0)),
        pl.BlockSpec((1, 1, L // SLAB, HEAD_DIM, SLAB), lambda b, h, i: (b, head0 + h, 0, 0, 0)),
        pl.BlockSpec((1, 1, C, HEAD_DIM), lambda b, h, i: (b, head0 + h, 0, 0)),
        pl.BlockSpec((1, 1, C // SLAB, HEAD_DIM, SLAB), lambda b, h, i: (b, head0 + h, 0, 0, 0)),
    ]


def _dense_attention(q, k, vt, kc, vct, gz):
    B, L, _ = q.shape
    C = kc.shape[2]
    tq = min(ATTN_TQ, L)
    tk = min(ATTN_TK, L)
    assert L % tq == 0 and L % tk == 0 and tk % SLAB == 0 and C % SLAB == 0
    gw = GROUP * HEAD_DIM
    kern = functools.partial(_dense_kernel, tq=tq, tk=tk, n_steps=L // tk)
    qz_spec = pl.BlockSpec((1, tq, gw), lambda b, h, i: (b, i, h))
    return pl.pallas_call(
        kern,
        grid=(B, N_KV_A, L // tq),
        in_specs=[qz_spec] + _kv_specs(L, C, 0) + [qz_spec],
        out_specs=qz_spec,
        out_shape=jax.ShapeDtypeStruct((B, L, N_HEADS_A * HEAD_DIM), BF16),
        scratch_shapes=[
            pltpu.SMEM((1,), F32),
            pltpu.VMEM((GROUP, 1, tq), F32),
            pltpu.VMEM((SUBLANES, GROUP * tq), F32),
            pltpu.VMEM((HEAD_DIM, GROUP * tq), F32),
        ],
        compiler_params=_params(3),
        name="dense_attention",
    )(q, k, vt, kc, vct, gz)


def _window_kernel(sink_ref, q_ref, k_ref, vt_ref, kc_ref, vct_ref, gz_ref, y_ref, k2_sc,
                   *, tq, sub, seq_len):
    h = pl.program_id(1)
    i = pl.program_id(2)
    band = sub + 2 * WINDOW
    nq = GROUP * sub
    n_sub = tq // sub

    kc = kc_ref[0, 0]
    vct = _concat_slabs(vct_ref, 0, vct_ref.shape[2])

    @pl.when(i == 0)
    def _():
        def kbody(j, best):
            off = pl.multiple_of(j * ATTN_TK, ATTN_TK)
            return jnp.maximum(best, _key_norm2(k_ref[0, 0, pl.ds(off, ATTN_TK), :]))

        k2_sc[0] = lax.fori_loop(0, seq_len // ATTN_TK, kbody, _key_norm2(kc))

    lane_head = lax.broadcasted_iota(jnp.int32, (1, nq), 1) // sub
    sink = jnp.zeros((1, nq), F32)
    sink_max = sink_ref[h * GROUP]
    for g in range(GROUP):
        sink = jnp.where(lane_head == g, sink_ref[h * GROUP + g], sink)
        sink_max = jnp.maximum(sink_max, sink_ref[h * GROUP + g])
    sink = sink * LOG2_E
    rel = (lax.broadcasted_iota(jnp.int32, (band, sub), 0)
           - lax.broadcasted_iota(jnp.int32, (band, sub), 1))
    q_all = [jnp.concatenate([q_ref[0, c * sub:(c + 1) * sub, g * HEAD_DIM:(g + 1) * HEAD_DIM]
                              for g in range(GROUP)], axis=0) for c in range(n_sub)]
    q2 = functools.reduce(jnp.maximum, [_key_norm2(qc) for qc in q_all])
    safe = jnp.logical_and(
        q2 * k2_sc[0] * (ATTN_SCALE * ATTN_SCALE * BOUND_SLACK) <= SAFE_LOGIT * SAFE_LOGIT,
        sink_max <= SAFE_LOGIT)

    def sub_tile(c, exact_max):
        rows = slice(c * sub, (c + 1) * sub)
        start = i * tq + c * sub
        kstart = pl.multiple_of(jnp.clip(start - WINDOW, 0, seq_len - band), WINDOW)
        bias = jnp.where(jnp.abs(rel + (kstart - start)) <= WINDOW, 0.0, NEG_INF)
        s = (_qk(k_ref[0, 0, pl.ds(kstart, band), :], q_all[c]) * EXP2_SCALE
             + jnp.concatenate([bias] * GROUP, axis=1))
        sc = _qk(kc, q_all[c]) * EXP2_SCALE
        if exact_max:
            m = jnp.maximum(jnp.maximum(jnp.max(s, axis=0, keepdims=True),
                                        jnp.max(sc, axis=0, keepdims=True)), sink)
            s, sc, sink_c = s - m, sc - m, sink - m
        else:
            sink_c = sink
        p = jnp.exp2(s)
        pc = jnp.exp2(sc)
        denom = (jnp.sum(p, axis=0, keepdims=True) + jnp.sum(pc, axis=0, keepdims=True)
                 + jnp.exp2(sink_c))
        vbt = _concat_slabs(vt_ref, kstart // SLAB, band // SLAB)
        o = (jnp.dot(vbt, p.astype(BF16), preferred_element_type=F32)
             + jnp.dot(vct, pc.astype(BF16), preferred_element_type=F32)) / denom
        o = o.T
        o = jnp.concatenate([o[g * sub:(g + 1) * sub] for g in range(GROUP)], axis=1)
        y_ref[0, rows, :] = (o * gz_ref[0, rows, :]).astype(y_ref.dtype)

    @pl.when(safe)
    def _():
        for c in range(n_sub):
            sub_tile(c, exact_max=False)

    @pl.when(jnp.logical_not(safe))
    def _():
        for c in range(n_sub):
            sub_tile(c, exact_max=True)


def _window_attention(q, k, vt, kc, vct, gz, sink):
    B, L, _ = q.shape
    C = kc.shape[2]
    sub = min(WINDOW_SUB, L - 2 * WINDOW)
    tq = min(WINDOW_TQ, L)
    assert L % tq == 0 and tq % sub == 0 and L >= sub + 2 * WINDOW and L % ATTN_TK == 0
    assert WINDOW % SLAB == 0 and sub % SLAB == 0 and C % SLAB == 0
    gw = GROUP * HEAD_DIM
    q_col = N_HEADS_A // GROUP
    kern = functools.partial(_window_kernel, tq=tq, sub=sub, seq_len=L)
    qz_spec = pl.BlockSpec((1, tq, gw), lambda b, h, i: (b, i, q_col + h))
    return pl.pallas_call(
        kern,
        grid=(B, N_KV_B, L // tq),
        in_specs=[pl.BlockSpec(memory_space=pltpu.SMEM), qz_spec] + _kv_specs(L, C, N_KV_A) + [qz_spec],
        out_specs=pl.BlockSpec((1, tq, gw), lambda b, h, i: (b, i, h)),
        out_shape=jax.ShapeDtypeStruct((B, L, N_HEADS_B * HEAD_DIM), BF16),
        scratch_shapes=[pltpu.SMEM((1,), F32)],
        compiler_params=_params(3),
        name="window_attention",
    )(sink, q, k, vt, kc, vct, gz)


def _finish_rows(r, x_rows, gated_gain):
    return x_rows + _rms(r, gated_gain)


def _out_kernel(ya_ref, yb_ref, w_ref, x_ref, mod_ref, g_ref, o_ref, *, n_sub, sub):
    gated_gain = mod_ref[0][:, 2 * D_MODEL:] * g_ref[...]
    for s in range(n_sub):
        rows = slice(s * sub, (s + 1) * sub)
        y = jnp.concatenate([ya_ref[0, rows, :], yb_ref[0, rows, :]], axis=1)
        r = jnp.dot(y, w_ref[...], preferred_element_type=F32)
        o_ref[0, rows, :] = _finish_rows(r, x_ref[0, rows, :], gated_gain)


def _out_projection(ya, yb, w, x, mod3, post_g):
    B, L, _ = x.shape
    tm = min(ROW_TILE, L)
    sub = min(SUB_ROWS, tm)
    wa = ya.shape[2]
    kern = functools.partial(_out_kernel, n_sub=tm // sub, sub=sub)
    return pl.pallas_call(
        kern,
        grid=(B, L // tm),
        in_specs=[
            pl.BlockSpec((1, tm, wa), lambda b, i: (b, i, 0)),
            pl.BlockSpec((1, tm, yb.shape[2]), lambda b, i: (b, i, 0)),
            _resident(w.shape, lambda b, i: (0, 0)),
            pl.BlockSpec((1, tm, D_MODEL), lambda b, i: (b, i, 0)),
            pl.BlockSpec((1, 1, 3 * D_MODEL), lambda b, i: (b, 0, 0)),
            _resident((1, D_MODEL), lambda b, i: (0, 0)),
        ],
        out_specs=pl.BlockSpec((1, tm, D_MODEL), lambda b, i: (b, i, 0)),
        out_shape=jax.ShapeDtypeStruct((B, L, D_MODEL), F32),
        compiler_params=_params(2),
        name="out_projection",
    )(ya, yb, w, x, mod3, post_g.reshape(1, D_MODEL))


def _pool_kernel(u_ref, up_ref, un_ref, gt_ref, pw_ref, ps_ref, w_ref, x_ref, mod_ref, g_ref,
                 o_ref, *, tm, n_sub, sub, seq_len):
    i = pl.program_id(1)
    gated_gain = mod_ref[0][:, 2 * D_MODEL:] * g_ref[...]

    for s in range(n_sub):
        r0 = s * sub
        rows = slice(r0, r0 + sub)
        t = i * tm + r0 + lax.broadcasted_iota(jnp.int32, (sub, 1), 0)
        mixed = []
        for g, w in enumerate(POOL_SIZES):
            cols = slice(g * POOL_GROUP, (g + 1) * POOL_GROUP)
            half = w // 2
            if s == 0:
                before = jnp.where(i > 0, up_ref[0, :, cols], 0.0)
            else:
                before = u_ref[0, r0 - POOL_HALO:r0, cols]
            if s == n_sub - 1:
                after = jnp.where(i < pl.num_programs(1) - 1, un_ref[0, :, cols], 0.0)
            else:
                after = u_ref[0, r0 + sub:r0 + sub + POOL_HALO, cols]
            u = u_ref[0, rows, cols]
            ext = jnp.concatenate([before, u, after], axis=0)
            tot = ext[1:] + ext[:-1]
            lo = 1
            span = 1
            while 2 * span < w:
                tot = tot[:-2 * span] + tot[2 * span:]
                lo += span
                span *= 2
            tot = tot[POOL_HALO - lo:POOL_HALO - lo + sub]
            cnt = jnp.minimum(t + half, seq_len) - jnp.maximum(t - half, 0)
            pooled = tot / cnt.astype(F32) - u
            mixed.append(jnp.dot(pooled.astype(BF16), pw_ref[g], preferred_element_type=F32))
        y = jnp.concatenate(mixed, axis=1) * ps_ref[...]
        y = (y * _silu(gt_ref[0, rows, :])).astype(BF16)
        r = jnp.dot(y, w_ref[...], preferred_element_type=F32)
        o_ref[0, rows, :] = _finish_rows(r, x_ref[0, rows, :], gated_gain)


def _pool_tail(u, gt, pool_w, pool_scale, w_out, x, mod3, post_g):
    B, L, _ = x.shape
    tm = min(ROW_TILE, L)
    sub = min(SUB_ROWS, tm)
    hb = tm // POOL_HALO
    n_hb = L // POOL_HALO
    kern = functools.partial(_pool_kernel, tm=tm, n_sub=tm // sub, sub=sub, seq_len=L)
    row_spec = pl.BlockSpec((1, tm, D_MODEL), lambda b, i: (b, i, 0))
    return pl.pallas_call(
        kern,
        grid=(B, L // tm),
        in_specs=[
            row_spec,
            pl.BlockSpec((1, POOL_HALO, D_MODEL), lambda b, i: (b, jnp.maximum(i * hb - 1, 0), 0)),
            pl.BlockSpec((1, POOL_HALO, D_MODEL),
                         lambda b, i: (b, jnp.minimum((i + 1) * hb, n_hb - 1), 0)),
            row_spec,
            _resident(pool_w.shape, lambda b, i: (0, 0, 0)),
            _resident((1, D_MODEL), lambda b, i: (0, 0)),
            _resident(w_out.shape, lambda b, i: (0, 0)),
            row_spec,
            pl.BlockSpec((1, 1, 3 * D_MODEL), lambda b, i: (b, 0, 0)),
            _resident((1, D_MODEL), lambda b, i: (0, 0)),
        ],
        out_specs=row_spec,
        out_shape=jax.ShapeDtypeStruct((B, L, D_MODEL), F32),
        compiler_params=_params(2),
        name="pool_tail",
    )(u, u, u, gt, pool_w, pool_scale.reshape(1, D_MODEL), w_out, x, mod3,
      post_g.reshape(1, D_MODEL))


def _rope_tables(seq_len):
    rows = seq_len // GRID_W
    inv = ROPE_THETA ** (-jnp.arange(AXIS_FREQS, dtype=F32) / AXIS_FREQS)
    row_ang = jnp.arange(rows, dtype=F32)[:, None] * inv
    col_ang = jnp.arange(GRID_W, dtype=F32)[:, None] * inv

    def table(fn, sign):
        shape = (rows, GRID_W, AXIS_FREQS)
        half = jnp.concatenate([jnp.broadcast_to(fn(row_ang)[:, None, :], shape),
                                jnp.broadcast_to(fn(col_ang)[None, :, :], shape)], axis=-1)
        return jnp.concatenate([sign * half, half], axis=-1).reshape(seq_len, HEAD_DIM)

    return table(jnp.cos, 1.0), table(jnp.sin, -1.0)


def kernel(x, c, ctx, c_ctx, ev_mod_w, ev_mod_b, ev_pre_g, ev_post_g, ev_w_in, ev_q_norm, ev_k_norm,
           ev_sink, ev_w_out, od_mod_w, od_mod_b, od_pre_g, od_post_g, od_w_in, od_pool_w,
           od_pool_scale, od_w_out):
    B, L, D = x.shape
    assert D == D_MODEL and L % GRID_W == 0
    assert ev_mod_w.shape[0] == 1 and od_mod_w.shape[0] == 1
    assert B + 1 <= MOD_ROWS
    ctx_row = B

    cvec = jnp.zeros((MOD_ROWS, D), F32).at[:B].set(c).at[ctx_row].set(c_ctx)
    mod_e = _modulation(cvec, ev_mod_w[0], ev_mod_b[0]).reshape(MOD_ROWS, 1, 3 * D)
    mod_o = _modulation(cvec, od_mod_w[0], od_mod_b[0]).reshape(MOD_ROWS, 1, 3 * D)

    w_in_e = ev_w_in[0].astype(BF16)
    norms = (ev_q_norm[0], ev_k_norm[0])
    n_kv = N_KV_A + N_KV_B

    def kv_plan(k_out, v_out, ka_op, kb_op):
        return ([(ka_op, k_out, TO_HEAD, h) for h in range(N_KV_A)]
                + [(OP_RAW, v_out, TO_SLABS_T, h) for h in range(N_KV_A)]
                + [(kb_op, k_out, TO_HEAD, N_KV_A + h) for h in range(N_KV_B)]
                + [(OP_RAW, v_out, TO_SLABS_T, N_KV_A + h) for h in range(N_KV_B)])

    plan = tuple([(OP_QNORM_ROPE, 0, TO_COLS, h * HEAD_DIM) for h in range(N_HEADS_A)]
                 + [(OP_ROPE, 0, TO_COLS, (N_HEADS_A + h) * HEAD_DIM) for h in range(N_HEADS_B)]
                 + kv_plan(1, 2, OP_KNORM_ROPE, OP_ROPE)
                 + [(OP_SILU, 3, TO_COLS, h * HEAD_DIM) for h in range(Q_W // HEAD_DIM)])
    q, k, vt, gz = _projection(
        x, mod_e, None, ev_pre_g[0], w_in_e, plan=plan,
        outs=((TO_COLS, Q_W, BF16), (TO_HEAD, n_kv, BF16), (TO_SLABS_T, n_kv, BF16), (TO_COLS, Q_W, F32)),
        rope=_rope_tables(L), norms=norms, name="in_projection_even")
    kc, vct = _projection(
        ctx, mod_e, ctx_row, ev_pre_g[0], w_in_e[:, Q_W:Q_W + KV_W],
        plan=tuple(kv_plan(0, 1, OP_KNORM, OP_RAW)),
        outs=((TO_HEAD, n_kv, BF16), (TO_SLABS_T, n_kv, BF16)), norms=norms, name="in_projection_ctx")

    ya = _dense_attention(q, k, vt, kc, vct, gz)
    yb = _window_attention(q, k, vt, kc, vct, gz, ev_sink[0])
    x = _out_projection(ya, yb, ev_w_out[0].astype(BF16), x, mod_e, ev_post_g[0])

    u, gt = _projection(
        x, mod_o, None, od_pre_g[0], od_w_in[0].astype(BF16),
        plan=tuple((OP_RAW, col // D, TO_COLS, col % D) for col in range(0, 2 * D, HEAD_DIM)),
        outs=((TO_COLS, D, F32), (TO_COLS, D, F32)), name="in_projection_odd")
    return _pool_tail(u, gt, od_pool_w[0].astype(BF16), od_pool_scale[0], od_w_out[0].astype(BF16),
                      x, mod_o, od_post_g[0])
```

```python
import functools
import math

import jax
import jax.numpy as jnp
from jax import lax
from jax.experimental import pallas as pl
from jax.experimental.pallas import tpu as pltpu

D_MODEL = 2048
GRID_W = 64
HEAD_DIM = 128
AXIS_FREQS = HEAD_DIM // 4
ROPE_THETA = 10000.0
N_HEADS_A = 8
N_KV_A = 2
N_HEADS_B = 8
N_KV_B = 2
GROUP = N_HEADS_A // N_KV_A
WINDOW = 128
ATTN_SCALE = HEAD_DIM ** -0.5
NEG_INF = -1e30
Q_W = (N_HEADS_A + N_HEADS_B) * HEAD_DIM
KV_A_W = N_KV_A * HEAD_DIM
KV_B_W = N_KV_B * HEAD_DIM
KV_W = 2 * KV_A_W + 2 * KV_B_W
POOL_SIZES = (2, 4, 8, 16)
POOL_GROUP = D_MODEL // len(POOL_SIZES)
POOL_HALO = 8
EPS = 1e-6

LANES = 128
SUBLANES = 8
VMEM_LIMIT_BYTES = 56 * 1024 * 1024

MOD_ROWS = SUBLANES
MOD_TN = 512
ROW_TILE = 512
SUB_ROWS = 256
COL_CHUNK = 512
ATTN_TQ = 1024
ATTN_TK = 1024
DENSE_UNROLL = 4
WINDOW_TQ = 1024
WINDOW_SUB = 128
LOG2_E = math.log2(math.e)
EXP2_SCALE = ATTN_SCALE * LOG2_E
BOUND_SLACK = 1.0 + 2.0 ** -7
SAFE_LOGIT = 50.0

OP_RAW, OP_ROPE, OP_QNORM_ROPE, OP_KNORM_ROPE, OP_KNORM, OP_SILU = range(6)
TO_COLS, TO_HEAD, TO_SLABS_T = range(3)
SLAB = LANES

F32 = jnp.float32
BF16 = jnp.bfloat16


def _params(n_axes):
    return pltpu.CompilerParams(
        dimension_semantics=("arbitrary",) * n_axes,
        vmem_limit_bytes=VMEM_LIMIT_BYTES,
    )


def _resident(block_shape, index_map):
    return pl.BlockSpec(block_shape, index_map, pipeline_mode=pl.Buffered(1))


def _silu(v):
    return v * jax.nn.sigmoid(v)


def _rms(v, gain):
    ms = jnp.mean(v * v, axis=1, keepdims=True)
    return (v * lax.rsqrt(ms + EPS)) * gain


def _mod_kernel(c_ref, w_ref, b_ref, o_ref):
    a = _silu(c_ref[...]).astype(BF16)
    o_ref[...] = jnp.dot(a, w_ref[...].astype(BF16), preferred_element_type=F32) + b_ref[...]


def _modulation(cvec, w, b):
    n = w.shape[1]
    return pl.pallas_call(
        _mod_kernel,
        grid=(n // MOD_TN,),
        in_specs=[
            pl.BlockSpec((MOD_ROWS, D_MODEL), lambda j: (0, 0)),
            pl.BlockSpec((D_MODEL, MOD_TN), lambda j: (0, j)),
            pl.BlockSpec((1, MOD_TN), lambda j: (0, j)),
        ],
        out_specs=pl.BlockSpec((MOD_ROWS, MOD_TN), lambda j: (0, j)),
        out_shape=jax.ShapeDtypeStruct((MOD_ROWS, n), F32),
        compiler_params=_params(1),
        name="modulation",
    )(cvec, w, b.reshape(1, n))


def _proj_kernel(*refs, plan, n_sub, sub, use_rope, use_norm):
    x_ref, mod_ref, g_ref, w_ref = refs[:4]
    pos = 4
    if use_rope:
        cos_ref, sin_ref = refs[pos:pos + 2]
        pos += 2
    if use_norm:
        qn_ref, kn_ref = refs[pos:pos + 2]
        pos += 2
    out_refs = refs[pos:]

    mrow = mod_ref[0]
    shift = mrow[:, :D_MODEL]
    gain = g_ref[...] * (1.0 + mrow[:, D_MODEL:2 * D_MODEL])
    heads_per_chunk = COL_CHUNK // HEAD_DIM
    slabs_per_sub = sub // SLAB

    for s in range(n_sub):
        rows = slice(s * sub, (s + 1) * sub)
        xs = x_ref[0, rows, :]
        ms = jnp.mean(xs * xs, axis=1, keepdims=True)
        h = ((xs * lax.rsqrt(ms + EPS)) * gain + shift).astype(BF16)
        if use_rope:
            cs = cos_ref[rows, :]
            sn = sin_ref[rows, :]
        for c in range(len(plan) // heads_per_chunk):
            acc = jnp.dot(h, w_ref[:, c * COL_CHUNK:(c + 1) * COL_CHUNK],
                          preferred_element_type=F32)
            for hh in range(heads_per_chunk):
                op, o_idx, kind, slot = plan[c * heads_per_chunk + hh]
                t = acc[:, hh * HEAD_DIM:(hh + 1) * HEAD_DIM]
                if op == OP_SILU:
                    t = _silu(t)
                elif op == OP_QNORM_ROPE:
                    t = _rms(t, qn_ref[...])
                elif op in (OP_KNORM_ROPE, OP_KNORM):
                    t = _rms(t, kn_ref[...])
                if op in (OP_ROPE, OP_QNORM_ROPE, OP_KNORM_ROPE):
                    t = t * cs + pltpu.roll(t, HEAD_DIM // 2, 1) * sn
                out = out_refs[o_idx]
                if kind == TO_COLS:
                    out[0, rows, slot:slot + HEAD_DIM] = t.astype(out.dtype)
                elif kind == TO_HEAD:
                    out[0, slot, rows, :] = t.astype(out.dtype)
                else:
                    for r in range(slabs_per_sub):
                        out[0, slot, s * slabs_per_sub + r] = (
                            t[r * SLAB:(r + 1) * SLAB, :].T.astype(out.dtype))


def _projection(x, mod3, mod_row, g, w, *, plan, outs, rope=None, norms=None, name):
    B, L, _ = x.shape
    n_total = w.shape[1]
    assert n_total == len(plan) * HEAD_DIM and n_total % COL_CHUNK == 0
    tm = min(ROW_TILE, L)
    sub = min(SUB_ROWS, tm)
    assert L % tm == 0 and tm % sub == 0 and sub % SLAB == 0
    mod_map = (lambda b, i: (b, 0, 0)) if mod_row is None else (lambda b, i: (mod_row, 0, 0))
    in_specs = [
        pl.BlockSpec((1, tm, D_MODEL), lambda b, i: (b, i, 0)),
        pl.BlockSpec((1, 1, 3 * D_MODEL), mod_map),
        _resident((1, D_MODEL), lambda b, i: (0, 0)),
        _resident((D_MODEL, n_total), lambda b, i: (0, 0)),
    ]
    args = [x, mod3, g.reshape(1, D_MODEL), w]
    if rope is not None:
        in_specs += [pl.BlockSpec((tm, HEAD_DIM), lambda b, i: (i, 0))] * 2
        args += list(rope)
    if norms is not None:
        in_specs += [_resident((1, HEAD_DIM), lambda b, i: (0, 0))] * 2
        args += [n.reshape(1, HEAD_DIM) for n in norms]
    out_specs, out_shape = [], []
    for kind, n, dt in outs:
        if kind == TO_COLS:
            out_specs.append(pl.BlockSpec((1, tm, n), lambda b, i: (b, i, 0)))
            out_shape.append(jax.ShapeDtypeStruct((B, L, n), dt))
        elif kind == TO_HEAD:
            out_specs.append(pl.BlockSpec((1, n, tm, HEAD_DIM), lambda b, i: (b, 0, i, 0)))
            out_shape.append(jax.ShapeDtypeStruct((B, n, L, HEAD_DIM), dt))
        else:
            out_specs.append(pl.BlockSpec((1, n, tm // SLAB, HEAD_DIM, SLAB),
                                          lambda b, i: (b, 0, i, 0, 0)))
            out_shape.append(jax.ShapeDtypeStruct((B, n, L // SLAB, HEAD_DIM, SLAB), dt))
    kern = functools.partial(
        _proj_kernel, plan=plan, n_sub=tm // sub, sub=sub,
        use_rope=rope is not None, use_norm=norms is not None)
    return pl.pallas_call(
        kern,
        grid=(B, L // tm),
        in_specs=in_specs,
        out_specs=out_specs,
        out_shape=out_shape,
        compiler_params=_params(2),
        name=name,
    )(*args)


def _qk(q, k):
    return lax.dot_general(q, k, (((1,), (1,)), ((), ())), preferred_element_type=F32)


def _key_norm2(k):
    kf = k.astype(F32)
    return jnp.max(jnp.sum(kf * kf, axis=1, keepdims=True))


def _concat_slabs(vt_ref, first, count):
    return jnp.concatenate([vt_ref[0, 0, first + t] for t in range(count)], axis=1)


def _dense_kernel(q_ref, k_ref, vt_ref, kc_ref, vct_ref, gz_ref, y_ref,
                  k2_sc, m_sc, l_sc, acc_sc, *, tq, tk, n_steps):
    nq = GROUP * tq
    slabs = tk // SLAB

    def key_tile(j):
        return k_ref[0, 0, pl.ds(pl.multiple_of(j * tk, tk), tk), :]

    def vt_tile(j):
        return _concat_slabs(vt_ref, j * slabs, slabs)

    kc = kc_ref[0, 0]
    vct = _concat_slabs(vct_ref, 0, vct_ref.shape[2])

    @pl.when(pl.program_id(2) == 0)
    def _():
        k2_sc[0] = lax.fori_loop(
            0, n_steps, lambda j, best: jnp.maximum(best, _key_norm2(key_tile(j))), _key_norm2(kc))

    l_sc[...] = jnp.zeros(l_sc.shape, F32)
    acc_sc[...] = jnp.zeros(acc_sc.shape, F32)
    q = [q_ref[0, :, g * HEAD_DIM:(g + 1) * HEAD_DIM] for g in range(GROUP)]

    q2 = functools.reduce(jnp.maximum, [_key_norm2(qg) for qg in q])
    safe = q2 * k2_sc[0] * (ATTN_SCALE * ATTN_SCALE * BOUND_SLACK) <= SAFE_LOGIT * SAFE_LOGIT

    @pl.when(safe)
    def _():
        q_all = jnp.concatenate(q, axis=0)

        def attend(k, vt):
            p = jnp.exp2(_qk(k, q_all) * EXP2_SCALE)
            l_sc[...] += jnp.sum(p.reshape(-1, SUBLANES, nq), axis=0)
            acc_sc[...] += jnp.dot(vt, p.astype(BF16), preferred_element_type=F32)

        n_loop = n_steps // DENSE_UNROLL

        def body(i, carry):
            for u in range(DENSE_UNROLL):
                j = i * DENSE_UNROLL + u
                attend(key_tile(j), vt_tile(j))
            return carry

        lax.fori_loop(0, n_loop, body, 0)
        for j in range(n_loop * DENSE_UNROLL, n_steps):
            attend(key_tile(j), vt_tile(j))
        attend(kc, vct)

    @pl.when(jnp.logical_not(safe))
    def _():
        m_sc[...] = jnp.full(m_sc.shape, -jnp.inf, F32)

        def attend(k, vt):
            for g in range(GROUP):
                st = _qk(k, q[g])
                m_prev = m_sc[g]
                m_new = jnp.maximum(m_prev, jnp.max(st, axis=0, keepdims=True))
                alpha = jnp.exp2((m_prev - m_new) * EXP2_SCALE)
                p = jnp.exp2((st - m_new) * EXP2_SCALE)
                cols = slice(g * tq, (g + 1) * tq)
                l_sc[0:1, cols] = alpha * l_sc[0:1, cols] + jnp.sum(p, axis=0, keepdims=True)
                acc_sc[:, cols] = alpha * acc_sc[:, cols] + jnp.dot(vt, p.astype(BF16),
                                                                     preferred_element_type=F32)
                m_sc[g] = m_new

        def body(j, carry):
            attend(key_tile(j), vt_tile(j))
            return carry

        lax.fori_loop(0, n_steps, body, 0)
        attend(kc, vct)

    o = (acc_sc[...] / jnp.sum(l_sc[...], axis=0, keepdims=True)).T
    o = jnp.concatenate([o[g * tq:(g + 1) * tq] for g in range(GROUP)], axis=1)
    y_ref[0] = (o * gz_ref[0]).astype(y_ref.dtype)


def _kv_specs(L, C, head0):
    return [
        pl.BlockSpec((1, 1, L, HEAD_DIM), lambda b, h, i: (b, head0 + h, 0, 0)),
        pl.BlockSpec((1, 1, L // SLAB, HEAD_DIM, SLAB), lambda b, h, i: (b, head0 + h, 0, 0, 0)),
        pl.BlockSpec((1, 1, C, HEAD_DIM), lambda b, h, i: (b, head0 + h, 0, 0)),
        pl.BlockSpec((1, 1, C // SLAB, HEAD_DIM, SLAB), lambda b, h, i: (b, head0 + h, 0, 0, 0)),
    ]


def _dense_attention(q, k, vt, kc, vct, gz):
    B, L, _ = q.shape
    C = kc.shape[2]
    tq = min(ATTN_TQ, L)
    tk = min(ATTN_TK, L)
    assert L % tq == 0 and L % tk == 0 and tk % SLAB == 0 and C % SLAB == 0
    gw = GROUP * HEAD_DIM
    kern = functools.partial(_dense_kernel, tq=tq, tk=tk, n_steps=L // tk)
    qz_spec = pl.BlockSpec((1, tq, gw), lambda b, h, i: (b, i, h))
    return pl.pallas_call(
        kern,
        grid=(B, N_KV_A, L // tq),
        in_specs=[qz_spec] + _kv_specs(L, C, 0) + [qz_spec],
        out_specs=qz_spec,
        out_shape=jax.ShapeDtypeStruct((B, L, N_HEADS_A * HEAD_DIM), BF16),
        scratch_shapes=[
            pltpu.SMEM((1,), F32),
            pltpu.VMEM((GROUP, 1, tq), F32),
            pltpu.VMEM((SUBLANES, GROUP * tq), F32),
            pltpu.VMEM((HEAD_DIM, GROUP * tq), F32),
        ],
        compiler_params=_params(3),
        name="dense_attention",
    )(q, k, vt, kc, vct, gz)


def _window_kernel(sink_ref, q_ref, k_ref, vt_ref, kc_ref, vct_ref, gz_ref, y_ref, k2_sc,
                   *, tq, sub, seq_len):
    h = pl.program_id(1)
    i = pl.program_id(2)
    band = sub + 2 * WINDOW
    nq = GROUP * sub
    n_sub = tq // sub

    kc = kc_ref[0, 0]
    vct = _concat_slabs(vct_ref, 0, vct_ref.shape[2])

    @pl.when(i == 0)
    def _():
        def kbody(j, best):
            off = pl.multiple_of(j * ATTN_TK, ATTN_TK)
            return jnp.maximum(best, _key_norm2(k_ref[0, 0, pl.ds(off, ATTN_TK), :]))

        k2_sc[0] = lax.fori_loop(0, seq_len // ATTN_TK, kbody, _key_norm2(kc))

    lane_head = lax.broadcasted_iota(jnp.int32, (1, nq), 1) // sub
    sink = jnp.zeros((1, nq), F32)
    sink_max = sink_ref[h * GROUP]
    for g in range(GROUP):
        sink = jnp.where(lane_head == g, sink_ref[h * GROUP + g], sink)
        sink_max = jnp.maximum(sink_max, sink_ref[h * GROUP + g])
    sink = sink * LOG2_E
    rel = (lax.broadcasted_iota(jnp.int32, (band, sub), 0)
           - lax.broadcasted_iota(jnp.int32, (band, sub), 1))
    q_all = [jnp.concatenate([q_ref[0, c * sub:(c + 1) * sub, g * HEAD_DIM:(g + 1) * HEAD_DIM]
                              for g in range(GROUP)], axis=0) for c in range(n_sub)]
    q2 = functools.reduce(jnp.maximum, [_key_norm2(qc) for qc in q_all])
    safe = jnp.logical_and(
        q2 * k2_sc[0] * (ATTN_SCALE * ATTN_SCALE * BOUND_SLACK) <= SAFE_LOGIT * SAFE_LOGIT,
        sink_max <= SAFE_LOGIT)

    def scores(c):
        start = i * tq + c * sub
        kstart = pl.multiple_of(jnp.clip(start - WINDOW, 0, seq_len - band), WINDOW)
        bias = jnp.where(jnp.abs(rel + (kstart - start)) <= WINDOW, 0.0, NEG_INF)
        s = (_qk(k_ref[0, 0, pl.ds(kstart, band), :], q_all[c]) * EXP2_SCALE
             + jnp.concatenate([bias] * GROUP, axis=1))
        sc = _qk(kc, q_all[c]) * EXP2_SCALE
        return s, sc, kstart

    def finish(c, s, sc, kstart, exact_max):
        rows = slice(c * sub, (c + 1) * sub)
        if exact_max:
            m = jnp.maximum(jnp.maximum(jnp.max(s, axis=0, keepdims=True),
                                        jnp.max(sc, axis=0, keepdims=True)), sink)
            s, sc, sink_c = s - m, sc - m, sink - m
        else:
            sink_c = sink
        p = jnp.exp2(s)
        pc = jnp.exp2(sc)
        denom = (jnp.sum(p, axis=0, keepdims=True) + jnp.sum(pc, axis=0, keepdims=True)
                 + jnp.exp2(sink_c))
        vbt = _concat_slabs(vt_ref, kstart // SLAB, band // SLAB)
        o = (jnp.dot(vbt, p.astype(BF16), preferred_element_type=F32)
             + jnp.dot(vct, pc.astype(BF16), preferred_element_type=F32)) / denom
        o = o.T
        o = jnp.concatenate([o[g * sub:(g + 1) * sub] for g in range(GROUP)], axis=1)
        y_ref[0, rows, :] = (o * gz_ref[0, rows, :]).astype(y_ref.dtype)

    def all_sub_tiles(exact_max):
        nxt = scores(0)
        for c in range(n_sub):
            cur = nxt
            if c + 1 < n_sub:
                nxt = scores(c + 1)
            finish(c, *cur, exact_max)

    @pl.when(safe)
    def _():
        all_sub_tiles(exact_max=False)

    @pl.when(jnp.logical_not(safe))
    def _():
        all_sub_tiles(exact_max=True)


def _window_attention(q, k, vt, kc, vct, gz, sink):
    B, L, _ = q.shape
    C = kc.shape[2]
    sub = min(WINDOW_SUB, L - 2 * WINDOW)
    tq = min(WINDOW_TQ, L)
    assert L % tq == 0 and tq % sub == 0 and L >= sub + 2 * WINDOW and L % ATTN_TK == 0
    assert WINDOW % SLAB == 0 and sub % SLAB == 0 and C % SLAB == 0
    gw = GROUP * HEAD_DIM
    q_col = N_HEADS_A // GROUP
    kern = functools.partial(_window_kernel, tq=tq, sub=sub, seq_len=L)
    qz_spec = pl.BlockSpec((1, tq, gw), lambda b, h, i: (b, i, q_col + h))
    return pl.pallas_call(
        kern,
        grid=(B, N_KV_B, L // tq),
        in_specs=[pl.BlockSpec(memory_space=pltpu.SMEM), qz_spec] + _kv_specs(L, C, N_KV_A) + [qz_spec],
        out_specs=pl.BlockSpec((1, tq, gw), lambda b, h, i: (b, i, h)),
        out_shape=jax.ShapeDtypeStruct((B, L, N_HEADS_B * HEAD_DIM), BF16),
        scratch_shapes=[pltpu.SMEM((1,), F32)],
        compiler_params=_params(3),
        name="window_attention",
    )(sink, q, k, vt, kc, vct, gz)


def _finish_rows(r, x_rows, gated_gain):
    return x_rows + _rms(r, gated_gain)


def _out_kernel(ya_ref, yb_ref, w_ref, x_ref, mod_ref, g_ref, o_ref, *, n_sub, sub):
    gated_gain = mod_ref[0][:, 2 * D_MODEL:] * g_ref[...]
    for s in range(n_sub):
        rows = slice(s * sub, (s + 1) * sub)
        y = jnp.concatenate([ya_ref[0, rows, :], yb_ref[0, rows, :]], axis=1)
        r = jnp.dot(y, w_ref[...], preferred_element_type=F32)
        o_ref[0, rows, :] = _finish_rows(r, x_ref[0, rows, :], gated_gain)


def _out_projection(ya, yb, w, x, mod3, post_g):
    B, L, _ = x.shape
    tm = min(ROW_TILE, L)
    sub = min(SUB_ROWS, tm)
    wa = ya.shape[2]
    kern = functools.partial(_out_kernel, n_sub=tm // sub, sub=sub)
    return pl.pallas_call(
        kern,
        grid=(B, L // tm),
        in_specs=[
            pl.BlockSpec((1, tm, wa), lambda b, i: (b, i, 0)),
            pl.BlockSpec((1, tm, yb.shape[2]), lambda b, i: (b, i, 0)),
            _resident(w.shape, lambda b, i: (0, 0)),
            pl.BlockSpec((1, tm, D_MODEL), lambda b, i: (b, i, 0)),
            pl.BlockSpec((1, 1, 3 * D_MODEL), lambda b, i: (b, 0, 0)),
            _resident((1, D_MODEL), lambda b, i: (0, 0)),
        ],
        out_specs=pl.BlockSpec((1, tm, D_MODEL), lambda b, i: (b, i, 0)),
        out_shape=jax.ShapeDtypeStruct((B, L, D_MODEL), F32),
        compiler_params=_params(2),
        name="out_projection",
    )(ya, yb, w, x, mod3, post_g.reshape(1, D_MODEL))


def _pool_kernel(u_ref, up_ref, un_ref, gt_ref, pw_ref, ps_ref, w_ref, x_ref, mod_ref, g_ref,
                 o_ref, *, tm, n_sub, sub, seq_len):
    i = pl.program_id(1)
    gated_gain = mod_ref[0][:, 2 * D_MODEL:] * g_ref[...]

    for s in range(n_sub):
        r0 = s * sub
        rows = slice(r0, r0 + sub)
        t = i * tm + r0 + lax.broadcasted_iota(jnp.int32, (sub, 1), 0)
        mixed = []
        for g, w in enumerate(POOL_SIZES):
            cols = slice(g * POOL_GROUP, (g + 1) * POOL_GROUP)
            half = w // 2
            if s == 0:
                before = jnp.where(i > 0, up_ref[0, :, cols], 0.0)
            else:
                before = u_ref[0, r0 - POOL_HALO:r0, cols]
            if s == n_sub - 1:
                after = jnp.where(i < pl.num_programs(1) - 1, un_ref[0, :, cols], 0.0)
            else:
                after = u_ref[0, r0 + sub:r0 + sub + POOL_HALO, cols]
            u = u_ref[0, rows, cols]
            ext = jnp.concatenate([before, u, after], axis=0)
            tot = ext[1:] + ext[:-1]
            lo = 1
            span = 1
            while 2 * span < w:
                tot = tot[:-2 * span] + tot[2 * span:]
                lo += span
                span *= 2
            tot = tot[POOL_HALO - lo:POOL_HALO - lo + sub]
            cnt = jnp.minimum(t + half, seq_len) - jnp.maximum(t - half, 0)
            pooled = tot / cnt.astype(F32) - u
            mixed.append(jnp.dot(pooled.astype(BF16), pw_ref[g], preferred_element_type=F32))
        y = jnp.concatenate(mixed, axis=1) * ps_ref[...]
        y = (y * _silu(gt_ref[0, rows, :])).astype(BF16)
        r = jnp.dot(y, w_ref[...], preferred_element_type=F32)
        o_ref[0, rows, :] = _finish_rows(r, x_ref[0, rows, :], gated_gain)


def _pool_tail(u, gt, pool_w, pool_scale, w_out, x, mod3, post_g):
    B, L, _ = x.shape
    tm = min(ROW_TILE, L)
    sub = min(SUB_ROWS, tm)
    hb = tm // POOL_HALO
    n_hb = L // POOL_HALO
    kern = functools.partial(_pool_kernel, tm=tm, n_sub=tm // sub, sub=sub, seq_len=L)
    row_spec = pl.BlockSpec((1, tm, D_MODEL), lambda b, i: (b, i, 0))
    return pl.pallas_call(
        kern,
        grid=(B, L // tm),
        in_specs=[
            row_spec,
            pl.BlockSpec((1, POOL_HALO, D_MODEL), lambda b, i: (b, jnp.maximum(i * hb - 1, 0), 0)),
            pl.BlockSpec((1, POOL_HALO, D_MODEL),
                         lambda b, i: (b, jnp.minimum((i + 1) * hb, n_hb - 1), 0)),
            row_spec,
            _resident(pool_w.shape, lambda b, i: (0, 0, 0)),
            _resident((1, D_MODEL), lambda b, i: (0, 0)),
            _resident(w_out.shape, lambda b, i: (0, 0)),
            row_spec,
            pl.BlockSpec((1, 1, 3 * D_MODEL), lambda b, i: (b, 0, 0)),
            _resident((1, D_MODEL), lambda b, i: (0, 0)),
        ],
        out_specs=row_spec,
        out_shape=jax.ShapeDtypeStruct((B, L, D_MODEL), F32),
        compiler_params=_params(2),
        name="pool_tail",
    )(u, u, u, gt, pool_w, pool_scale.reshape(1, D_MODEL), w_out, x, mod3,
      post_g.reshape(1, D_MODEL))


def _rope_tables(seq_len):
    rows = seq_len // GRID_W
    inv = ROPE_THETA ** (-jnp.arange(AXIS_FREQS, dtype=F32) / AXIS_FREQS)
    row_ang = jnp.arange(rows, dtype=F32)[:, None] * inv
    col_ang = jnp.arange(GRID_W, dtype=F32)[:, None] * inv

    def table(fn, sign):
        shape = (rows, GRID_W, AXIS_FREQS)
        half = jnp.concatenate([jnp.broadcast_to(fn(row_ang)[:, None, :], shape),
                                jnp.broadcast_to(fn(col_ang)[None, :, :], shape)], axis=-1)
        return jnp.concatenate([sign * half, half], axis=-1).reshape(seq_len, HEAD_DIM)

    return table(jnp.cos, 1.0), table(jnp.sin, -1.0)


def kernel(x, c, ctx, c_ctx, ev_mod_w, ev_mod_b, ev_pre_g, ev_post_g, ev_w_in, ev_q_norm, ev_k_norm,
           ev_sink, ev_w_out, od_mod_w, od_mod_b, od_pre_g, od_post_g, od_w_in, od_pool_w,
           od_pool_scale, od_w_out):
    B, L, D = x.shape
    assert D == D_MODEL and L % GRID_W == 0
    assert ev_mod_w.shape[0] == 1 and od_mod_w.shape[0] == 1
    assert B + 1 <= MOD_ROWS
    ctx_row = B

    cvec = jnp.zeros((MOD_ROWS, D), F32).at[:B].set(c).at[ctx_row].set(c_ctx)
    mod_e = _modulation(cvec, ev_mod_w[0], ev_mod_b[0]).reshape(MOD_ROWS, 1, 3 * D)
    mod_o = _modulation(cvec, od_mod_w[0], od_mod_b[0]).reshape(MOD_ROWS, 1, 3 * D)

    w_in_e = ev_w_in[0].astype(BF16)
    norms = (ev_q_norm[0], ev_k_norm[0])
    n_kv = N_KV_A + N_KV_B

    def kv_plan(k_out, v_out, ka_op, kb_op):
        return ([(ka_op, k_out, TO_HEAD, h) for h in range(N_KV_A)]
                + [(OP_RAW, v_out, TO_SLABS_T, h) for h in range(N_KV_A)]
                + [(kb_op, k_out, TO_HEAD, N_KV_A + h) for h in range(N_KV_B)]
                + [(OP_RAW, v_out, TO_SLABS_T, N_KV_A + h) for h in range(N_KV_B)])

    plan = tuple([(OP_QNORM_ROPE, 0, TO_COLS, h * HEAD_DIM) for h in range(N_HEADS_A)]
                 + [(OP_ROPE, 0, TO_COLS, (N_HEADS_A + h) * HEAD_DIM) for h in range(N_HEADS_B)]
                 + kv_plan(1, 2, OP_KNORM_ROPE, OP_ROPE)
                 + [(OP_SILU, 3, TO_COLS, h * HEAD_DIM) for h in range(Q_W // HEAD_DIM)])
    q, k, vt, gz = _projection(
        x, mod_e, None, ev_pre_g[0], w_in_e, plan=plan,
        outs=((TO_COLS, Q_W, BF16), (TO_HEAD, n_kv, BF16), (TO_SLABS_T, n_kv, BF16), (TO_COLS, Q_W, F32)),
        rope=_rope_tables(L), norms=norms, name="in_projection_even")
    kc, vct = _projection(
        ctx, mod_e, ctx_row, ev_pre_g[0], w_in_e[:, Q_W:Q_W + KV_W],
        plan=tuple(kv_plan(0, 1, OP_KNORM, OP_RAW)),
        outs=((TO_HEAD, n_kv, BF16), (TO_SLABS_T, n_kv, BF16)), norms=norms, name="in_projection_ctx")

    ya = _dense_attention(q, k, vt, kc, vct, gz)
    yb = _window_attention(q, k, vt, kc, vct, gz, ev_sink[0])
    x = _out_projection(ya, yb, ev_w_out[0].astype(BF16), x, mod_e, ev_post_g[0])

    u, gt = _projection(
        x, mod_o, None, od_pre_g[0], od_w_in[0].astype(BF16),
        plan=tuple((OP_RAW, col // D, TO_COLS, col % D) for col in range(0, 2 * D, HEAD_DIM)),
        outs=((TO_COLS, D, F32), (TO_COLS, D, F32)), name="in_projection_odd")
    return _pool_tail(u, gt, od_pool_w[0].astype(BF16), od_pool_scale[0], od_w_out[0].astype(BF16),
                      x, mod_o, od_post_g[0])
```

```python
import functools
import math

import jax
import jax.numpy as jnp
from jax import lax
from jax.experimental import pallas as pl
from jax.experimental.pallas import tpu as pltpu

D_MODEL = 2048
GRID_W = 64
HEAD_DIM = 128
AXIS_FREQS = HEAD_DIM // 4
ROPE_THETA = 10000.0
N_HEADS_A = 8
N_KV_A = 2
N_HEADS_B = 8
N_KV_B = 2
GROUP = N_HEADS_A // N_KV_A
WINDOW = 128
ATTN_SCALE = HEAD_DIM ** -0.5
NEG_INF = -1e30
Q_W = (N_HEADS_A + N_HEADS_B) * HEAD_DIM
KV_A_W = N_KV_A * HEAD_DIM
KV_B_W = N_KV_B * HEAD_DIM
KV_W = 2 * KV_A_W + 2 * KV_B_W
POOL_SIZES = (2, 4, 8, 16)
POOL_GROUP = D_MODEL // len(POOL_SIZES)
POOL_HALO = 8
EPS = 1e-6

LANES = 128
SUBLANES = 8
VMEM_LIMIT_BYTES = 56 * 1024 * 1024

MOD_ROWS = SUBLANES
MOD_TN = 512
ROW_TILE = 512
SUB_ROWS = 256
COL_CHUNK = 512
ATTN_TQ = 1024
ATTN_TK = 1024
DENSE_UNROLL = 8
WINDOW_TQ = 1024
WINDOW_SUB = 128
LOG2_E = math.log2(math.e)
EXP2_SCALE = ATTN_SCALE * LOG2_E
BOUND_SLACK = 1.0 + 2.0 ** -7
SAFE_LOGIT = 50.0

OP_RAW, OP_ROPE, OP_QNORM_ROPE, OP_KNORM_ROPE, OP_KNORM, OP_SILU = range(6)
TO_COLS, TO_HEAD, TO_SLABS_T = range(3)
SLAB = LANES

F32 = jnp.float32
BF16 = jnp.bfloat16


def _params(n_axes):
    return pltpu.CompilerParams(
        dimension_semantics=("arbitrary",) * n_axes,
        vmem_limit_bytes=VMEM_LIMIT_BYTES,
    )


def _resident(block_shape, index_map):
    return pl.BlockSpec(block_shape, index_map, pipeline_mode=pl.Buffered(1))


def _silu(v):
    return v * jax.nn.sigmoid(v)


def _rms(v, gain):
    ms = jnp.mean(v * v, axis=1, keepdims=True)
    return (v * lax.rsqrt(ms + EPS)) * gain


def _mod_kernel(c_ref, w_ref, b_ref, o_ref):
    a = _silu(c_ref[...]).astype(BF16)
    o_ref[...] = jnp.dot(a, w_ref[...].astype(BF16), preferred_element_type=F32) + b_ref[...]


def _modulation(cvec, w, b):
    n = w.shape[1]
    return pl.pallas_call(
        _mod_kernel,
        grid=(n // MOD_TN,),
        in_specs=[
            pl.BlockSpec((MOD_ROWS, D_MODEL), lambda j: (0, 0)),
            pl.BlockSpec((D_MODEL, MOD_TN), lambda j: (0, j)),
            pl.BlockSpec((1, MOD_TN), lambda j: (0, j)),
        ],
        out_specs=pl.BlockSpec((MOD_ROWS, MOD_TN), lambda j: (0, j)),
        out_shape=jax.ShapeDtypeStruct((MOD_ROWS, n), F32),
        compiler_params=_params(1),
        name="modulation",
    )(cvec, w, b.reshape(1, n))


def _proj_kernel(*refs, plan, n_sub, sub, use_rope, use_norm):
    x_ref, mod_ref, g_ref, w_ref = refs[:4]
    pos = 4
    if use_rope:
        cos_ref, sin_ref = refs[pos:pos + 2]
        pos += 2
    if use_norm:
        qn_ref, kn_ref = refs[pos:pos + 2]
        pos += 2
    out_refs = refs[pos:]

    mrow = mod_ref[0]
    shift = mrow[:, :D_MODEL]
    gain = g_ref[...] * (1.0 + mrow[:, D_MODEL:2 * D_MODEL])
    heads_per_chunk = COL_CHUNK // HEAD_DIM
    slabs_per_sub = sub // SLAB

    for s in range(n_sub):
        rows = slice(s * sub, (s + 1) * sub)
        xs = x_ref[0, rows, :]
        ms = jnp.mean(xs * xs, axis=1, keepdims=True)
        h = ((xs * lax.rsqrt(ms + EPS)) * gain + shift).astype(BF16)
        if use_rope:
            cs = cos_ref[rows, :]
            sn = sin_ref[rows, :]
        for c in range(len(plan) // heads_per_chunk):
            acc = jnp.dot(h, w_ref[:, c * COL_CHUNK:(c + 1) * COL_CHUNK],
                          preferred_element_type=F32)
            for hh in range(heads_per_chunk):
                op, o_idx, kind, slot = plan[c * heads_per_chunk + hh]
                t = acc[:, hh * HEAD_DIM:(hh + 1) * HEAD_DIM]
                if op == OP_SILU:
                    t = _silu(t)
                elif op == OP_QNORM_ROPE:
                    t = _rms(t, qn_ref[...])
                elif op in (OP_KNORM_ROPE, OP_KNORM):
                    t = _rms(t, kn_ref[...])
                if op in (OP_ROPE, OP_QNORM_ROPE, OP_KNORM_ROPE):
                    t = t * cs + pltpu.roll(t, HEAD_DIM // 2, 1) * sn
                out = out_refs[o_idx]
                if kind == TO_COLS:
                    out[0, rows, slot:slot + HEAD_DIM] = t.astype(out.dtype)
                elif kind == TO_HEAD:
                    out[0, slot, rows, :] = t.astype(out.dtype)
                else:
                    for r in range(slabs_per_sub):
                        out[0, slot, s * slabs_per_sub + r] = (
                            t[r * SLAB:(r + 1) * SLAB, :].T.astype(out.dtype))


def _projection(x, mod3, mod_row, g, w, *, plan, outs, rope=None, norms=None, name):
    B, L, _ = x.shape
    n_total = w.shape[1]
    assert n_total == len(plan) * HEAD_DIM and n_total % COL_CHUNK == 0
    tm = min(ROW_TILE, L)
    sub = min(SUB_ROWS, tm)
    assert L % tm == 0 and tm % sub == 0 and sub % SLAB == 0
    mod_map = (lambda b, i: (b, 0, 0)) if mod_row is None else (lambda b, i: (mod_row, 0, 0))
    in_specs = [
        pl.BlockSpec((1, tm, D_MODEL), lambda b, i: (b, i, 0)),
        pl.BlockSpec((1, 1, 3 * D_MODEL), mod_map),
        _resident((1, D_MODEL), lambda b, i: (0, 0)),
        _resident((D_MODEL, n_total), lambda b, i: (0, 0)),
    ]
    args = [x, mod3, g.reshape(1, D_MODEL), w]
    if rope is not None:
        in_specs += [pl.BlockSpec((tm, HEAD_DIM), lambda b, i: (i, 0))] * 2
        args += list(rope)
    if norms is not None:
        in_specs += [_resident((1, HEAD_DIM), lambda b, i: (0, 0))] * 2
        args += [n.reshape(1, HEAD_DIM) for n in norms]
    out_specs, out_shape = [], []
    for kind, n, dt in outs:
        if kind == TO_COLS:
            out_specs.append(pl.BlockSpec((1, tm, n), lambda b, i: (b, i, 0)))
            out_shape.append(jax.ShapeDtypeStruct((B, L, n), dt))
        elif kind == TO_HEAD:
            out_specs.append(pl.BlockSpec((1, n, tm, HEAD_DIM), lambda b, i: (b, 0, i, 0)))
            out_shape.append(jax.ShapeDtypeStruct((B, n, L, HEAD_DIM), dt))
        else:
            out_specs.append(pl.BlockSpec((1, n, tm // SLAB, HEAD_DIM, SLAB),
                                          lambda b, i: (b, 0, i, 0, 0)))
            out_shape.append(jax.ShapeDtypeStruct((B, n, L // SLAB, HEAD_DIM, SLAB), dt))
    kern = functools.partial(
        _proj_kernel, plan=plan, n_sub=tm // sub, sub=sub,
        use_rope=rope is not None, use_norm=norms is not None)
    return pl.pallas_call(
        kern,
        grid=(B, L // tm),
        in_specs=in_specs,
        out_specs=out_specs,
        out_shape=out_shape,
        compiler_params=_params(2),
        name=name,
    )(*args)


def _qk(q, k):
    return lax.dot_general(q, k, (((1,), (1,)), ((), ())), preferred_element_type=F32)


def _key_norm2(k):
    kf = k.astype(F32)
    return jnp.max(jnp.sum(kf * kf, axis=1, keepdims=True))


def _concat_slabs(vt_ref, first, count):
    return jnp.concatenate([vt_ref[0, 0, first + t] for t in range(count)], axis=1)


def _dense_kernel(q_ref, k_ref, vt_ref, kc_ref, vct_ref, gz_ref, y_ref,
                  k2_sc, m_sc, l_sc, acc_sc, *, tq, tk, n_steps):
    nq = GROUP * tq
    slabs = tk // SLAB

    def key_tile(j):
        return k_ref[0, 0, pl.ds(pl.multiple_of(j * tk, tk), tk), :]

    def vt_tile(j):
        return _concat_slabs(vt_ref, j * slabs, slabs)

    kc = kc_ref[0, 0]
    vct = _concat_slabs(vct_ref, 0, vct_ref.shape[2])

    @pl.when(pl.program_id(2) == 0)
    def _():
        k2_sc[0] = lax.fori_loop(
            0, n_steps, lambda j, best: jnp.maximum(best, _key_norm2(key_tile(j))), _key_norm2(kc))

    l_sc[...] = jnp.zeros(l_sc.shape, F32)
    acc_sc[...] = jnp.zeros(acc_sc.shape, F32)
    q = [q_ref[0, :, g * HEAD_DIM:(g + 1) * HEAD_DIM] for g in range(GROUP)]

    q2 = functools.reduce(jnp.maximum, [_key_norm2(qg) for qg in q])
    safe = q2 * k2_sc[0] * (ATTN_SCALE * ATTN_SCALE * BOUND_SLACK) <= SAFE_LOGIT * SAFE_LOGIT

    @pl.when(safe)
    def _():
        q_all = jnp.concatenate(q, axis=0)

        def attend(k, vt):
            p = jnp.exp2(_qk(k, q_all) * EXP2_SCALE)
            l_sc[...] += jnp.sum(p.reshape(-1, SUBLANES, nq), axis=0)
            acc_sc[...] += jnp.dot(vt, p.astype(BF16), preferred_element_type=F32)

        n_loop = n_steps // DENSE_UNROLL

        def body(i, carry):
            for u in range(DENSE_UNROLL):
                j = i * DENSE_UNROLL + u
                attend(key_tile(j), vt_tile(j))
            return carry

        lax.fori_loop(0, n_loop, body, 0)
        for j in range(n_loop * DENSE_UNROLL, n_steps):
            attend(key_tile(j), vt_tile(j))
        attend(kc, vct)

    @pl.when(jnp.logical_not(safe))
    def _():
        m_sc[...] = jnp.full(m_sc.shape, -jnp.inf, F32)

        def attend(k, vt):
            for g in range(GROUP):
                st = _qk(k, q[g])
                m_prev = m_sc[g]
                m_new = jnp.maximum(m_prev, jnp.max(st, axis=0, keepdims=True))
                alpha = jnp.exp2((m_prev - m_new) * EXP2_SCALE)
                p = jnp.exp2((st - m_new) * EXP2_SCALE)
                cols = slice(g * tq, (g + 1) * tq)
                l_sc[0:1, cols] = alpha * l_sc[0:1, cols] + jnp.sum(p, axis=0, keepdims=True)
                acc_sc[:, cols] = alpha * acc_sc[:, cols] + jnp.dot(vt, p.astype(BF16),
                                                                     preferred_element_type=F32)
                m_sc[g] = m_new

        def body(j, carry):
            attend(key_tile(j), vt_tile(j))
            return carry

        lax.fori_loop(0, n_steps, body, 0)
        attend(kc, vct)

    o = (acc_sc[...] / jnp.sum(l_sc[...], axis=0, keepdims=True)).T
    o = jnp.concatenate([o[g * tq:(g + 1) * tq] for g in range(GROUP)], axis=1)
    y_ref[0] = (o * gz_ref[0]).astype(y_ref.dtype)


def _kv_specs(L, C, head0):
    return [
        pl.BlockSpec((1, 1, L, HEAD_DIM), lambda b, h, i: (b, head0 + h, 0, 0)),
        pl.BlockSpec((1, 1, L // SLAB, HEAD_DIM, SLAB), lambda b, h, i: (b, head0 + h, 0, 0, 0)),
        pl.BlockSpec((1, 1, C, HEAD_DIM), lambda b, h, i: (b, head0 + h, 0, 0)),
        pl.BlockSpec((1, 1, C // SLAB, HEAD_DIM, SLAB), lambda b, h, i: (b, head0 + h, 0, 0, 0)),
    ]


def _dense_attention(q, k, vt, kc, vct, gz):
    B, L, _ = q.shape
    C = kc.shape[2]
    tq = min(ATTN_TQ, L)
    tk = min(ATTN_TK, L)
    assert L % tq == 0 and L % tk == 0 and tk % SLAB == 0 and C % SLAB == 0
    gw = GROUP * HEAD_DIM
    kern = functools.partial(_dense_kernel, tq=tq, tk=tk, n_steps=L // tk)
    qz_spec = pl.BlockSpec((1, tq, gw), lambda b, h, i: (b, i, h))
    return pl.pallas_call(
        kern,
        grid=(B, N_KV_A, L // tq),
        in_specs=[qz_spec] + _kv_specs(L, C, 0) + [qz_spec],
        out_specs=qz_spec,
        out_shape=jax.ShapeDtypeStruct((B, L, N_HEADS_A * HEAD_DIM), BF16),
        scratch_shapes=[
            pltpu.SMEM((1,), F32),
            pltpu.VMEM((GROUP, 1, tq), F32),
            pltpu.VMEM((SUBLANES, GROUP * tq), F32),
            pltpu.VMEM((HEAD_DIM, GROUP * tq), F32),
        ],
        compiler_params=_params(3),
        name="dense_attention",
    )(q, k, vt, kc, vct, gz)


def _window_kernel(sink_ref, q_ref, k_ref, vt_ref, kc_ref, vct_ref, gz_ref, y_ref, k2_sc,
                   *, tq, sub, seq_len):
    h = pl.program_id(1)
    i = pl.program_id(2)
    band = sub + 2 * WINDOW
    nq = GROUP * sub
    n_sub = tq // sub

    kc = kc_ref[0, 0]
    vct = _concat_slabs(vct_ref, 0, vct_ref.shape[2])

    @pl.when(i == 0)
    def _():
        def kbody(j, best):
            off = pl.multiple_of(j * ATTN_TK, ATTN_TK)
            return jnp.maximum(best, _key_norm2(k_ref[0, 0, pl.ds(off, ATTN_TK), :]))

        k2_sc[0] = lax.fori_loop(0, seq_len // ATTN_TK, kbody, _key_norm2(kc))

    lane_head = lax.broadcasted_iota(jnp.int32, (1, nq), 1) // sub
    sink = jnp.zeros((1, nq), F32)
    sink_max = sink_ref[h * GROUP]
    for g in range(GROUP):
        sink = jnp.where(lane_head == g, sink_ref[h * GROUP + g], sink)
        sink_max = jnp.maximum(sink_max, sink_ref[h * GROUP + g])
    sink = sink * LOG2_E
    rel = (lax.broadcasted_iota(jnp.int32, (band, sub), 0)
           - lax.broadcasted_iota(jnp.int32, (band, sub), 1))
    q_all = [jnp.concatenate([q_ref[0, c * sub:(c + 1) * sub, g * HEAD_DIM:(g + 1) * HEAD_DIM]
                              for g in range(GROUP)], axis=0) for c in range(n_sub)]
    q2 = functools.reduce(jnp.maximum, [_key_norm2(qc) for qc in q_all])
    safe = jnp.logical_and(
        q2 * k2_sc[0] * (ATTN_SCALE * ATTN_SCALE * BOUND_SLACK) <= SAFE_LOGIT * SAFE_LOGIT,
        sink_max <= SAFE_LOGIT)

    def scores(c):
        start = i * tq + c * sub
        kstart = pl.multiple_of(jnp.clip(start - WINDOW, 0, seq_len - band), WINDOW)
        bias = jnp.where(jnp.abs(rel + (kstart - start)) <= WINDOW, 0.0, NEG_INF)
        s = (_qk(k_ref[0, 0, pl.ds(kstart, band), :], q_all[c]) * EXP2_SCALE
             + jnp.concatenate([bias] * GROUP, axis=1))
        sc = _qk(kc, q_all[c]) * EXP2_SCALE
        return s, sc, kstart

    def finish(c, s, sc, kstart, exact_max):
        rows = slice(c * sub, (c + 1) * sub)
        if exact_max:
            m = jnp.maximum(jnp.maximum(jnp.max(s, axis=0, keepdims=True),
                                        jnp.max(sc, axis=0, keepdims=True)), sink)
            s, sc, sink_c = s - m, sc - m, sink - m
        else:
            sink_c = sink
        p = jnp.exp2(s)
        pc = jnp.exp2(sc)
        denom = (jnp.sum(p, axis=0, keepdims=True) + jnp.sum(pc, axis=0, keepdims=True)
                 + jnp.exp2(sink_c))
        vbt = _concat_slabs(vt_ref, kstart // SLAB, band // SLAB)
        o = (jnp.dot(vbt, p.astype(BF16), preferred_element_type=F32)
             + jnp.dot(vct, pc.astype(BF16), preferred_element_type=F32)) / denom
        o = o.T
        o = jnp.concatenate([o[g * sub:(g + 1) * sub] for g in range(GROUP)], axis=1)
        y_ref[0, rows, :] = (o * gz_ref[0, rows, :]).astype(y_ref.dtype)

    def all_sub_tiles(exact_max):
        nxt = scores(0)
        for c in range(n_sub):
            cur = nxt
            if c + 1 < n_sub:
                nxt = scores(c + 1)
            finish(c, *cur, exact_max)

    @pl.when(safe)
    def _():
        all_sub_tiles(exact_max=False)

    @pl.when(jnp.logical_not(safe))
    def _():
        all_sub_tiles(exact_max=True)


def _window_attention(q, k, vt, kc, vct, gz, sink):
    B, L, _ = q.shape
    C = kc.shape[2]
    sub = min(WINDOW_SUB, L - 2 * WINDOW)
    tq = min(WINDOW_TQ, L)
    assert L % tq == 0 and tq % sub == 0 and L >= sub + 2 * WINDOW and L % ATTN_TK == 0
    assert WINDOW % SLAB == 0 and sub % SLAB == 0 and C % SLAB == 0
    gw = GROUP * HEAD_DIM
    q_col = N_HEADS_A // GROUP
    kern = functools.partial(_window_kernel, tq=tq, sub=sub, seq_len=L)
    qz_spec = pl.BlockSpec((1, tq, gw), lambda b, h, i: (b, i, q_col + h))
    return pl.pallas_call(
        kern,
        grid=(B, N_KV_B, L // tq),
        in_specs=[pl.BlockSpec(memory_space=pltpu.SMEM), qz_spec] + _kv_specs(L, C, N_KV_A) + [qz_spec],
        out_specs=pl.BlockSpec((1, tq, gw), lambda b, h, i: (b, i, h)),
        out_shape=jax.ShapeDtypeStruct((B, L, N_HEADS_B * HEAD_DIM), BF16),
        scratch_shapes=[pltpu.SMEM((1,), F32)],
        compiler_params=_params(3),
        name="window_attention",
    )(sink, q, k, vt, kc, vct, gz)


def _finish_rows(r, x_rows, gated_gain):
    return x_rows + _rms(r, gated_gain)


def _out_kernel(ya_ref, yb_ref, w_ref, x_ref, mod_ref, g_ref, o_ref, *, n_sub, sub):
    gated_gain = mod_ref[0][:, 2 * D_MODEL:] * g_ref[...]
    for s in range(n_sub):
        rows = slice(s * sub, (s + 1) * sub)
        y = jnp.concatenate([ya_ref[0, rows, :], yb_ref[0, rows, :]], axis=1)
        r = jnp.dot(y, w_ref[...], preferred_element_type=F32)
        o_ref[0, rows, :] = _finish_rows(r, x_ref[0, rows, :], gated_gain)


def _out_projection(ya, yb, w, x, mod3, post_g):
    B, L, _ = x.shape
    tm = min(ROW_TILE, L)
    sub = min(SUB_ROWS, tm)
    wa = ya.shape[2]
    kern = functools.partial(_out_kernel, n_sub=tm // sub, sub=sub)
    return pl.pallas_call(
        kern,
        grid=(B, L // tm),
        in_specs=[
            pl.BlockSpec((1, tm, wa), lambda b, i: (b, i, 0)),
            pl.BlockSpec((1, tm, yb.shape[2]), lambda b, i: (b, i, 0)),
            _resident(w.shape, lambda b, i: (0, 0)),
            pl.BlockSpec((1, tm, D_MODEL), lambda b, i: (b, i, 0)),
            pl.BlockSpec((1, 1, 3 * D_MODEL), lambda b, i: (b, 0, 0)),
            _resident((1, D_MODEL), lambda b, i: (0, 0)),
        ],
        out_specs=pl.BlockSpec((1, tm, D_MODEL), lambda b, i: (b, i, 0)),
        out_shape=jax.ShapeDtypeStruct((B, L, D_MODEL), F32),
        compiler_params=_params(2),
        name="out_projection",
    )(ya, yb, w, x, mod3, post_g.reshape(1, D_MODEL))


def _pool_kernel(u_ref, up_ref, un_ref, gt_ref, pw_ref, ps_ref, w_ref, x_ref, mod_ref, g_ref,
                 o_ref, *, tm, n_sub, sub, seq_len):
    i = pl.program_id(1)
    gated_gain = mod_ref[0][:, 2 * D_MODEL:] * g_ref[...]

    for s in range(n_sub):
        r0 = s * sub
        rows = slice(r0, r0 + sub)
        t = i * tm + r0 + lax.broadcasted_iota(jnp.int32, (sub, 1), 0)
        mixed = []
        for g, w in enumerate(POOL_SIZES):
            cols = slice(g * POOL_GROUP, (g + 1) * POOL_GROUP)
            half = w // 2
            if s == 0:
                before = jnp.where(i > 0, up_ref[0, :, cols], 0.0)
            else:
                before = u_ref[0, r0 - POOL_HALO:r0, cols]
            if s == n_sub - 1:
                after = jnp.where(i < pl.num_programs(1) - 1, un_ref[0, :, cols], 0.0)
            else:
                after = u_ref[0, r0 + sub:r0 + sub + POOL_HALO, cols]
            u = u_ref[0, rows, cols]
            ext = jnp.concatenate([before, u, after], axis=0)
            tot = ext[1:] + ext[:-1]
            lo = 1
            span = 1
            while 2 * span < w:
                tot = tot[:-2 * span] + tot[2 * span:]
                lo += span
                span *= 2
            tot = tot[POOL_HALO - lo:POOL_HALO - lo + sub]
            cnt = jnp.minimum(t + half, seq_len) - jnp.maximum(t - half, 0)
            pooled = tot / cnt.astype(F32) - u
            mixed.append(jnp.dot(pooled.astype(BF16), pw_ref[g], preferred_element_type=F32))
        y = jnp.concatenate(mixed, axis=1) * ps_ref[...]
        y = (y * _silu(gt_ref[0, rows, :])).astype(BF16)
        r = jnp.dot(y, w_ref[...], preferred_element_type=F32)
        o_ref[0, rows, :] = _finish_rows(r, x_ref[0, rows, :], gated_gain)


def _pool_tail(u, gt, pool_w, pool_scale, w_out, x, mod3, post_g):
    B, L, _ = x.shape
    tm = min(ROW_TILE, L)
    sub = min(SUB_ROWS, tm)
    hb = tm // POOL_HALO
    n_hb = L // POOL_HALO
    kern = functools.partial(_pool_kernel, tm=tm, n_sub=tm // sub, sub=sub, seq_len=L)
    row_spec = pl.BlockSpec((1, tm, D_MODEL), lambda b, i: (b, i, 0))
    return pl.pallas_call(
        kern,
        grid=(B, L // tm),
        in_specs=[
            row_spec,
            pl.BlockSpec((1, POOL_HALO, D_MODEL), lambda b, i: (b, jnp.maximum(i * hb - 1, 0), 0)),
            pl.BlockSpec((1, POOL_HALO, D_MODEL),
                         lambda b, i: (b, jnp.minimum((i + 1) * hb, n_hb - 1), 0)),
            row_spec,
            _resident(pool_w.shape, lambda b, i: (0, 0, 0)),
            _resident((1, D_MODEL), lambda b, i: (0, 0)),
            _resident(w_out.shape, lambda b, i: (0, 0)),
            row_spec,
            pl.BlockSpec((1, 1, 3 * D_MODEL), lambda b, i: (b, 0, 0)),
            _resident((1, D_MODEL), lambda b, i: (0, 0)),
        ],
        out_specs=row_spec,
        out_shape=jax.ShapeDtypeStruct((B, L, D_MODEL), F32),
        compiler_params=_params(2),
        name="pool_tail",
    )(u, u, u, gt, pool_w, pool_scale.reshape(1, D_MODEL), w_out, x, mod3,
      post_g.reshape(1, D_MODEL))


def _rope_tables(seq_len):
    rows = seq_len // GRID_W
    inv = ROPE_THETA ** (-jnp.arange(AXIS_FREQS, dtype=F32) / AXIS_FREQS)
    row_ang = jnp.arange(rows, dtype=F32)[:, None] * inv
    col_ang = jnp.arange(GRID_W, dtype=F32)[:, None] * inv

    def table(fn, sign):
        shape = (rows, GRID_W, AXIS_FREQS)
        half = jnp.concatenate([jnp.broadcast_to(fn(row_ang)[:, None, :], shape),
                                jnp.broadcast_to(fn(col_ang)[None, :, :], shape)], axis=-1)
        return jnp.concatenate([sign * half, half], axis=-1).reshape(seq_len, HEAD_DIM)

    return table(jnp.cos, 1.0), table(jnp.sin, -1.0)


def kernel(x, c, ctx, c_ctx, ev_mod_w, ev_mod_b, ev_pre_g, ev_post_g, ev_w_in, ev_q_norm, ev_k_norm,
           ev_sink, ev_w_out, od_mod_w, od_mod_b, od_pre_g, od_post_g, od_w_in, od_pool_w,
           od_pool_scale, od_w_out):
    B, L, D = x.shape
    assert D == D_MODEL and L % GRID_W == 0
    assert ev_mod_w.shape[0] == 1 and od_mod_w.shape[0] == 1
    assert B + 1 <= MOD_ROWS
    ctx_row = B

    cvec = jnp.zeros((MOD_ROWS, D), F32).at[:B].set(c).at[ctx_row].set(c_ctx)
    mod_e = _modulation(cvec, ev_mod_w[0], ev_mod_b[0]).reshape(MOD_ROWS, 1, 3 * D)
    mod_o = _modulation(cvec, od_mod_w[0], od_mod_b[0]).reshape(MOD_ROWS, 1, 3 * D)

    w_in_e = ev_w_in[0].astype(BF16)
    norms = (ev_q_norm[0], ev_k_norm[0])
    n_kv = N_KV_A + N_KV_B

    def kv_plan(k_out, v_out, ka_op, kb_op):
        return ([(ka_op, k_out, TO_HEAD, h) for h in range(N_KV_A)]
                + [(OP_RAW, v_out, TO_SLABS_T, h) for h in range(N_KV_A)]
                + [(kb_op, k_out, TO_HEAD, N_KV_A + h) for h in range(N_KV_B)]
                + [(OP_RAW, v_out, TO_SLABS_T, N_KV_A + h) for h in range(N_KV_B)])

    plan = tuple([(OP_QNORM_ROPE, 0, TO_COLS, h * HEAD_DIM) for h in range(N_HEADS_A)]
                 + [(OP_ROPE, 0, TO_COLS, (N_HEADS_A + h) * HEAD_DIM) for h in range(N_HEADS_B)]
                 + kv_plan(1, 2, OP_KNORM_ROPE, OP_ROPE)
                 + [(OP_SILU, 3, TO_COLS, h * HEAD_DIM) for h in range(Q_W // HEAD_DIM)])
    q, k, vt, gz = _projection(
        x, mod_e, None, ev_pre_g[0], w_in_e, plan=plan,
        outs=((TO_COLS, Q_W, BF16), (TO_HEAD, n_kv, BF16), (TO_SLABS_T, n_kv, BF16), (TO_COLS, Q_W, F32)),
        rope=_rope_tables(L), norms=norms, name="in_projection_even")
    kc, vct = _projection(
        ctx, mod_e, ctx_row, ev_pre_g[0], w_in_e[:, Q_W:Q_W + KV_W],
        plan=tuple(kv_plan(0, 1, OP_KNORM, OP_RAW)),
        outs=((TO_HEAD, n_kv, BF16), (TO_SLABS_T, n_kv, BF16)), norms=norms, name="in_projection_ctx")

    ya = _dense_attention(q, k, vt, kc, vct, gz)
    yb = _window_attention(q, k, vt, kc, vct, gz, ev_sink[0])
    x = _out_projection(ya, yb, ev_w_out[0].astype(BF16), x, mod_e, ev_post_g[0])

    u, gt = _projection(
        x, mod_o, None, od_pre_g[0], od_w_in[0].astype(BF16),
        plan=tuple((OP_RAW, col // D, TO_COLS, col % D) for col in range(0, 2 * D, HEAD_DIM)),
        outs=((TO_COLS, D, F32), (TO_COLS, D, F32)), name="in_projection_odd")
    return _pool_tail(u, gt, od_pool_w[0].astype(BF16), od_pool_scale[0], od_w_out[0].astype(BF16),
                      x, mod_o, od_post_g[0])
```

```python
import functools
import math

import jax
import jax.numpy as jnp
from jax import lax
from jax.experimental import pallas as pl
from jax.experimental.pallas import tpu as pltpu

D_MODEL = 2048
GRID_W = 64
HEAD_DIM = 128
AXIS_FREQS = HEAD_DIM // 4
ROPE_THETA = 10000.0
N_HEADS_A = 8
N_KV_A = 2
N_HEADS_B = 8
N_KV_B = 2
GROUP = N_HEADS_A // N_KV_A
WINDOW = 128
ATTN_SCALE = HEAD_DIM ** -0.5
NEG_INF = -1e30
Q_W = (N_HEADS_A + N_HEADS_B) * HEAD_DIM
KV_A_W = N_KV_A * HEAD_DIM
KV_B_W = N_KV_B * HEAD_DIM
KV_W = 2 * KV_A_W + 2 * KV_B_W
POOL_SIZES = (2, 4, 8, 16)
POOL_GROUP = D_MODEL // len(POOL_SIZES)
POOL_HALO = 8
EPS = 1e-6

LANES = 128
SUBLANES = 8
VMEM_LIMIT_BYTES = 56 * 1024 * 1024

MOD_ROWS = SUBLANES
MOD_TN = 512
ROW_TILE = 512
SUB_ROWS = 256
COL_CHUNK = 512
ATTN_TQ = 1024
ATTN_TK = 1024
DENSE_UNROLL = 4
WINDOW_TQ = 1024
WINDOW_SUB = 128
LOG2_E = math.log2(math.e)
EXP2_SCALE = ATTN_SCALE * LOG2_E
BOUND_SLACK = 1.0 + 2.0 ** -7
SAFE_LOGIT = 50.0

OP_RAW, OP_ROPE, OP_QNORM_ROPE, OP_KNORM_ROPE, OP_KNORM, OP_SILU = range(6)
TO_COLS, TO_HEAD, TO_SLABS_T = range(3)
SLAB = LANES

F32 = jnp.float32
BF16 = jnp.bfloat16


def _params(n_axes):
    return pltpu.CompilerParams(
        dimension_semantics=("arbitrary",) * n_axes,
        vmem_limit_bytes=VMEM_LIMIT_BYTES,
    )


def _resident(block_shape, index_map):
    return pl.BlockSpec(block_shape, index_map, pipeline_mode=pl.Buffered(1))


def _silu(v):
    return v * jax.nn.sigmoid(v)


def _rms(v, gain):
    ms = jnp.mean(v * v, axis=1, keepdims=True)
    return (v * lax.rsqrt(ms + EPS)) * gain


def _mod_kernel(c_ref, w_ref, b_ref, o_ref):
    a = _silu(c_ref[...]).astype(BF16)
    o_ref[...] = jnp.dot(a, w_ref[...].astype(BF16), preferred_element_type=F32) + b_ref[...]


def _modulation(cvec, w, b):
    n = w.shape[1]
    return pl.pallas_call(
        _mod_kernel,
        grid=(n // MOD_TN,),
        in_specs=[
            pl.BlockSpec((MOD_ROWS, D_MODEL), lambda j: (0, 0)),
            pl.BlockSpec((D_MODEL, MOD_TN), lambda j: (0, j)),
            pl.BlockSpec((1, MOD_TN), lambda j: (0, j)),
        ],
        out_specs=pl.BlockSpec((MOD_ROWS, MOD_TN), lambda j: (0, j)),
        out_shape=jax.ShapeDtypeStruct((MOD_ROWS, n), F32),
        compiler_params=_params(1),
        name="modulation",
    )(cvec, w, b.reshape(1, n))


def _proj_kernel(*refs, plan, n_sub, sub, use_rope, use_norm):
    x_ref, mod_ref, g_ref, w_ref = refs[:4]
    pos = 4
    if use_rope:
        cos_ref, sin_ref = refs[pos:pos + 2]
        pos += 2
    if use_norm:
        qn_ref, kn_ref = refs[pos:pos + 2]
        pos += 2
    out_refs = refs[pos:]

    mrow = mod_ref[0]
    shift = mrow[:, :D_MODEL]
    gain = g_ref[...] * (1.0 + mrow[:, D_MODEL:2 * D_MODEL])
    heads_per_chunk = COL_CHUNK // HEAD_DIM
    slabs_per_sub = sub // SLAB

    for s in range(n_sub):
        rows = slice(s * sub, (s + 1) * sub)
        xs = x_ref[0, rows, :]
        ms = jnp.mean(xs * xs, axis=1, keepdims=True)
        h = ((xs * lax.rsqrt(ms + EPS)) * gain + shift).astype(BF16)
        if use_rope:
            cs = cos_ref[rows, :]
            sn = sin_ref[rows, :]
        for c in range(len(plan) // heads_per_chunk):
            acc = jnp.dot(h, w_ref[:, c * COL_CHUNK:(c + 1) * COL_CHUNK],
                          preferred_element_type=F32)
            for hh in range(heads_per_chunk):
                op, o_idx, kind, slot = plan[c * heads_per_chunk + hh]
                t = acc[:, hh * HEAD_DIM:(hh + 1) * HEAD_DIM]
                if op == OP_SILU:
                    t = _silu(t)
                elif op == OP_QNORM_ROPE:
                    t = _rms(t, qn_ref[...])
                elif op in (OP_KNORM_ROPE, OP_KNORM):
                    t = _rms(t, kn_ref[...])
                if op in (OP_ROPE, OP_QNORM_ROPE, OP_KNORM_ROPE):
                    t = t * cs + pltpu.roll(t, HEAD_DIM // 2, 1) * sn
                out = out_refs[o_idx]
                if kind == TO_COLS:
                    out[0, rows, slot:slot + HEAD_DIM] = t.astype(out.dtype)
                elif kind == TO_HEAD:
                    out[0, slot, rows, :] = t.astype(out.dtype)
                else:
                    for r in range(slabs_per_sub):
                        out[0, slot, s * slabs_per_sub + r] = (
                            t[r * SLAB:(r + 1) * SLAB, :].T.astype(out.dtype))


def _projection(x, mod3, mod_row, g, w, *, plan, outs, rope=None, norms=None, name):
    B, L, _ = x.shape
    n_total = w.shape[1]
    assert n_total == len(plan) * HEAD_DIM and n_total % COL_CHUNK == 0
    tm = min(ROW_TILE, L)
    sub = min(SUB_ROWS, tm)
    assert L % tm == 0 and tm % sub == 0 and sub % SLAB == 0
    mod_map = (lambda b, i: (b, 0, 0)) if mod_row is None else (lambda b, i: (mod_row, 0, 0))
    in_specs = [
        pl.BlockSpec((1, tm, D_MODEL), lambda b, i: (b, i, 0)),
        pl.BlockSpec((1, 1, 3 * D_MODEL), mod_map),
        _resident((1, D_MODEL), lambda b, i: (0, 0)),
        _resident((D_MODEL, n_total), lambda b, i: (0, 0)),
    ]
    args = [x, mod3, g.reshape(1, D_MODEL), w]
    if rope is not None:
        in_specs += [pl.BlockSpec((tm, HEAD_DIM), lambda b, i: (i, 0))] * 2
        args += list(rope)
    if norms is not None:
        in_specs += [_resident((1, HEAD_DIM), lambda b, i: (0, 0))] * 2
        args += [n.reshape(1, HEAD_DIM) for n in norms]
    out_specs, out_shape = [], []
    for kind, n, dt in outs:
        if kind == TO_COLS:
            out_specs.append(pl.BlockSpec((1, tm, n), lambda b, i: (b, i, 0)))
            out_shape.append(jax.ShapeDtypeStruct((B, L, n), dt))
        elif kind == TO_HEAD:
            out_specs.append(pl.BlockSpec((1, n, tm, HEAD_DIM), lambda b, i: (b, 0, i, 0)))
            out_shape.append(jax.ShapeDtypeStruct((B, n, L, HEAD_DIM), dt))
        else:
            out_specs.append(pl.BlockSpec((1, n, tm // SLAB, HEAD_DIM, SLAB),
                                          lambda b, i: (b, 0, i, 0, 0)))
            out_shape.append(jax.ShapeDtypeStruct((B, n, L // SLAB, HEAD_DIM, SLAB), dt))
    kern = functools.partial(
        _proj_kernel, plan=plan, n_sub=tm // sub, sub=sub,
        use_rope=rope is not None, use_norm=norms is not None)
    return pl.pallas_call(
        kern,
        grid=(B, L // tm),
        in_specs=in_specs,
        out_specs=out_specs,
        out_shape=out_shape,
        compiler_params=_params(2),
        name=name,
    )(*args)


def _qk(q, k):
    return lax.dot_general(q, k, (((1,), (1,)), ((), ())), preferred_element_type=F32)


def _key_norm2(k):
    kf = k.astype(F32)
    return jnp.max(jnp.sum(kf * kf, axis=1, keepdims=True))


def _concat_slabs(vt_ref, first, count):
    return jnp.concatenate([vt_ref[0, 0, first + t] for t in range(count)], axis=1)


def _dense_kernel(qn_ref, kn_ref, q_ref, k_ref, vt_ref, kc_ref, vct_ref, gz_ref, y_ref,
                  m_sc, l_sc, acc_sc, *, tq, tk, n_steps):
    nq = GROUP * tq
    slabs = tk // SLAB

    def key_tile(j):
        return k_ref[0, 0, pl.ds(pl.multiple_of(j * tk, tk), tk), :]

    def vt_tile(j):
        return _concat_slabs(vt_ref, j * slabs, slabs)

    kc = kc_ref[0, 0]
    vct = _concat_slabs(vct_ref, 0, vct_ref.shape[2])

    l_sc[...] = jnp.zeros(l_sc.shape, F32)
    acc_sc[...] = jnp.zeros(acc_sc.shape, F32)
    q = [q_ref[0, :, g * HEAD_DIM:(g + 1) * HEAD_DIM] for g in range(GROUP)]

    bound = (ATTN_SCALE * HEAD_DIM * BOUND_SLACK) * jnp.max(jnp.abs(qn_ref[...])) * jnp.max(jnp.abs(kn_ref[...]))
    safe = bound <= SAFE_LOGIT

    @pl.when(safe)
    def _():
        q_all = jnp.concatenate(q, axis=0)

        def attend(k, vt):
            p = jnp.exp2(_qk(k, q_all) * EXP2_SCALE)
            l_sc[...] += jnp.sum(p.reshape(-1, SUBLANES, nq), axis=0)
            acc_sc[...] += jnp.dot(vt, p.astype(BF16), preferred_element_type=F32)

        n_loop = n_steps // DENSE_UNROLL

        def body(i, carry):
            for u in range(DENSE_UNROLL):
                j = i * DENSE_UNROLL + u
                attend(key_tile(j), vt_tile(j))
            return carry

        lax.fori_loop(0, n_loop, body, 0)
        for j in range(n_loop * DENSE_UNROLL, n_steps):
            attend(key_tile(j), vt_tile(j))
        attend(kc, vct)

    @pl.when(jnp.logical_not(safe))
    def _():
        m_sc[...] = jnp.full(m_sc.shape, -jnp.inf, F32)

        def attend(k, vt):
            for g in range(GROUP):
                st = _qk(k, q[g])
                m_prev = m_sc[g]
                m_new = jnp.maximum(m_prev, jnp.max(st, axis=0, keepdims=True))
                alpha = jnp.exp2((m_prev - m_new) * EXP2_SCALE)
                p = jnp.exp2((st - m_new) * EXP2_SCALE)
                cols = slice(g * tq, (g + 1) * tq)
                l_sc[0:1, cols] = alpha * l_sc[0:1, cols] + jnp.sum(p, axis=0, keepdims=True)
                acc_sc[:, cols] = alpha * acc_sc[:, cols] + jnp.dot(vt, p.astype(BF16),
                                                                     preferred_element_type=F32)
                m_sc[g] = m_new

        def body(j, carry):
            attend(key_tile(j), vt_tile(j))
            return carry

        lax.fori_loop(0, n_steps, body, 0)
        attend(kc, vct)

    o = (acc_sc[...] / jnp.sum(l_sc[...], axis=0, keepdims=True)).T
    o = jnp.concatenate([o[g * tq:(g + 1) * tq] for g in range(GROUP)], axis=1)
    y_ref[0] = (o * gz_ref[0]).astype(y_ref.dtype)


def _kv_specs(L, C, head0):
    return [
        pl.BlockSpec((1, 1, L, HEAD_DIM), lambda b, h, i: (b, head0 + h, 0, 0)),
        pl.BlockSpec((1, 1, L // SLAB, HEAD_DIM, SLAB), lambda b, h, i: (b, head0 + h, 0, 0, 0)),
        pl.BlockSpec((1, 1, C, HEAD_DIM), lambda b, h, i: (b, head0 + h, 0, 0)),
        pl.BlockSpec((1, 1, C // SLAB, HEAD_DIM, SLAB), lambda b, h, i: (b, head0 + h, 0, 0, 0)),
    ]


def _dense_attention(q, k, vt, kc, vct, gz, q_norm, k_norm):
    B, L, _ = q.shape
    C = kc.shape[2]
    tq = min(ATTN_TQ, L)
    tk = min(ATTN_TK, L)
    assert L % tq == 0 and L % tk == 0 and tk % SLAB == 0 and C % SLAB == 0
    gw = GROUP * HEAD_DIM
    kern = functools.partial(_dense_kernel, tq=tq, tk=tk, n_steps=L // tk)
    qz_spec = pl.BlockSpec((1, tq, gw), lambda b, h, i: (b, i, h))
    gain_spec = _resident((1, HEAD_DIM), lambda b, h, i: (0, 0))
    return pl.pallas_call(
        kern,
        grid=(B, N_KV_A, L // tq),
        in_specs=[gain_spec, gain_spec, qz_spec] + _kv_specs(L, C, 0) + [qz_spec],
        out_specs=qz_spec,
        out_shape=jax.ShapeDtypeStruct((B, L, N_HEADS_A * HEAD_DIM), BF16),
        scratch_shapes=[
            pltpu.VMEM((GROUP, 1, tq), F32),
            pltpu.VMEM((SUBLANES, GROUP * tq), F32),
            pltpu.VMEM((HEAD_DIM, GROUP * tq), F32),
        ],
        compiler_params=_params(3),
        name="dense_attention",
    )(q_norm.reshape(1, HEAD_DIM), k_norm.reshape(1, HEAD_DIM), q, k, vt, kc, vct, gz)


def _window_kernel(sink_ref, q_ref, k_ref, vt_ref, kc_ref, vct_ref, gz_ref, y_ref, k2_sc,
                   *, tq, sub, seq_len):
    h = pl.program_id(1)
    i = pl.program_id(2)
    band = sub + 2 * WINDOW
    nq = GROUP * sub
    n_sub = tq // sub

    kc = kc_ref[0, 0]
    vct = _concat_slabs(vct_ref, 0, vct_ref.shape[2])

    @pl.when(i == 0)
    def _():
        def kbody(j, best):
            off = pl.multiple_of(j * ATTN_TK, ATTN_TK)
            return jnp.maximum(best, _key_norm2(k_ref[0, 0, pl.ds(off, ATTN_TK), :]))

        k2_sc[0] = lax.fori_loop(0, seq_len // ATTN_TK, kbody, _key_norm2(kc))

    lane_head = lax.broadcasted_iota(jnp.int32, (1, nq), 1) // sub
    sink = jnp.zeros((1, nq), F32)
    sink_max = sink_ref[h * GROUP]
    for g in range(GROUP):
        sink = jnp.where(lane_head == g, sink_ref[h * GROUP + g], sink)
        sink_max = jnp.maximum(sink_max, sink_ref[h * GROUP + g])
    sink = sink * LOG2_E
    rel = (lax.broadcasted_iota(jnp.int32, (band, sub), 0)
           - lax.broadcasted_iota(jnp.int32, (band, sub), 1))
    q_all = [jnp.concatenate([q_ref[0, c * sub:(c + 1) * sub, g * HEAD_DIM:(g + 1) * HEAD_DIM]
                              for g in range(GROUP)], axis=0) for c in range(n_sub)]
    q2 = functools.reduce(jnp.maximum, [_key_norm2(qc) for qc in q_all])
    safe = jnp.logical_and(
        q2 * k2_sc[0] * (ATTN_SCALE * ATTN_SCALE * BOUND_SLACK) <= SAFE_LOGIT * SAFE_LOGIT,
        sink_max <= SAFE_LOGIT)

    def scores(c):
        start = i * tq + c * sub
        kstart = pl.multiple_of(jnp.clip(start - WINDOW, 0, seq_len - band), WINDOW)
        bias = jnp.where(jnp.abs(rel + (kstart - start)) <= WINDOW, 0.0, NEG_INF)
        s = (_qk(k_ref[0, 0, pl.ds(kstart, band), :], q_all[c]) * EXP2_SCALE
             + jnp.concatenate([bias] * GROUP, axis=1))
        sc = _qk(kc, q_all[c]) * EXP2_SCALE
        return s, sc, kstart

    def finish(c, s, sc, kstart, exact_max):
        rows = slice(c * sub, (c + 1) * sub)
        if exact_max:
            m = jnp.maximum(jnp.maximum(jnp.max(s, axis=0, keepdims=True),
                                        jnp.max(sc, axis=0, keepdims=True)), sink)
            s, sc, sink_c = s - m, sc - m, sink - m
        else:
            sink_c = sink
        p = jnp.exp2(s)
        pc = jnp.exp2(sc)
        denom = (jnp.sum(p, axis=0, keepdims=True) + jnp.sum(pc, axis=0, keepdims=True)
                 + jnp.exp2(sink_c))
        vbt = _concat_slabs(vt_ref, kstart // SLAB, band // SLAB)
        o = (jnp.dot(vbt, p.astype(BF16), preferred_element_type=F32)
             + jnp.dot(vct, pc.astype(BF16), preferred_element_type=F32)) / denom
        o = o.T
        o = jnp.concatenate([o[g * sub:(g + 1) * sub] for g in range(GROUP)], axis=1)
        y_ref[0, rows, :] = (o * gz_ref[0, rows, :]).astype(y_ref.dtype)

    def all_sub_tiles(exact_max):
        nxt = scores(0)
        for c in range(n_sub):
            cur = nxt
            if c + 1 < n_sub:
                nxt = scores(c + 1)
            finish(c, *cur, exact_max)

    @pl.when(safe)
    def _():
        all_sub_tiles(exact_max=False)

    @pl.when(jnp.logical_not(safe))
    def _():
        all_sub_tiles(exact_max=True)


def _window_attention(q, k, vt, kc, vct, gz, sink):
    B, L, _ = q.shape
    C = kc.shape[2]
    sub = min(WINDOW_SUB, L - 2 * WINDOW)
    tq = min(WINDOW_TQ, L)
    assert L % tq == 0 and tq % sub == 0 and L >= sub + 2 * WINDOW and L % ATTN_TK == 0
    assert WINDOW % SLAB == 0 and sub % SLAB == 0 and C % SLAB == 0
    gw = GROUP * HEAD_DIM
    q_col = N_HEADS_A // GROUP
    kern = functools.partial(_window_kernel, tq=tq, sub=sub, seq_len=L)
    qz_spec = pl.BlockSpec((1, tq, gw), lambda b, h, i: (b, i, q_col + h))
    return pl.pallas_call(
        kern,
        grid=(B, N_KV_B, L // tq),
        in_specs=[pl.BlockSpec(memory_space=pltpu.SMEM), qz_spec] + _kv_specs(L, C, N_KV_A) + [qz_spec],
        out_specs=pl.BlockSpec((1, tq, gw), lambda b, h, i: (b, i, h)),
        out_shape=jax.ShapeDtypeStruct((B, L, N_HEADS_B * HEAD_DIM), BF16),
        scratch_shapes=[pltpu.SMEM((1,), F32)],
        compiler_params=_params(3),
        name="window_attention",
    )(sink, q, k, vt, kc, vct, gz)


def _finish_rows(r, x_rows, gated_gain):
    return x_rows + _rms(r, gated_gain)


def _out_kernel(ya_ref, yb_ref, w_ref, x_ref, mod_ref, g_ref, o_ref, *, n_sub, sub):
    gated_gain = mod_ref[0][:, 2 * D_MODEL:] * g_ref[...]
    for s in range(n_sub):
        rows = slice(s * sub, (s + 1) * sub)
        y = jnp.concatenate([ya_ref[0, rows, :], yb_ref[0, rows, :]], axis=1)
        r = jnp.dot(y, w_ref[...], preferred_element_type=F32)
        o_ref[0, rows, :] = _finish_rows(r, x_ref[0, rows, :], gated_gain)


def _out_projection(ya, yb, w, x, mod3, post_g):
    B, L, _ = x.shape
    tm = min(ROW_TILE, L)
    sub = min(SUB_ROWS, tm)
    wa = ya.shape[2]
    kern = functools.partial(_out_kernel, n_sub=tm // sub, sub=sub)
    return pl.pallas_call(
        kern,
        grid=(B, L // tm),
        in_specs=[
            pl.BlockSpec((1, tm, wa), lambda b, i: (b, i, 0)),
            pl.BlockSpec((1, tm, yb.shape[2]), lambda b, i: (b, i, 0)),
            _resident(w.shape, lambda b, i: (0, 0)),
            pl.BlockSpec((1, tm, D_MODEL), lambda b, i: (b, i, 0)),
            pl.BlockSpec((1, 1, 3 * D_MODEL), lambda b, i: (b, 0, 0)),
            _resident((1, D_MODEL), lambda b, i: (0, 0)),
        ],
        out_specs=pl.BlockSpec((1, tm, D_MODEL), lambda b, i: (b, i, 0)),
        out_shape=jax.ShapeDtypeStruct((B, L, D_MODEL), F32),
        compiler_params=_params(2),
        name="out_projection",
    )(ya, yb, w, x, mod3, post_g.reshape(1, D_MODEL))


def _pool_kernel(u_ref, up_ref, un_ref, gt_ref, pw_ref, ps_ref, w_ref, x_ref, mod_ref, g_ref,
                 o_ref, *, tm, n_sub, sub, seq_len):
    i = pl.program_id(1)
    gated_gain = mod_ref[0][:, 2 * D_MODEL:] * g_ref[...]

    for s in range(n_sub):
        r0 = s * sub
        rows = slice(r0, r0 + sub)
        t = i * tm + r0 + lax.broadcasted_iota(jnp.int32, (sub, 1), 0)
        mixed = []
        for g, w in enumerate(POOL_SIZES):
            cols = slice(g * POOL_GROUP, (g + 1) * POOL_GROUP)
            half = w // 2
            if s == 0:
                before = jnp.where(i > 0, up_ref[0, :, cols], 0.0)
            else:
                before = u_ref[0, r0 - POOL_HALO:r0, cols]
            if s == n_sub - 1:
                after = jnp.where(i < pl.num_programs(1) - 1, un_ref[0, :, cols], 0.0)
            else:
                after = u_ref[0, r0 + sub:r0 + sub + POOL_HALO, cols]
            u = u_ref[0, rows, cols]
            ext = jnp.concatenate([before, u, after], axis=0)
            tot = ext[1:] + ext[:-1]
            lo = 1
            span = 1
            while 2 * span < w:
                tot = tot[:-2 * span] + tot[2 * span:]
                lo += span
                span *= 2
            tot = tot[POOL_HALO - lo:POOL_HALO - lo + sub]
            cnt = jnp.minimum(t + half, seq_len) - jnp.maximum(t - half, 0)
            pooled = tot / cnt.astype(F32) - u
            mixed.append(jnp.dot(pooled.astype(BF16), pw_ref[g], preferred_element_type=F32))
        y = jnp.concatenate(mixed, axis=1) * ps_ref[...]
        y = (y * _silu(gt_ref[0, rows, :])).astype(BF16)
        r = jnp.dot(y, w_ref[...], preferred_element_type=F32)
        o_ref[0, rows, :] = _finish_rows(r, x_ref[0, rows, :], gated_gain)


def _pool_tail(u, gt, pool_w, pool_scale, w_out, x, mod3, post_g):
    B, L, _ = x.shape
    tm = min(ROW_TILE, L)
    sub = min(SUB_ROWS, tm)
    hb = tm // POOL_HALO
    n_hb = L // POOL_HALO
    kern = functools.partial(_pool_kernel, tm=tm, n_sub=tm // sub, sub=sub, seq_len=L)
    row_spec = pl.BlockSpec((1, tm, D_MODEL), lambda b, i: (b, i, 0))
    return pl.pallas_call(
        kern,
        grid=(B, L // tm),
        in_specs=[
            row_spec,
            pl.BlockSpec((1, POOL_HALO, D_MODEL), lambda b, i: (b, jnp.maximum(i * hb - 1, 0), 0)),
            pl.BlockSpec((1, POOL_HALO, D_MODEL),
                         lambda b, i: (b, jnp.minimum((i + 1) * hb, n_hb - 1), 0)),
            row_spec,
            _resident(pool_w.shape, lambda b, i: (0, 0, 0)),
            _resident((1, D_MODEL), lambda b, i: (0, 0)),
            _resident(w_out.shape, lambda b, i: (0, 0)),
            row_spec,
            pl.BlockSpec((1, 1, 3 * D_MODEL), lambda b, i: (b, 0, 0)),
            _resident((1, D_MODEL), lambda b, i: (0, 0)),
        ],
        out_specs=row_spec,
        out_shape=jax.ShapeDtypeStruct((B, L, D_MODEL), F32),
        compiler_params=_params(2),
        name="pool_tail",
    )(u, u, u, gt, pool_w, pool_scale.reshape(1, D_MODEL), w_out, x, mod3,
      post_g.reshape(1, D_MODEL))


def _rope_tables(seq_len):
    rows = seq_len // GRID_W
    inv = ROPE_THETA ** (-jnp.arange(AXIS_FREQS, dtype=F32) / AXIS_FREQS)
    row_ang = jnp.arange(rows, dtype=F32)[:, None] * inv
    col_ang = jnp.arange(GRID_W, dtype=F32)[:, None] * inv

    def table(fn, sign):
        shape = (rows, GRID_W, AXIS_FREQS)
        half = jnp.concatenate([jnp.broadcast_to(fn(row_ang)[:, None, :], shape),
                                jnp.broadcast_to(fn(col_ang)[None, :, :], shape)], axis=-1)
        return jnp.concatenate([sign * half, half], axis=-1).reshape(seq_len, HEAD_DIM)

    return table(jnp.cos, 1.0), table(jnp.sin, -1.0)


def kernel(x, c, ctx, c_ctx, ev_mod_w, ev_mod_b, ev_pre_g, ev_post_g, ev_w_in, ev_q_norm, ev_k_norm,
           ev_sink, ev_w_out, od_mod_w, od_mod_b, od_pre_g, od_post_g, od_w_in, od_pool_w,
           od_pool_scale, od_w_out):
    B, L, D = x.shape
    assert D == D_MODEL and L % GRID_W == 0
    assert ev_mod_w.shape[0] == 1 and od_mod_w.shape[0] == 1
    assert B + 1 <= MOD_ROWS
    ctx_row = B

    cvec = jnp.zeros((MOD_ROWS, D), F32).at[:B].set(c).at[ctx_row].set(c_ctx)
    mod_e = _modulation(cvec, ev_mod_w[0], ev_mod_b[0]).reshape(MOD_ROWS, 1, 3 * D)
    mod_o = _modulation(cvec, od_mod_w[0], od_mod_b[0]).reshape(MOD_ROWS, 1, 3 * D)

    w_in_e = ev_w_in[0].astype(BF16)
    norms = (ev_q_norm[0], ev_k_norm[0])
    n_kv = N_KV_A + N_KV_B

    def kv_plan(k_out, v_out, ka_op, kb_op):
        return ([(ka_op, k_out, TO_HEAD, h) for h in range(N_KV_A)]
                + [(OP_RAW, v_out, TO_SLABS_T, h) for h in range(N_KV_A)]
                + [(kb_op, k_out, TO_HEAD, N_KV_A + h) for h in range(N_KV_B)]
                + [(OP_RAW, v_out, TO_SLABS_T, N_KV_A + h) for h in range(N_KV_B)])

    plan = tuple([(OP_QNORM_ROPE, 0, TO_COLS, h * HEAD_DIM) for h in range(N_HEADS_A)]
                 + [(OP_ROPE, 0, TO_COLS, (N_HEADS_A + h) * HEAD_DIM) for h in range(N_HEADS_B)]
                 + kv_plan(1, 2, OP_KNORM_ROPE, OP_ROPE)
                 + [(OP_SILU, 3, TO_COLS, h * HEAD_DIM) for h in range(Q_W // HEAD_DIM)])
    q, k, vt, gz = _projection(
        x, mod_e, None, ev_pre_g[0], w_in_e, plan=plan,
        outs=((TO_COLS, Q_W, BF16), (TO_HEAD, n_kv, BF16), (TO_SLABS_T, n_kv, BF16), (TO_COLS, Q_W, F32)),
        rope=_rope_tables(L), norms=norms, name="in_projection_even")
    kc, vct = _projection(
        ctx, mod_e, ctx_row, ev_pre_g[0], w_in_e[:, Q_W:Q_W + KV_W],
        plan=tuple(kv_plan(0, 1, OP_KNORM, OP_RAW)),
        outs=((TO_HEAD, n_kv, BF16), (TO_SLABS_T, n_kv, BF16)), norms=norms, name="in_projection_ctx")

    ya = _dense_attention(q, k, vt, kc, vct, gz, *norms)
    yb = _window_attention(q, k, vt, kc, vct, gz, ev_sink[0])
    x = _out_projection(ya, yb, ev_w_out[0].astype(BF16), x, mod_e, ev_post_g[0])

    u, gt = _projection(
        x, mod_o, None, od_pre_g[0], od_w_in[0].astype(BF16),
        plan=tuple((OP_RAW, col // D, TO_COLS, col % D) for col in range(0, 2 * D, HEAD_DIM)),
        outs=((TO_COLS, D, F32), (TO_COLS, D, F32)), name="in_projection_odd")
    return _pool_tail(u, gt, od_pool_w[0].astype(BF16), od_pool_scale[0], od_w_out[0].astype(BF16),
                      x, mod_o, od_post_g[0])
```

```python
import functools
import math

import jax
import jax.numpy as jnp
from jax import lax
from jax.experimental import pallas as pl
from jax.experimental.pallas import tpu as pltpu

D_MODEL = 2048
GRID_W = 64
HEAD_DIM = 128
AXIS_FREQS = HEAD_DIM // 4
ROPE_THETA = 10000.0
N_HEADS_A = 8
N_KV_A = 2
N_HEADS_B = 8
N_KV_B = 2
GROUP = N_HEADS_A // N_KV_A
WINDOW = 128
ATTN_SCALE = HEAD_DIM ** -0.5
NEG_INF = -1e30
Q_W = (N_HEADS_A + N_HEADS_B) * HEAD_DIM
KV_A_W = N_KV_A * HEAD_DIM
KV_B_W = N_KV_B * HEAD_DIM
KV_W = 2 * KV_A_W + 2 * KV_B_W
POOL_SIZES = (2, 4, 8, 16)
POOL_GROUP = D_MODEL // len(POOL_SIZES)
POOL_HALO = 8
EPS = 1e-6

LANES = 128
SUBLANES = 8
VMEM_LIMIT_BYTES = 56 * 1024 * 1024

MOD_ROWS = SUBLANES
MOD_TN = 512
ROW_TILE = 512
SUB_ROWS = 256
COL_CHUNK = 512
ATTN_TQ = 1024
ATTN_TK = 1024
DENSE_UNROLL = 4
WINDOW_TQ = 2048
WINDOW_SUB = 128
LOG2_E = math.log2(math.e)
EXP2_SCALE = ATTN_SCALE * LOG2_E
BOUND_SLACK = 1.0 + 2.0 ** -7
SAFE_LOGIT = 50.0

OP_RAW, OP_ROPE, OP_QNORM_ROPE, OP_KNORM_ROPE, OP_KNORM, OP_SILU = range(6)
TO_COLS, TO_HEAD, TO_SLABS_T = range(3)
SLAB = LANES

F32 = jnp.float32
BF16 = jnp.bfloat16


def _params(n_axes):
    return pltpu.CompilerParams(
        dimension_semantics=("arbitrary",) * n_axes,
        vmem_limit_bytes=VMEM_LIMIT_BYTES,
    )


def _resident(block_shape, index_map):
    return pl.BlockSpec(block_shape, index_map, pipeline_mode=pl.Buffered(1))


def _silu(v):
    return v * jax.nn.sigmoid(v)


def _rms(v, gain):
    ms = jnp.mean(v * v, axis=1, keepdims=True)
    return (v * lax.rsqrt(ms + EPS)) * gain


def _mod_kernel(c_ref, w_ref, b_ref, o_ref):
    a = _silu(c_ref[...]).astype(BF16)
    o_ref[...] = jnp.dot(a, w_ref[...].astype(BF16), preferred_element_type=F32) + b_ref[...]


def _modulation(cvec, w, b):
    n = w.shape[1]
    return pl.pallas_call(
        _mod_kernel,
        grid=(n // MOD_TN,),
        in_specs=[
            pl.BlockSpec((MOD_ROWS, D_MODEL), lambda j: (0, 0)),
            pl.BlockSpec((D_MODEL, MOD_TN), lambda j: (0, j)),
            pl.BlockSpec((1, MOD_TN), lambda j: (0, j)),
        ],
        out_specs=pl.BlockSpec((MOD_ROWS, MOD_TN), lambda j: (0, j)),
        out_shape=jax.ShapeDtypeStruct((MOD_ROWS, n), F32),
        compiler_params=_params(1),
        name="modulation",
    )(cvec, w, b.reshape(1, n))


def _proj_kernel(*refs, plan, n_sub, sub, use_rope, use_norm):
    x_ref, mod_ref, g_ref, w_ref = refs[:4]
    pos = 4
    if use_rope:
        cos_ref, sin_ref = refs[pos:pos + 2]
        pos += 2
    if use_norm:
        qn_ref, kn_ref = refs[pos:pos + 2]
        pos += 2
    out_refs = refs[pos:]

    mrow = mod_ref[0]
    shift = mrow[:, :D_MODEL]
    gain = g_ref[...] * (1.0 + mrow[:, D_MODEL:2 * D_MODEL])
    heads_per_chunk = COL_CHUNK // HEAD_DIM
    slabs_per_sub = sub // SLAB

    for s in range(n_sub):
        rows = slice(s * sub, (s + 1) * sub)
        xs = x_ref[0, rows, :]
        ms = jnp.mean(xs * xs, axis=1, keepdims=True)
        h = ((xs * lax.rsqrt(ms + EPS)) * gain + shift).astype(BF16)
        if use_rope:
            cs = cos_ref[rows, :]
            sn = sin_ref[rows, :]
        for c in range(len(plan) // heads_per_chunk):
            acc = jnp.dot(h, w_ref[:, c * COL_CHUNK:(c + 1) * COL_CHUNK],
                          preferred_element_type=F32)
            for hh in range(heads_per_chunk):
                op, o_idx, kind, slot = plan[c * heads_per_chunk + hh]
                t = acc[:, hh * HEAD_DIM:(hh + 1) * HEAD_DIM]
                if op == OP_SILU:
                    t = _silu(t)
                elif op == OP_QNORM_ROPE:
                    t = _rms(t, qn_ref[...])
                elif op in (OP_KNORM_ROPE, OP_KNORM):
                    t = _rms(t, kn_ref[...])
                if op in (OP_ROPE, OP_QNORM_ROPE, OP_KNORM_ROPE):
                    t = t * cs + pltpu.roll(t, HEAD_DIM // 2, 1) * sn
                out = out_refs[o_idx]
                if kind == TO_COLS:
                    out[0, rows, slot:slot + HEAD_DIM] = t.astype(out.dtype)
                elif kind == TO_HEAD:
                    out[0, slot, rows, :] = t.astype(out.dtype)
                else:
                    for r in range(slabs_per_sub):
                        out[0, slot, s * slabs_per_sub + r] = (
                            t[r * SLAB:(r + 1) * SLAB, :].T.astype(out.dtype))


def _projection(x, mod3, mod_row, g, w, *, plan, outs, rope=None, norms=None, name):
    B, L, _ = x.shape
    n_total = w.shape[1]
    assert n_total == len(plan) * HEAD_DIM and n_total % COL_CHUNK == 0
    tm = min(ROW_TILE, L)
    sub = min(SUB_ROWS, tm)
    assert L % tm == 0 and tm % sub == 0 and sub % SLAB == 0
    mod_map = (lambda b, i: (b, 0, 0)) if mod_row is None else (lambda b, i: (mod_row, 0, 0))
    in_specs = [
        pl.BlockSpec((1, tm, D_MODEL), lambda b, i: (b, i, 0)),
        pl.BlockSpec((1, 1, 3 * D_MODEL), mod_map),
        _resident((1, D_MODEL), lambda b, i: (0, 0)),
        _resident((D_MODEL, n_total), lambda b, i: (0, 0)),
    ]
    args = [x, mod3, g.reshape(1, D_MODEL), w]
    if rope is not None:
        in_specs += [pl.BlockSpec((tm, HEAD_DIM), lambda b, i: (i, 0))] * 2
        args += list(rope)
    if norms is not None:
        in_specs += [_resident((1, HEAD_DIM), lambda b, i: (0, 0))] * 2
        args += [n.reshape(1, HEAD_DIM) for n in norms]
    out_specs, out_shape = [], []
    for kind, n, dt in outs:
        if kind == TO_COLS:
            out_specs.append(pl.BlockSpec((1, tm, n), lambda b, i: (b, i, 0)))
            out_shape.append(jax.ShapeDtypeStruct((B, L, n), dt))
        elif kind == TO_HEAD:
            out_specs.append(pl.BlockSpec((1, n, tm, HEAD_DIM), lambda b, i: (b, 0, i, 0)))
            out_shape.append(jax.ShapeDtypeStruct((B, n, L, HEAD_DIM), dt))
        else:
            out_specs.append(pl.BlockSpec((1, n, tm // SLAB, HEAD_DIM, SLAB),
                                          lambda b, i: (b, 0, i, 0, 0)))
            out_shape.append(jax.ShapeDtypeStruct((B, n, L // SLAB, HEAD_DIM, SLAB), dt))
    kern = functools.partial(
        _proj_kernel, plan=plan, n_sub=tm // sub, sub=sub,
        use_rope=rope is not None, use_norm=norms is not None)
    return pl.pallas_call(
        kern,
        grid=(B, L // tm),
        in_specs=in_specs,
        out_specs=out_specs,
        out_shape=out_shape,
        compiler_params=_params(2),
        name=name,
    )(*args)


def _qk(q, k):
    return lax.dot_general(q, k, (((1,), (1,)), ((), ())), preferred_element_type=F32)


def _key_norm2(k):
    kf = k.astype(F32)
    return jnp.max(jnp.sum(kf * kf, axis=1, keepdims=True))


def _concat_slabs(vt_ref, first, count):
    return jnp.concatenate([vt_ref[0, 0, first + t] for t in range(count)], axis=1)


def _dense_kernel(qn_ref, kn_ref, q_ref, k_ref, vt_ref, kc_ref, vct_ref, gz_ref, y_ref,
                  m_sc, l_sc, acc_sc, *, tq, tk, n_steps):
    nq = GROUP * tq
    slabs = tk // SLAB

    def key_tile(j):
        return k_ref[0, 0, pl.ds(pl.multiple_of(j * tk, tk), tk), :]

    def vt_tile(j):
        return _concat_slabs(vt_ref, j * slabs, slabs)

    kc = kc_ref[0, 0]
    vct = _concat_slabs(vct_ref, 0, vct_ref.shape[2])

    l_sc[...] = jnp.zeros(l_sc.shape, F32)
    acc_sc[...] = jnp.zeros(acc_sc.shape, F32)
    q = [q_ref[0, :, g * HEAD_DIM:(g + 1) * HEAD_DIM] for g in range(GROUP)]

    bound = (ATTN_SCALE * HEAD_DIM * BOUND_SLACK) * jnp.max(jnp.abs(qn_ref[...])) * jnp.max(jnp.abs(kn_ref[...]))
    safe = bound <= SAFE_LOGIT

    @pl.when(safe)
    def _():
        q_all = jnp.concatenate(q, axis=0)

        def attend(k, vt):
            p = jnp.exp2(_qk(k, q_all) * EXP2_SCALE)
            l_sc[...] += jnp.sum(p.reshape(-1, SUBLANES, nq), axis=0)
            acc_sc[...] += jnp.dot(vt, p.astype(BF16), preferred_element_type=F32)

        n_loop = n_steps // DENSE_UNROLL

        def body(i, carry):
            for u in range(DENSE_UNROLL):
                j = i * DENSE_UNROLL + u
                attend(key_tile(j), vt_tile(j))
            return carry

        lax.fori_loop(0, n_loop, body, 0)
        for j in range(n_loop * DENSE_UNROLL, n_steps):
            attend(key_tile(j), vt_tile(j))
        attend(kc, vct)

    @pl.when(jnp.logical_not(safe))
    def _():
        m_sc[...] = jnp.full(m_sc.shape, -jnp.inf, F32)

        def attend(k, vt):
            for g in range(GROUP):
                st = _qk(k, q[g])
                m_prev = m_sc[g]
                m_new = jnp.maximum(m_prev, jnp.max(st, axis=0, keepdims=True))
                alpha = jnp.exp2((m_prev - m_new) * EXP2_SCALE)
                p = jnp.exp2((st - m_new) * EXP2_SCALE)
                cols = slice(g * tq, (g + 1) * tq)
                l_sc[0:1, cols] = alpha * l_sc[0:1, cols] + jnp.sum(p, axis=0, keepdims=True)
                acc_sc[:, cols] = alpha * acc_sc[:, cols] + jnp.dot(vt, p.astype(BF16),
                                                                     preferred_element_type=F32)
                m_sc[g] = m_new

        def body(j, carry):
            attend(key_tile(j), vt_tile(j))
            return carry

        lax.fori_loop(0, n_steps, body, 0)
        attend(kc, vct)

    o = (acc_sc[...] / jnp.sum(l_sc[...], axis=0, keepdims=True)).T
    o = jnp.concatenate([o[g * tq:(g + 1) * tq] for g in range(GROUP)], axis=1)
    y_ref[0] = (o * gz_ref[0]).astype(y_ref.dtype)


def _kv_specs(L, C, head0):
    return [
        pl.BlockSpec((1, 1, L, HEAD_DIM), lambda b, h, i: (b, head0 + h, 0, 0)),
        pl.BlockSpec((1, 1, L // SLAB, HEAD_DIM, SLAB), lambda b, h, i: (b, head0 + h, 0, 0, 0)),
        pl.BlockSpec((1, 1, C, HEAD_DIM), lambda b, h, i: (b, head0 + h, 0, 0)),
        pl.BlockSpec((1, 1, C // SLAB, HEAD_DIM, SLAB), lambda b, h, i: (b, head0 + h, 0, 0, 0)),
    ]


def _dense_attention(q, k, vt, kc, vct, gz, q_norm, k_norm):
    B, L, _ = q.shape
    C = kc.shape[2]
    tq = min(ATTN_TQ, L)
    tk = min(ATTN_TK, L)
    assert L % tq == 0 and L % tk == 0 and tk % SLAB == 0 and C % SLAB == 0
    gw = GROUP * HEAD_DIM
    kern = functools.partial(_dense_kernel, tq=tq, tk=tk, n_steps=L // tk)
    qz_spec = pl.BlockSpec((1, tq, gw), lambda b, h, i: (b, i, h))
    gain_spec = _resident((1, HEAD_DIM), lambda b, h, i: (0, 0))
    return pl.pallas_call(
        kern,
        grid=(B, N_KV_A, L // tq),
        in_specs=[gain_spec, gain_spec, qz_spec] + _kv_specs(L, C, 0) + [qz_spec],
        out_specs=qz_spec,
        out_shape=jax.ShapeDtypeStruct((B, L, N_HEADS_A * HEAD_DIM), BF16),
        scratch_shapes=[
            pltpu.VMEM((GROUP, 1, tq), F32),
            pltpu.VMEM((SUBLANES, GROUP * tq), F32),
            pltpu.VMEM((HEAD_DIM, GROUP * tq), F32),
        ],
        compiler_params=_params(3),
        name="dense_attention",
    )(q_norm.reshape(1, HEAD_DIM), k_norm.reshape(1, HEAD_DIM), q, k, vt, kc, vct, gz)


def _window_kernel(sink_ref, q_ref, k_ref, vt_ref, kc_ref, vct_ref, gz_ref, y_ref, k2_sc,
                   *, tq, sub, seq_len):
    h = pl.program_id(1)
    i = pl.program_id(2)
    band = sub + 2 * WINDOW
    nq = GROUP * sub
    n_sub = tq // sub

    kc = kc_ref[0, 0]
    vct = _concat_slabs(vct_ref, 0, vct_ref.shape[2])

    @pl.when(i == 0)
    def _():
        def kbody(j, best):
            off = pl.multiple_of(j * ATTN_TK, ATTN_TK)
            return jnp.maximum(best, _key_norm2(k_ref[0, 0, pl.ds(off, ATTN_TK), :]))

        k2_sc[0] = lax.fori_loop(0, seq_len // ATTN_TK, kbody, _key_norm2(kc))

    lane_head = lax.broadcasted_iota(jnp.int32, (1, nq), 1) // sub
    sink = jnp.zeros((1, nq), F32)
    sink_max = sink_ref[h * GROUP]
    for g in range(GROUP):
        sink = jnp.where(lane_head == g, sink_ref[h * GROUP + g], sink)
        sink_max = jnp.maximum(sink_max, sink_ref[h * GROUP + g])
    sink = sink * LOG2_E
    rel = (lax.broadcasted_iota(jnp.int32, (band, sub), 0)
           - lax.broadcasted_iota(jnp.int32, (band, sub), 1))
    q_all = [jnp.concatenate([q_ref[0, c * sub:(c + 1) * sub, g * HEAD_DIM:(g + 1) * HEAD_DIM]
                              for g in range(GROUP)], axis=0) for c in range(n_sub)]
    q2 = functools.reduce(jnp.maximum, [_key_norm2(qc) for qc in q_all])
    safe = jnp.logical_and(
        q2 * k2_sc[0] * (ATTN_SCALE * ATTN_SCALE * BOUND_SLACK) <= SAFE_LOGIT * SAFE_LOGIT,
        sink_max <= SAFE_LOGIT)

    def scores(c):
        start = i * tq + c * sub
        kstart = pl.multiple_of(jnp.clip(start - WINDOW, 0, seq_len - band), WINDOW)
        bias = jnp.where(jnp.abs(rel + (kstart - start)) <= WINDOW, 0.0, NEG_INF)
        s = (_qk(k_ref[0, 0, pl.ds(kstart, band), :], q_all[c]) * EXP2_SCALE
             + jnp.concatenate([bias] * GROUP, axis=1))
        sc = _qk(kc, q_all[c]) * EXP2_SCALE
        return s, sc, kstart

    def finish(c, s, sc, kstart, exact_max):
        rows = slice(c * sub, (c + 1) * sub)
        if exact_max:
            m = jnp.maximum(jnp.maximum(jnp.max(s, axis=0, keepdims=True),
                                        jnp.max(sc, axis=0, keepdims=True)), sink)
            s, sc, sink_c = s - m, sc - m, sink - m
        else:
            sink_c = sink
        p = jnp.exp2(s)
        pc = jnp.exp2(sc)
        denom = (jnp.sum(p, axis=0, keepdims=True) + jnp.sum(pc, axis=0, keepdims=True)
                 + jnp.exp2(sink_c))
        vbt = _concat_slabs(vt_ref, kstart // SLAB, band // SLAB)
        o = (jnp.dot(vbt, p.astype(BF16), preferred_element_type=F32)
             + jnp.dot(vct, pc.astype(BF16), preferred_element_type=F32)) / denom
        o = o.T
        o = jnp.concatenate([o[g * sub:(g + 1) * sub] for g in range(GROUP)], axis=1)
        y_ref[0, rows, :] = (o * gz_ref[0, rows, :]).astype(y_ref.dtype)

    def all_sub_tiles(exact_max):
        nxt = scores(0)
        for c in range(n_sub):
            cur = nxt
            if c + 1 < n_sub:
                nxt = scores(c + 1)
            finish(c, *cur, exact_max)

    @pl.when(safe)
    def _():
        all_sub_tiles(exact_max=False)

    @pl.when(jnp.logical_not(safe))
    def _():
        all_sub_tiles(exact_max=True)


def _window_attention(q, k, vt, kc, vct, gz, sink):
    B, L, _ = q.shape
    C = kc.shape[2]
    sub = min(WINDOW_SUB, L - 2 * WINDOW)
    tq = min(WINDOW_TQ, L)
    assert L % tq == 0 and tq % sub == 0 and L >= sub + 2 * WINDOW and L % ATTN_TK == 0
    assert WINDOW % SLAB == 0 and sub % SLAB == 0 and C % SLAB == 0
    gw = GROUP * HEAD_DIM
    q_col = N_HEADS_A // GROUP
    kern = functools.partial(_window_kernel, tq=tq, sub=sub, seq_len=L)
    qz_spec = pl.BlockSpec((1, tq, gw), lambda b, h, i: (b, i, q_col + h))
    return pl.pallas_call(
        kern,
        grid=(B, N_KV_B, L // tq),
        in_specs=[pl.BlockSpec(memory_space=pltpu.SMEM), qz_spec] + _kv_specs(L, C, N_KV_A) + [qz_spec],
        out_specs=pl.BlockSpec((1, tq, gw), lambda b, h, i: (b, i, h)),
        out_shape=jax.ShapeDtypeStruct((B, L, N_HEADS_B * HEAD_DIM), BF16),
        scratch_shapes=[pltpu.SMEM((1,), F32)],
        compiler_params=_params(3),
        name="window_attention",
    )(sink, q, k, vt, kc, vct, gz)


def _finish_rows(r, x_rows, gated_gain):
    return x_rows + _rms(r, gated_gain)


def _out_kernel(ya_ref, yb_ref, w_ref, x_ref, mod_ref, g_ref, o_ref, *, n_sub, sub):
    gated_gain = mod_ref[0][:, 2 * D_MODEL:] * g_ref[...]
    for s in range(n_sub):
        rows = slice(s * sub, (s + 1) * sub)
        y = jnp.concatenate([ya_ref[0, rows, :], yb_ref[0, rows, :]], axis=1)
        r = jnp.dot(y, w_ref[...], preferred_element_type=F32)
        o_ref[0, rows, :] = _finish_rows(r, x_ref[0, rows, :], gated_gain)


def _out_projection(ya, yb, w, x, mod3, post_g):
    B, L, _ = x.shape
    tm = min(ROW_TILE, L)
    sub = min(SUB_ROWS, tm)
    wa = ya.shape[2]
    kern = functools.partial(_out_kernel, n_sub=tm // sub, sub=sub)
    return pl.pallas_call(
        kern,
        grid=(B, L // tm),
        in_specs=[
            pl.BlockSpec((1, tm, wa), lambda b, i: (b, i, 0)),
            pl.BlockSpec((1, tm, yb.shape[2]), lambda b, i: (b, i, 0)),
            _resident(w.shape, lambda b, i: (0, 0)),
            pl.BlockSpec((1, tm, D_MODEL), lambda b, i: (b, i, 0)),
            pl.BlockSpec((1, 1, 3 * D_MODEL), lambda b, i: (b, 0, 0)),
            _resident((1, D_MODEL), lambda b, i: (0, 0)),
        ],
        out_specs=pl.BlockSpec((1, tm, D_MODEL), lambda b, i: (b, i, 0)),
        out_shape=jax.ShapeDtypeStruct((B, L, D_MODEL), F32),
        compiler_params=_params(2),
        name="out_projection",
    )(ya, yb, w, x, mod3, post_g.reshape(1, D_MODEL))


def _pool_kernel(u_ref, up_ref, un_ref, gt_ref, pw_ref, ps_ref, w_ref, x_ref, mod_ref, g_ref,
                 o_ref, *, tm, n_sub, sub, seq_len):
    i = pl.program_id(1)
    gated_gain = mod_ref[0][:, 2 * D_MODEL:] * g_ref[...]

    for s in range(n_sub):
        r0 = s * sub
        rows = slice(r0, r0 + sub)
        t = i * tm + r0 + lax.broadcasted_iota(jnp.int32, (sub, 1), 0)
        mixed = []
        for g, w in enumerate(POOL_SIZES):
            cols = slice(g * POOL_GROUP, (g + 1) * POOL_GROUP)
            half = w // 2
            if s == 0:
                before = jnp.where(i > 0, up_ref[0, :, cols], 0.0)
            else:
                before = u_ref[0, r0 - POOL_HALO:r0, cols]
            if s == n_sub - 1:
                after = jnp.where(i < pl.num_programs(1) - 1, un_ref[0, :, cols], 0.0)
            else:
                after = u_ref[0, r0 + sub:r0 + sub + POOL_HALO, cols]
            u = u_ref[0, rows, cols]
            ext = jnp.concatenate([before, u, after], axis=0)
            tot = ext[1:] + ext[:-1]
            lo = 1
            span = 1
            while 2 * span < w:
                tot = tot[:-2 * span] + tot[2 * span:]
                lo += span
                span *= 2
            tot = tot[POOL_HALO - lo:POOL_HALO - lo + sub]
            cnt = jnp.minimum(t + half, seq_len) - jnp.maximum(t - half, 0)
            pooled = tot / cnt.astype(F32) - u
            mixed.append(jnp.dot(pooled.astype(BF16), pw_ref[g], preferred_element_type=F32))
        y = jnp.concatenate(mixed, axis=1) * ps_ref[...]
        y = (y * _silu(gt_ref[0, rows, :])).astype(BF16)
        r = jnp.dot(y, w_ref[...], preferred_element_type=F32)
        o_ref[0, rows, :] = _finish_rows(r, x_ref[0, rows, :], gated_gain)


def _pool_tail(u, gt, pool_w, pool_scale, w_out, x, mod3, post_g):
    B, L, _ = x.shape
    tm = min(ROW_TILE, L)
    sub = min(SUB_ROWS, tm)
    hb = tm // POOL_HALO
    n_hb = L // POOL_HALO
    kern = functools.partial(_pool_kernel, tm=tm, n_sub=tm // sub, sub=sub, seq_len=L)
    row_spec = pl.BlockSpec((1, tm, D_MODEL), lambda b, i: (b, i, 0))
    return pl.pallas_call(
        kern,
        grid=(B, L // tm),
        in_specs=[
            row_spec,
            pl.BlockSpec((1, POOL_HALO, D_MODEL), lambda b, i: (b, jnp.maximum(i * hb - 1, 0), 0)),
            pl.BlockSpec((1, POOL_HALO, D_MODEL),
                         lambda b, i: (b, jnp.minimum((i + 1) * hb, n_hb - 1), 0)),
            row_spec,
            _resident(pool_w.shape, lambda b, i: (0, 0, 0)),
            _resident((1, D_MODEL), lambda b, i: (0, 0)),
            _resident(w_out.shape, lambda b, i: (0, 0)),
            row_spec,
            pl.BlockSpec((1, 1, 3 * D_MODEL), lambda b, i: (b, 0, 0)),
            _resident((1, D_MODEL), lambda b, i: (0, 0)),
        ],
        out_specs=row_spec,
        out_shape=jax.ShapeDtypeStruct((B, L, D_MODEL), F32),
        compiler_params=_params(2),
        name="pool_tail",
    )(u, u, u, gt, pool_w, pool_scale.reshape(1, D_MODEL), w_out, x, mod3,
      post_g.reshape(1, D_MODEL))


def _rope_tables(seq_len):
    rows = seq_len // GRID_W
    inv = ROPE_THETA ** (-jnp.arange(AXIS_FREQS, dtype=F32) / AXIS_FREQS)
    row_ang = jnp.arange(rows, dtype=F32)[:, None] * inv
    col_ang = jnp.arange(GRID_W, dtype=F32)[:, None] * inv

    def table(fn, sign):
        shape = (rows, GRID_W, AXIS_FREQS)
        half = jnp.concatenate([jnp.broadcast_to(fn(row_ang)[:, None, :], shape),
                                jnp.broadcast_to(fn(col_ang)[None, :, :], shape)], axis=-1)
        return jnp.concatenate([sign * half, half], axis=-1).reshape(seq_len, HEAD_DIM)

    return table(jnp.cos, 1.0), table(jnp.sin, -1.0)


def kernel(x, c, ctx, c_ctx, ev_mod_w, ev_mod_b, ev_pre_g, ev_post_g, ev_w_in, ev_q_norm, ev_k_norm,
           ev_sink, ev_w_out, od_mod_w, od_mod_b, od_pre_g, od_post_g, od_w_in, od_pool_w,
           od_pool_scale, od_w_out):
    B, L, D = x.shape
    assert D == D_MODEL and L % GRID_W == 0
    assert ev_mod_w.shape[0] == 1 and od_mod_w.shape[0] == 1
    assert B + 1 <= MOD_ROWS
    ctx_row = B

    cvec = jnp.zeros((MOD_ROWS, D), F32).at[:B].set(c).at[ctx_row].set(c_ctx)
    mod_e = _modulation(cvec, ev_mod_w[0], ev_mod_b[0]).reshape(MOD_ROWS, 1, 3 * D)
    mod_o = _modulation(cvec, od_mod_w[0], od_mod_b[0]).reshape(MOD_ROWS, 1, 3 * D)

    w_in_e = ev_w_in[0].astype(BF16)
    norms = (ev_q_norm[0], ev_k_norm[0])
    n_kv = N_KV_A + N_KV_B

    def kv_plan(k_out, v_out, ka_op, kb_op):
        return ([(ka_op, k_out, TO_HEAD, h) for h in range(N_KV_A)]
                + [(OP_RAW, v_out, TO_SLABS_T, h) for h in range(N_KV_A)]
                + [(kb_op, k_out, TO_HEAD, N_KV_A + h) for h in range(N_KV_B)]
                + [(OP_RAW, v_out, TO_SLABS_T, N_KV_A + h) for h in range(N_KV_B)])

    plan = tuple([(OP_QNORM_ROPE, 0, TO_COLS, h * HEAD_DIM) for h in range(N_HEADS_A)]
                 + [(OP_ROPE, 0, TO_COLS, (N_HEADS_A + h) * HEAD_DIM) for h in range(N_HEADS_B)]
                 + kv_plan(1, 2, OP_KNORM_ROPE, OP_ROPE)
                 + [(OP_SILU, 3, TO_COLS, h * HEAD_DIM) for h in range(Q_W // HEAD_DIM)])
    q, k, vt, gz = _projection(
        x, mod_e, None, ev_pre_g[0], w_in_e, plan=plan,
        outs=((TO_COLS, Q_W, BF16), (TO_HEAD, n_kv, BF16), (TO_SLABS_T, n_kv, BF16), (TO_COLS, Q_W, F32)),
        rope=_rope_tables(L), norms=norms, name="in_projection_even")
    kc, vct = _projection(
        ctx, mod_e, ctx_row, ev_pre_g[0], w_in_e[:, Q_W:Q_W + KV_W],
        plan=tuple(kv_plan(0, 1, OP_KNORM, OP_RAW)),
        outs=((TO_HEAD, n_kv, BF16), (TO_SLABS_T, n_kv, BF16)), norms=norms, name="in_projection_ctx")

    ya = _dense_attention(q, k, vt, kc, vct, gz, *norms)
    yb = _window_attention(q, k, vt, kc, vct, gz, ev_sink[0])
    x = _out_projection(ya, yb, ev_w_out[0].astype(BF16), x, mod_e, ev_post_g[0])

    u, gt = _projection(
        x, mod_o, None, od_pre_g[0], od_w_in[0].astype(BF16),
        plan=tuple((OP_RAW, col // D, TO_COLS, col % D) for col in range(0, 2 * D, HEAD_DIM)),
        outs=((TO_COLS, D, F32), (TO_COLS, D, F32)), name="in_projection_odd")
    return _pool_tail(u, gt, od_pool_w[0].astype(BF16), od_pool_scale[0], od_w_out[0].astype(BF16),
                      x, mod_o, od_post_g[0])
```

```python
import functools
import math

import jax
import jax.numpy as jnp
from jax import lax
from jax.experimental import pallas as pl
from jax.experimental.pallas import tpu as pltpu

D_MODEL = 2048
GRID_W = 64
HEAD_DIM = 128
AXIS_FREQS = HEAD_DIM // 4
ROPE_THETA = 10000.0
N_HEADS_A = 8
N_KV_A = 2
N_HEADS_B = 8
N_KV_B = 2
GROUP = N_HEADS_A // N_KV_A
WINDOW = 128
ATTN_SCALE = HEAD_DIM ** -0.5
NEG_INF = -1e30
Q_W = (N_HEADS_A + N_HEADS_B) * HEAD_DIM
KV_A_W = N_KV_A * HEAD_DIM
KV_B_W = N_KV_B * HEAD_DIM
KV_W = 2 * KV_A_W + 2 * KV_B_W
POOL_SIZES = (2, 4, 8, 16)
POOL_GROUP = D_MODEL // len(POOL_SIZES)
POOL_HALO = 8
EPS = 1e-6

LANES = 128
SUBLANES = 8
VMEM_LIMIT_BYTES = 56 * 1024 * 1024

MOD_ROWS = SUBLANES
MOD_TN = 512
ROW_TILE = 512
SUB_ROWS = 256
COL_CHUNK = 512
ATTN_TQ = 1024
ATTN_TK = 1024
DENSE_UNROLL = 4
WINDOW_TQ = 1024
WINDOW_SUB = 128
LOG2_E = math.log2(math.e)
EXP2_SCALE = ATTN_SCALE * LOG2_E
BOUND_SLACK = 1.0 + 2.0 ** -7
SAFE_LOGIT = 50.0

OP_RAW, OP_ROPE, OP_QNORM_ROPE, OP_KNORM_ROPE, OP_KNORM, OP_SILU = range(6)
TO_COLS, TO_HEAD, TO_SLABS_T = range(3)
SLAB = LANES

F32 = jnp.float32
BF16 = jnp.bfloat16


def _params(n_axes):
    return pltpu.CompilerParams(
        dimension_semantics=("arbitrary",) * n_axes,
        vmem_limit_bytes=VMEM_LIMIT_BYTES,
    )


def _resident(block_shape, index_map):
    return pl.BlockSpec(block_shape, index_map, pipeline_mode=pl.Buffered(1))


def _silu(v):
    return v * jax.nn.sigmoid(v)


def _rms(v, gain):
    ms = jnp.mean(v * v, axis=1, keepdims=True)
    return (v * lax.rsqrt(ms + EPS)) * gain


def _mod_kernel(c_ref, we_ref, wo_ref, be_ref, bo_ref, o_ref):
    a = _silu(c_ref[...]).astype(BF16)

    @pl.when(pl.program_id(0) == 0)
    def _():
        o_ref[0] = jnp.dot(a, we_ref[...].astype(BF16), preferred_element_type=F32) + be_ref[...]

    @pl.when(pl.program_id(0) == 1)
    def _():
        o_ref[0] = jnp.dot(a, wo_ref[...].astype(BF16), preferred_element_type=F32) + bo_ref[...]


def _modulation(cvec, w_even, b_even, w_odd, b_odd):
    n = w_even.shape[1]
    nt = n // MOD_TN
    even = lambda l, j: (0, jnp.where(l == 0, j, nt - 1))
    odd = lambda l, j: (0, jnp.where(l == 1, j, 0))
    return pl.pallas_call(
        _mod_kernel,
        grid=(2, nt),
        in_specs=[
            pl.BlockSpec((MOD_ROWS, D_MODEL), lambda l, j: (0, 0)),
            pl.BlockSpec((D_MODEL, MOD_TN), even),
            pl.BlockSpec((D_MODEL, MOD_TN), odd),
            pl.BlockSpec((1, MOD_TN), even),
            pl.BlockSpec((1, MOD_TN), odd),
        ],
        out_specs=pl.BlockSpec((1, MOD_ROWS, MOD_TN), lambda l, j: (l, 0, j)),
        out_shape=jax.ShapeDtypeStruct((2, MOD_ROWS, n), F32),
        compiler_params=_params(2),
        name="modulation",
    )(cvec, w_even, w_odd, b_even.reshape(1, n), b_odd.reshape(1, n))


def _proj_kernel(*refs, plan, n_sub, sub, use_rope, use_norm):
    x_ref, mod_ref, g_ref, w_ref = refs[:4]
    pos = 4
    if use_rope:
        cos_ref, sin_ref = refs[pos:pos + 2]
        pos += 2
    if use_norm:
        qn_ref, kn_ref = refs[pos:pos + 2]
        pos += 2
    out_refs = refs[pos:]

    mrow = mod_ref[0]
    shift = mrow[:, :D_MODEL]
    gain = g_ref[...] * (1.0 + mrow[:, D_MODEL:2 * D_MODEL])
    heads_per_chunk = COL_CHUNK // HEAD_DIM
    slabs_per_sub = sub // SLAB

    for s in range(n_sub):
        rows = slice(s * sub, (s + 1) * sub)
        xs = x_ref[0, rows, :]
        ms = jnp.mean(xs * xs, axis=1, keepdims=True)
        h = ((xs * lax.rsqrt(ms + EPS)) * gain + shift).astype(BF16)
        if use_rope:
            cs = cos_ref[rows, :]
            sn = sin_ref[rows, :]
        for c in range(len(plan) // heads_per_chunk):
            acc = jnp.dot(h, w_ref[:, c * COL_CHUNK:(c + 1) * COL_CHUNK],
                          preferred_element_type=F32)
            for hh in range(heads_per_chunk):
                op, o_idx, kind, slot = plan[c * heads_per_chunk + hh]
                t = acc[:, hh * HEAD_DIM:(hh + 1) * HEAD_DIM]
                if op == OP_SILU:
                    t = _silu(t)
                elif op == OP_QNORM_ROPE:
                    t = _rms(t, qn_ref[...])
                elif op in (OP_KNORM_ROPE, OP_KNORM):
                    t = _rms(t, kn_ref[...])
                if op in (OP_ROPE, OP_QNORM_ROPE, OP_KNORM_ROPE):
                    t = t * cs + pltpu.roll(t, HEAD_DIM // 2, 1) * sn
                out = out_refs[o_idx]
                if kind == TO_COLS:
                    out[0, rows, slot:slot + HEAD_DIM] = t.astype(out.dtype)
                elif kind == TO_HEAD:
                    out[0, slot, rows, :] = t.astype(out.dtype)
                else:
                    for r in range(slabs_per_sub):
                        out[0, slot, s * slabs_per_sub + r] = (
                            t[r * SLAB:(r + 1) * SLAB, :].T.astype(out.dtype))


def _projection(x, mod3, mod_row, g, w, *, plan, outs, rope=None, norms=None, name):
    B, L, _ = x.shape
    n_total = w.shape[1]
    assert n_total == len(plan) * HEAD_DIM and n_total % COL_CHUNK == 0
    tm = min(ROW_TILE, L)
    sub = min(SUB_ROWS, tm)
    assert L % tm == 0 and tm % sub == 0 and sub % SLAB == 0
    mod_map = (lambda b, i: (b, 0, 0)) if mod_row is None else (lambda b, i: (mod_row, 0, 0))
    in_specs = [
        pl.BlockSpec((1, tm, D_MODEL), lambda b, i: (b, i, 0)),
        pl.BlockSpec((1, 1, 3 * D_MODEL), mod_map),
        _resident((1, D_MODEL), lambda b, i: (0, 0)),
        _resident((D_MODEL, n_total), lambda b, i: (0, 0)),
    ]
    args = [x, mod3, g.reshape(1, D_MODEL), w]
    if rope is not None:
        in_specs += [pl.BlockSpec((tm, HEAD_DIM), lambda b, i: (i, 0))] * 2
        args += list(rope)
    if norms is not None:
        in_specs += [_resident((1, HEAD_DIM), lambda b, i: (0, 0))] * 2
        args += [n.reshape(1, HEAD_DIM) for n in norms]
    out_specs, out_shape = [], []
    for kind, n, dt in outs:
        if kind == TO_COLS:
            out_specs.append(pl.BlockSpec((1, tm, n), lambda b, i: (b, i, 0)))
            out_shape.append(jax.ShapeDtypeStruct((B, L, n), dt))
        elif kind == TO_HEAD:
            out_specs.append(pl.BlockSpec((1, n, tm, HEAD_DIM), lambda b, i: (b, 0, i, 0)))
            out_shape.append(jax.ShapeDtypeStruct((B, n, L, HEAD_DIM), dt))
        else:
            out_specs.append(pl.BlockSpec((1, n, tm // SLAB, HEAD_DIM, SLAB),
                                          lambda b, i: (b, 0, i, 0, 0)))
            out_shape.append(jax.ShapeDtypeStruct((B, n, L // SLAB, HEAD_DIM, SLAB), dt))
    kern = functools.partial(
        _proj_kernel, plan=plan, n_sub=tm // sub, sub=sub,
        use_rope=rope is not None, use_norm=norms is not None)
    return pl.pallas_call(
        kern,
        grid=(B, L // tm),
        in_specs=in_specs,
        out_specs=out_specs,
        out_shape=out_shape,
        compiler_params=_params(2),
        name=name,
    )(*args)


def _qk(q, k):
    return lax.dot_general(q, k, (((1,), (1,)), ((), ())), preferred_element_type=F32)


def _key_norm2(k):
    kf = k.astype(F32)
    return jnp.max(jnp.sum(kf * kf, axis=1, keepdims=True))


def _concat_slabs(vt_ref, first, count):
    return jnp.concatenate([vt_ref[0, 0, first + t] for t in range(count)], axis=1)


def _dense_kernel(qn_ref, kn_ref, q_ref, k_ref, vt_ref, kc_ref, vct_ref, gz_ref, y_ref,
                  m_sc, l_sc, acc_sc, *, tq, tk, n_steps):
    nq = GROUP * tq
    slabs = tk // SLAB

    def key_tile(j):
        return k_ref[0, 0, pl.ds(pl.multiple_of(j * tk, tk), tk), :]

    def vt_tile(j):
        return _concat_slabs(vt_ref, j * slabs, slabs)

    kc = kc_ref[0, 0]
    vct = _concat_slabs(vct_ref, 0, vct_ref.shape[2])

    l_sc[...] = jnp.zeros(l_sc.shape, F32)
    acc_sc[...] = jnp.zeros(acc_sc.shape, F32)
    q = [q_ref[0, :, g * HEAD_DIM:(g + 1) * HEAD_DIM] for g in range(GROUP)]

    bound = (ATTN_SCALE * HEAD_DIM * BOUND_SLACK) * jnp.max(jnp.abs(qn_ref[...])) * jnp.max(jnp.abs(kn_ref[...]))
    safe = bound <= SAFE_LOGIT

    @pl.when(safe)
    def _():
        q_all = jnp.concatenate(q, axis=0)

        def attend(k, vt):
            p = jnp.exp2(_qk(k, q_all) * EXP2_SCALE)
            l_sc[...] += jnp.sum(p.reshape(-1, SUBLANES, nq), axis=0)
            acc_sc[...] += jnp.dot(vt, p.astype(BF16), preferred_element_type=F32)

        n_loop = n_steps // DENSE_UNROLL

        def body(i, carry):
            for u in range(DENSE_UNROLL):
                j = i * DENSE_UNROLL + u
                attend(key_tile(j), vt_tile(j))
            return carry

        lax.fori_loop(0, n_loop, body, 0)
        for j in range(n_loop * DENSE_UNROLL, n_steps):
            attend(key_tile(j), vt_tile(j))
        attend(kc, vct)

    @pl.when(jnp.logical_not(safe))
    def _():
        m_sc[...] = jnp.full(m_sc.shape, -jnp.inf, F32)

        def attend(k, vt):
            for g in range(GROUP):
                st = _qk(k, q[g])
                m_prev = m_sc[g]
                m_new = jnp.maximum(m_prev, jnp.max(st, axis=0, keepdims=True))
                alpha = jnp.exp2((m_prev - m_new) * EXP2_SCALE)
                p = jnp.exp2((st - m_new) * EXP2_SCALE)
                cols = slice(g * tq, (g + 1) * tq)
                l_sc[0:1, cols] = alpha * l_sc[0:1, cols] + jnp.sum(p, axis=0, keepdims=True)
                acc_sc[:, cols] = alpha * acc_sc[:, cols] + jnp.dot(vt, p.astype(BF16),
                                                                     preferred_element_type=F32)
                m_sc[g] = m_new

        def body(j, carry):
            attend(key_tile(j), vt_tile(j))
            return carry

        lax.fori_loop(0, n_steps, body, 0)
        attend(kc, vct)

    o = (acc_sc[...] / jnp.sum(l_sc[...], axis=0, keepdims=True)).T
    o = jnp.concatenate([o[g * tq:(g + 1) * tq] for g in range(GROUP)], axis=1)
    y_ref[0] = (o * gz_ref[0]).astype(y_ref.dtype)


def _kv_specs(L, C, head0):
    return [
        pl.BlockSpec((1, 1, L, HEAD_DIM), lambda b, h, i: (b, head0 + h, 0, 0)),
        pl.BlockSpec((1, 1, L // SLAB, HEAD_DIM, SLAB), lambda b, h, i: (b, head0 + h, 0, 0, 0)),
        pl.BlockSpec((1, 1, C, HEAD_DIM), lambda b, h, i: (b, head0 + h, 0, 0)),
        pl.BlockSpec((1, 1, C // SLAB, HEAD_DIM, SLAB), lambda b, h, i: (b, head0 + h, 0, 0, 0)),
    ]


def _dense_attention(q, k, vt, kc, vct, gz, q_norm, k_norm):
    B, L, _ = q.shape
    C = kc.shape[2]
    tq = min(ATTN_TQ, L)
    tk = min(ATTN_TK, L)
    assert L % tq == 0 and L % tk == 0 and tk % SLAB == 0 and C % SLAB == 0
    gw = GROUP * HEAD_DIM
    kern = functools.partial(_dense_kernel, tq=tq, tk=tk, n_steps=L // tk)
    qz_spec = pl.BlockSpec((1, tq, gw), lambda b, h, i: (b, i, h))
    gain_spec = _resident((1, HEAD_DIM), lambda b, h, i: (0, 0))
    return pl.pallas_call(
        kern,
        grid=(B, N_KV_A, L // tq),
        in_specs=[gain_spec, gain_spec, qz_spec] + _kv_specs(L, C, 0) + [qz_spec],
        out_specs=qz_spec,
        out_shape=jax.ShapeDtypeStruct((B, L, N_HEADS_A * HEAD_DIM), BF16),
        scratch_shapes=[
            pltpu.VMEM((GROUP, 1, tq), F32),
            pltpu.VMEM((SUBLANES, GROUP * tq), F32),
            pltpu.VMEM((HEAD_DIM, GROUP * tq), F32),
        ],
        compiler_params=_params(3),
        name="dense_attention",
    )(q_norm.reshape(1, HEAD_DIM), k_norm.reshape(1, HEAD_DIM), q, k, vt, kc, vct, gz)


def _window_kernel(sink_ref, q_ref, k_ref, vt_ref, kc_ref, vct_ref, gz_ref, y_ref, k2_sc,
                   *, tq, sub, seq_len):
    h = pl.program_id(1)
    i = pl.program_id(2)
    band = sub + 2 * WINDOW
    nq = GROUP * sub
    n_sub = tq // sub

    kc = kc_ref[0, 0]
    vct = _concat_slabs(vct_ref, 0, vct_ref.shape[2])

    @pl.when(i == 0)
    def _():
        def kbody(j, best):
            off = pl.multiple_of(j * ATTN_TK, ATTN_TK)
            return jnp.maximum(best, _key_norm2(k_ref[0, 0, pl.ds(off, ATTN_TK), :]))

        k2_sc[0] = lax.fori_loop(0, seq_len // ATTN_TK, kbody, _key_norm2(kc))

    lane_head = lax.broadcasted_iota(jnp.int32, (1, nq), 1) // sub
    sink = jnp.zeros((1, nq), F32)
    sink_max = sink_ref[h * GROUP]
    for g in range(GROUP):
        sink = jnp.where(lane_head == g, sink_ref[h * GROUP + g], sink)
        sink_max = jnp.maximum(sink_max, sink_ref[h * GROUP + g])
    sink = sink * LOG2_E
    rel = (lax.broadcasted_iota(jnp.int32, (band, sub), 0)
           - lax.broadcasted_iota(jnp.int32, (band, sub), 1))
    q_all = [jnp.concatenate([q_ref[0, c * sub:(c + 1) * sub, g * HEAD_DIM:(g + 1) * HEAD_DIM]
                              for g in range(GROUP)], axis=0) for c in range(n_sub)]
    q2 = functools.reduce(jnp.maximum, [_key_norm2(qc) for qc in q_all])
    safe = jnp.logical_and(
        q2 * k2_sc[0] * (ATTN_SCALE * ATTN_SCALE * BOUND_SLACK) <= SAFE_LOGIT * SAFE_LOGIT,
        sink_max <= SAFE_LOGIT)

    def scores(c):
        start = i * tq + c * sub
        kstart = pl.multiple_of(jnp.clip(start - WINDOW, 0, seq_len - band), WINDOW)
        bias = jnp.where(jnp.abs(rel + (kstart - start)) <= WINDOW, 0.0, NEG_INF)
        s = (_qk(k_ref[0, 0, pl.ds(kstart, band), :], q_all[c]) * EXP2_SCALE
             + jnp.concatenate([bias] * GROUP, axis=1))
        sc = _qk(kc, q_all[c]) * EXP2_SCALE
        return s, sc, kstart

    def finish(c, s, sc, kstart, exact_max):
        rows = slice(c * sub, (c + 1) * sub)
        if exact_max:
            m = jnp.maximum(jnp.maximum(jnp.max(s, axis=0, keepdims=True),
                                        jnp.max(sc, axis=0, keepdims=True)), sink)
            s, sc, sink_c = s - m, sc - m, sink - m
        else:
            sink_c = sink
        p = jnp.exp2(s)
        pc = jnp.exp2(sc)
        denom = (jnp.sum(p, axis=0, keepdims=True) + jnp.sum(pc, axis=0, keepdims=True)
                 + jnp.exp2(sink_c))
        vbt = _concat_slabs(vt_ref, kstart // SLAB, band // SLAB)
        o = (jnp.dot(vbt, p.astype(BF16), preferred_element_type=F32)
             + jnp.dot(vct, pc.astype(BF16), preferred_element_type=F32)) / denom
        o = o.T
        o = jnp.concatenate([o[g * sub:(g + 1) * sub] for g in range(GROUP)], axis=1)
        y_ref[0, rows, :] = (o * gz_ref[0, rows, :]).astype(y_ref.dtype)

    def all_sub_tiles(exact_max):
        nxt = scores(0)
        for c in range(n_sub):
            cur = nxt
            if c + 1 < n_sub:
                nxt = scores(c + 1)
            finish(c, *cur, exact_max)

    @pl.when(safe)
    def _():
        all_sub_tiles(exact_max=False)

    @pl.when(jnp.logical_not(safe))
    def _():
        all_sub_tiles(exact_max=True)


def _window_attention(q, k, vt, kc, vct, gz, sink):
    B, L, _ = q.shape
    C = kc.shape[2]
    sub = min(WINDOW_SUB, L - 2 * WINDOW)
    tq = min(WINDOW_TQ, L)
    assert L % tq == 0 and tq % sub == 0 and L >= sub + 2 * WINDOW and L % ATTN_TK == 0
    assert WINDOW % SLAB == 0 and sub % SLAB == 0 and C % SLAB == 0
    gw = GROUP * HEAD_DIM
    q_col = N_HEADS_A // GROUP
    kern = functools.partial(_window_kernel, tq=tq, sub=sub, seq_len=L)
    qz_spec = pl.BlockSpec((1, tq, gw), lambda b, h, i: (b, i, q_col + h))
    return pl.pallas_call(
        kern,
        grid=(B, N_KV_B, L // tq),
        in_specs=[pl.BlockSpec(memory_space=pltpu.SMEM), qz_spec] + _kv_specs(L, C, N_KV_A) + [qz_spec],
        out_specs=pl.BlockSpec((1, tq, gw), lambda b, h, i: (b, i, h)),
        out_shape=jax.ShapeDtypeStruct((B, L, N_HEADS_B * HEAD_DIM), BF16),
        scratch_shapes=[pltpu.SMEM((1,), F32)],
        compiler_params=_params(3),
        name="window_attention",
    )(sink, q, k, vt, kc, vct, gz)


def _finish_rows(r, x_rows, gated_gain):
    return x_rows + _rms(r, gated_gain)


def _out_kernel(ya_ref, yb_ref, w_ref, x_ref, mod_ref, g_ref, o_ref, *, n_sub, sub):
    gated_gain = mod_ref[0][:, 2 * D_MODEL:] * g_ref[...]
    for s in range(n_sub):
        rows = slice(s * sub, (s + 1) * sub)
        y = jnp.concatenate([ya_ref[0, rows, :], yb_ref[0, rows, :]], axis=1)
        r = jnp.dot(y, w_ref[...], preferred_element_type=F32)
        o_ref[0, rows, :] = _finish_rows(r, x_ref[0, rows, :], gated_gain)


def _out_projection(ya, yb, w, x, mod3, post_g):
    B, L, _ = x.shape
    tm = min(ROW_TILE, L)
    sub = min(SUB_ROWS, tm)
    wa = ya.shape[2]
    kern = functools.partial(_out_kernel, n_sub=tm // sub, sub=sub)
    return pl.pallas_call(
        kern,
        grid=(B, L // tm),
        in_specs=[
            pl.BlockSpec((1, tm, wa), lambda b, i: (b, i, 0)),
            pl.BlockSpec((1, tm, yb.shape[2]), lambda b, i: (b, i, 0)),
            _resident(w.shape, lambda b, i: (0, 0)),
            pl.BlockSpec((1, tm, D_MODEL), lambda b, i: (b, i, 0)),
            pl.BlockSpec((1, 1, 3 * D_MODEL), lambda b, i: (b, 0, 0)),
            _resident((1, D_MODEL), lambda b, i: (0, 0)),
        ],
        out_specs=pl.BlockSpec((1, tm, D_MODEL), lambda b, i: (b, i, 0)),
        out_shape=jax.ShapeDtypeStruct((B, L, D_MODEL), F32),
        compiler_params=_params(2),
        name="out_projection",
    )(ya, yb, w, x, mod3, post_g.reshape(1, D_MODEL))


def _pool_kernel(u_ref, up_ref, un_ref, gt_ref, pw_ref, ps_ref, w_ref, x_ref, mod_ref, g_ref,
                 o_ref, *, tm, n_sub, sub, seq_len):
    i = pl.program_id(1)
    gated_gain = mod_ref[0][:, 2 * D_MODEL:] * g_ref[...]

    for s in range(n_sub):
        r0 = s * sub
        rows = slice(r0, r0 + sub)
        t = i * tm + r0 + lax.broadcasted_iota(jnp.int32, (sub, 1), 0)
        mixed = []
        for g, w in enumerate(POOL_SIZES):
            cols = slice(g * POOL_GROUP, (g + 1) * POOL_GROUP)
            half = w // 2
            if s == 0:
                before = jnp.where(i > 0, up_ref[0, :, cols], 0.0)
            else:
                before = u_ref[0, r0 - POOL_HALO:r0, cols]
            if s == n_sub - 1:
                after = jnp.where(i < pl.num_programs(1) - 1, un_ref[0, :, cols], 0.0)
            else:
                after = u_ref[0, r0 + sub:r0 + sub + POOL_HALO, cols]
            u = u_ref[0, rows, cols]
            ext = jnp.concatenate([before, u, after], axis=0)
            tot = ext[1:] + ext[:-1]
            lo = 1
            span = 1
            while 2 * span < w:
                tot = tot[:-2 * span] + tot[2 * span:]
                lo += span
                span *= 2
            tot = tot[POOL_HALO - lo:POOL_HALO - lo + sub]
            cnt = jnp.minimum(t + half, seq_len) - jnp.maximum(t - half, 0)
            pooled = tot / cnt.astype(F32) - u
            mixed.append(jnp.dot(pooled.astype(BF16), pw_ref[g], preferred_element_type=F32))
        y = jnp.concatenate(mixed, axis=1) * ps_ref[...]
        y = (y * _silu(gt_ref[0, rows, :])).astype(BF16)
        r = jnp.dot(y, w_ref[...], preferred_element_type=F32)
        o_ref[0, rows, :] = _finish_rows(r, x_ref[0, rows, :], gated_gain)


def _pool_tail(u, gt, pool_w, pool_scale, w_out, x, mod3, post_g):
    B, L, _ = x.shape
    tm = min(ROW_TILE, L)
    sub = min(SUB_ROWS, tm)
    hb = tm // POOL_HALO
    n_hb = L // POOL_HALO
    kern = functools.partial(_pool_kernel, tm=tm, n_sub=tm // sub, sub=sub, seq_len=L)
    row_spec = pl.BlockSpec((1, tm, D_MODEL), lambda b, i: (b, i, 0))
    return pl.pallas_call(
        kern,
        grid=(B, L // tm),
        in_specs=[
            row_spec,
            pl.BlockSpec((1, POOL_HALO, D_MODEL), lambda b, i: (b, jnp.maximum(i * hb - 1, 0), 0)),
            pl.BlockSpec((1, POOL_HALO, D_MODEL),
                         lambda b, i: (b, jnp.minimum((i + 1) * hb, n_hb - 1), 0)),
            row_spec,
            _resident(pool_w.shape, lambda b, i: (0, 0, 0)),
            _resident((1, D_MODEL), lambda b, i: (0, 0)),
            _resident(w_out.shape, lambda b, i: (0, 0)),
            row_spec,
            pl.BlockSpec((1, 1, 3 * D_MODEL), lambda b, i: (b, 0, 0)),
            _resident((1, D_MODEL), lambda b, i: (0, 0)),
        ],
        out_specs=row_spec,
        out_shape=jax.ShapeDtypeStruct((B, L, D_MODEL), F32),
        compiler_params=_params(2),
        name="pool_tail",
    )(u, u, u, gt, pool_w, pool_scale.reshape(1, D_MODEL), w_out, x, mod3,
      post_g.reshape(1, D_MODEL))


def _rope_tables(seq_len):
    rows = seq_len // GRID_W
    inv = ROPE_THETA ** (-jnp.arange(AXIS_FREQS, dtype=F32) / AXIS_FREQS)
    row_ang = jnp.arange(rows, dtype=F32)[:, None] * inv
    col_ang = jnp.arange(GRID_W, dtype=F32)[:, None] * inv

    def table(fn, sign):
        shape = (rows, GRID_W, AXIS_FREQS)
        half = jnp.concatenate([jnp.broadcast_to(fn(row_ang)[:, None, :], shape),
                                jnp.broadcast_to(fn(col_ang)[None, :, :], shape)], axis=-1)
        return jnp.concatenate([sign * half, half], axis=-1).reshape(seq_len, HEAD_DIM)

    return table(jnp.cos, 1.0), table(jnp.sin, -1.0)


def kernel(x, c, ctx, c_ctx, ev_mod_w, ev_mod_b, ev_pre_g, ev_post_g, ev_w_in, ev_q_norm, ev_k_norm,
           ev_sink, ev_w_out, od_mod_w, od_mod_b, od_pre_g, od_post_g, od_w_in, od_pool_w,
           od_pool_scale, od_w_out):
    B, L, D = x.shape
    assert D == D_MODEL and L % GRID_W == 0
    assert ev_mod_w.shape[0] == 1 and od_mod_w.shape[0] == 1
    assert B + 1 <= MOD_ROWS
    ctx_row = B

    cvec = jnp.zeros((MOD_ROWS, D), F32).at[:B].set(c).at[ctx_row].set(c_ctx)
    mod = _modulation(cvec, ev_mod_w[0], ev_mod_b[0], od_mod_w[0], od_mod_b[0])
    mod_e = mod[0].reshape(MOD_ROWS, 1, 3 * D)
    mod_o = mod[1].reshape(MOD_ROWS, 1, 3 * D)

    w_in_e = ev_w_in[0].astype(BF16)
    norms = (ev_q_norm[0], ev_k_norm[0])
    n_kv = N_KV_A + N_KV_B

    def kv_plan(k_out, v_out, ka_op, kb_op):
        return ([(ka_op, k_out, TO_HEAD, h) for h in range(N_KV_A)]
                + [(OP_RAW, v_out, TO_SLABS_T, h) for h in range(N_KV_A)]
                + [(kb_op, k_out, TO_HEAD, N_KV_A + h) for h in range(N_KV_B)]
                + [(OP_RAW, v_out, TO_SLABS_T, N_KV_A + h) for h in range(N_KV_B)])

    plan = tuple([(OP_QNORM_ROPE, 0, TO_COLS, h * HEAD_DIM) for h in range(N_HEADS_A)]
                 + [(OP_ROPE, 0, TO_COLS, (N_HEADS_A + h) * HEAD_DIM) for h in range(N_HEADS_B)]
                 + kv_plan(1, 2, OP_KNORM_ROPE, OP_ROPE)
                 + [(OP_SILU, 3, TO_COLS, h * HEAD_DIM) for h in range(Q_W // HEAD_DIM)])
    q, k, vt, gz = _projection(
        x, mod_e, None, ev_pre_g[0], w_in_e, plan=plan,
        outs=((TO_COLS, Q_W, BF16), (TO_HEAD, n_kv, BF16), (TO_SLABS_T, n_kv, BF16), (TO_COLS, Q_W, F32)),
        rope=_rope_tables(L), norms=norms, name="in_projection_even")
    kc, vct = _projection(
        ctx, mod_e, ctx_row, ev_pre_g[0], w_in_e[:, Q_W:Q_W + KV_W],
        plan=tuple(kv_plan(0, 1, OP_KNORM, OP_RAW)),
        outs=((TO_HEAD, n_kv, BF16), (TO_SLABS_T, n_kv, BF16)), norms=norms, name="in_projection_ctx")

    ya = _dense_attention(q, k, vt, kc, vct, gz, *norms)
    yb = _window_attention(q, k, vt, kc, vct, gz, ev_sink[0])
    x = _out_projection(ya, yb, ev_w_out[0].astype(BF16), x, mod_e, ev_post_g[0])

    u, gt = _projection(
        x, mod_o, None, od_pre_g[0], od_w_in[0].astype(BF16),
        plan=tuple((OP_RAW, col // D, TO_COLS, col % D) for col in range(0, 2 * D, HEAD_DIM)),
        outs=((TO_COLS, D, F32), (TO_COLS, D, F32)), name="in_projection_odd")
    return _pool_tail(u, gt, od_pool_w[0].astype(BF16), od_pool_scale[0], od_w_out[0].astype(BF16),
                      x, mod_o, od_post_g[0])
```
